```python
import math, functools
import jax, jax.numpy as jnp
from jax import lax
import numpy as np

D_MODEL = 1024
BATCH = 8
SEQ = 4096
DEPTH = 1
DEC_BATCH = 128
DEC_SEQ = 1
PAST_LEN = 16384
PAGE_SIZE = 128

MEM_LEN = 256
CONV_CH = 512
CONV_WIDTH = 31
SWA_HEADS = 8
SWA_KV_HEADS = 2
SWA_GROUP = SWA_HEADS // SWA_KV_HEADS
SWA_HEAD_DIM = 64
WINDOW = 128
BLOCK = WINDOW
SWA_SCALE = SWA_HEAD_DIM ** -0.5
MEM_HEADS = 4
MEM_HEAD_DIM = 128
MEM_SCALE = MEM_HEAD_DIM ** -0.5
REL_BUCKETS = 32
REL_MAX_DIST = 128
PEER_HEADS = 8
PEER_N_KEYS = 128
PEER_N_EXPERTS = PEER_N_KEYS * PEER_N_KEYS
PEER_DK = 128
PEER_DK_HALF = PEER_DK // 2
PEER_TOPK = 16
PEER_BLOCK = 128
N_BRANCH = 3
EPS = 1e-6
NEG_INF = -1e30

W_GLU = 2 * CONV_CH
W_Q = SWA_HEADS * SWA_HEAD_DIM
W_KV = SWA_KV_HEADS * SWA_HEAD_DIM
W_QM = MEM_HEADS * MEM_HEAD_DIM
W_GATE = N_BRANCH * D_MODEL
IN_COLS = W_GLU + W_Q + 2 * W_KV + W_QM + W_GATE
IN_SPLITS = (W_GLU, W_GLU + W_Q, W_GLU + W_Q + W_KV, W_GLU + W_Q + 2 * W_KV, W_GLU + W_Q + 2 * W_KV + W_QM)

kernel_name = 'hybrid_conv_swa_mem_peer_decode_step'


def rmsnorm(x, g):
    xf = x.astype(jnp.float32)
    y = xf * lax.rsqrt(jnp.mean(xf * xf, axis=-1, keepdims=True) + EPS)
    return (y * g.astype(jnp.float32)).astype(x.dtype)


def layernorm(x, g, b):
    xf = x.astype(jnp.float32)
    mu = jnp.mean(xf, axis=-1, keepdims=True)
    var = jnp.mean(jnp.square(xf - mu), axis=-1, keepdims=True)
    y = (xf - mu) * lax.rsqrt(var + EPS)
    return (y * g.astype(jnp.float32) + b.astype(jnp.float32)).astype(x.dtype)


def rel_bucket(dist):
    n = jnp.maximum(dist, 0)
    max_exact = REL_BUCKETS // 2
    nf = jnp.maximum(n, 1).astype(jnp.float32)
    large = max_exact + (jnp.log(nf / max_exact) / math.log(REL_MAX_DIST / max_exact) * (REL_BUCKETS - max_exact)).astype(jnp.int32)
    large = jnp.minimum(large, REL_BUCKETS - 1)
    return jnp.where(n < max_exact, n, large)


def rel_bias(dist, table):
    b = jnp.moveaxis(table[rel_bucket(dist)], -1, 0).astype(jnp.float32)
    return b.reshape(SWA_KV_HEADS, SWA_GROUP, *dist.shape)


def sink_softmax(s, mask, sinks):
    sink = sinks.astype(jnp.float32).reshape(SWA_KV_HEADS, SWA_GROUP, 1, 1)
    s = jnp.where(mask, s, NEG_INF)
    m = jnp.maximum(jnp.max(s, axis=-1, keepdims=True), sink)
    p = jnp.exp(s - m)
    return p / (jnp.sum(p, axis=-1, keepdims=True) + jnp.exp(sink - m))


def conformer_conv(u, buf, dw_w, dw_b, ln_g, ln_b, w_pw):
    full = jnp.concatenate([buf.astype(u.dtype), u], axis=1)
    y = lax.conv_general_dilated(full, dw_w[:, None, :].astype(u.dtype), window_strides=(1,), padding='VALID',
                                 dimension_numbers=('NWC', 'WIO', 'NWC'), feature_group_count=CONV_CH) + dw_b
    y = jax.nn.silu(layernorm(y, ln_g, ln_b))
    return y @ w_pw, full[:, -(CONV_WIDTH - 1):]


def swa_prompt(q, k, v, sinks, table):
    n, s = q.shape[0], q.shape[1]
    nb = s // BLOCK
    qb = q.reshape(n, nb, BLOCK, SWA_KV_HEADS, SWA_GROUP, SWA_HEAD_DIM)
    kb = k.reshape(n, nb, BLOCK, SWA_KV_HEADS, SWA_HEAD_DIM)
    vb = v.reshape(n, nb, BLOCK, SWA_KV_HEADS, SWA_HEAD_DIM)
    pad = ((0, 0), (1, 0), (0, 0), (0, 0), (0, 0))
    kk = jnp.concatenate([jnp.pad(kb[:, :-1], pad), kb], axis=2)
    vv = jnp.concatenate([jnp.pad(vb[:, :-1], pad), vb], axis=2)
    qi = jnp.arange(BLOCK)[:, None]
    ki = jnp.arange(2 * BLOCK)[None, :]
    dist = BLOCK + qi - ki
    band = (dist >= 0) & (dist <= WINDOW)
    has_prev = (jnp.arange(nb) > 0)[:, None, None] | (ki >= BLOCK)[None]
    mask = (band[None] & has_prev)[None, :, None, None]
    sc = jnp.einsum('bnqhgd,bnkhd->bnhgqk', qb, kk, preferred_element_type=jnp.float32) * SWA_SCALE + rel_bias(dist, table)
    p = sink_softmax(sc, mask, sinks)
    o = jnp.einsum('bnhgqk,bnkhd->bnqhgd', p.astype(v.dtype), vv)
    return o.reshape(n, s, SWA_HEADS * SWA_HEAD_DIM), k[:, -WINDOW:], v[:, -WINDOW:]


def swa_sample(q, k, v, cache_k, cache_v, sinks, table):
    n, L = q.shape[0], q.shape[1]
    kk = jnp.concatenate([cache_k.astype(k.dtype), k], axis=1)
    vv = jnp.concatenate([cache_v.astype(v.dtype), v], axis=1)
    qpos = PAST_LEN + jnp.arange(L)
    kpos = PAST_LEN - WINDOW + jnp.arange(WINDOW + L)
    dist = qpos[:, None] - kpos[None, :]
    mask = (dist >= 0) & (dist <= WINDOW)
    qg = q.reshape(n, L, SWA_KV_HEADS, SWA_GROUP, SWA_HEAD_DIM)
    sc = jnp.einsum('bqhgd,bkhd->bhgqk', qg, kk, preferred_element_type=jnp.float32) * SWA_SCALE + rel_bias(dist, table)
    p = sink_softmax(sc, mask, sinks)
    o = jnp.einsum('bhgqk,bkhd->bqhgd', p.astype(v.dtype), vv)
    return o.reshape(n, L, SWA_HEADS * SWA_HEAD_DIM), kk[:, L:], vv[:, L:]


def mem_attend(q, mk, mv):
    s = jnp.einsum('blhd,bmhd->bhlm', q, mk.astype(q.dtype), preferred_element_type=jnp.float32) * MEM_SCALE
    w = jax.nn.softmax(s, axis=-1)
    o = jnp.einsum('bhlm,bmhd->blhd', w.astype(q.dtype), mv.astype(q.dtype))
    return o.reshape(q.shape[0], q.shape[1], MEM_HEADS * MEM_HEAD_DIM)


def peer_ffn(h, w_q, keys, w_down, w_up):
    shape = h.shape
    t = h.reshape(-1, D_MODEL)
    T = t.shape[0]
    nblk = -(-T // PEER_BLOCK)
    t = jnp.pad(t, ((0, nblk * PEER_BLOCK - T), (0, 0))).reshape(nblk, PEER_BLOCK, D_MODEL)

    def one_block(xb):
        q = (xb @ w_q).reshape(PEER_BLOCK, PEER_HEADS, 2, PEER_DK_HALF)
        s = jnp.einsum('thcd,hcnd->thcn', q, keys, preferred_element_type=jnp.float32)
        sv, si = lax.top_k(s, PEER_TOPK)
        cand = (sv[:, :, 0, :, None] + sv[:, :, 1, None, :]).reshape(PEER_BLOCK, PEER_HEADS, PEER_TOPK * PEER_TOPK)
        cv, ci = lax.top_k(cand, PEER_TOPK)
        i0 = jnp.take_along_axis(si[:, :, 0], ci // PEER_TOPK, axis=-1)
        i1 = jnp.take_along_axis(si[:, :, 1], ci % PEER_TOPK, axis=-1)
        eid = i0 * PEER_N_KEYS + i1
        gate = jax.nn.softmax(cv, axis=-1)
        u = w_down[eid]
        act = jax.nn.gelu(jnp.einsum('td,thkd->thk', xb, u, preferred_element_type=jnp.float32), approximate=False)
        return jnp.einsum('thk,thkd->td', (gate * act).astype(xb.dtype), w_up[eid])

    out = lax.map(one_block, t).reshape(-1, D_MODEL)[:T]
    return out.reshape(shape)


def decoder_layer(x, conv_buf, attn_fn, mem_k, mem_v, norm1_g, w_in, conv_dw_w, conv_dw_b, conv_ln_g, conv_ln_b,
                  w_conv_out, w_swa_out, w_mem_out, w_out, norm2_g, peer_w_q, peer_keys, peer_w_down, peer_w_up):
    n, L = x.shape[0], x.shape[1]
    h = rmsnorm(x, norm1_g)
    z_glu, q, k, v, qm, gate_logits = jnp.split(h @ w_in, IN_SPLITS, axis=-1)
    a, b = jnp.split(z_glu, 2, axis=-1)
    conv_y, new_conv = conformer_conv(a * jax.nn.sigmoid(b), conv_buf, conv_dw_w, conv_dw_b, conv_ln_g, conv_ln_b, w_conv_out)
    swa_o, new_k, new_v = attn_fn(q.reshape(n, L, SWA_HEADS, SWA_HEAD_DIM),
                                  k.reshape(n, L, SWA_KV_HEADS, SWA_HEAD_DIM),
                                  v.reshape(n, L, SWA_KV_HEADS, SWA_HEAD_DIM))
    mem_o = mem_attend(qm.reshape(n, L, MEM_HEADS, MEM_HEAD_DIM), mem_k, mem_v)
    g_conv, g_swa, g_mem = jnp.split(jax.nn.sigmoid(gate_logits), N_BRANCH, axis=-1)
    merged = g_conv * conv_y + g_swa * (swa_o @ w_swa_out) + g_mem * (mem_o @ w_mem_out)
    x = x + merged @ w_out
    x = x + peer_ffn(rmsnorm(x, norm2_g), peer_w_q, peer_keys, peer_w_down, peer_w_up)
    return x, new_conv, new_k, new_v


def setup_inputs(seed: int = 0) -> dict:
    key = jax.random.key(seed)
    ks = jax.random.split(key, 32)
    f32 = jnp.float32

    def nrm(k, shape, scale):
        return jax.random.normal(k, shape, f32) * scale

    def gain(k, shape):
        return 1.0 + 0.05 * jax.random.normal(k, shape, f32)

    return {
        'x_prompt': nrm(ks[0], (BATCH, SEQ, D_MODEL), 1.0),
        'x_sample': nrm(ks[1], (DEC_BATCH, DEC_SEQ, D_MODEL), 1.0),
        'cache_conv': nrm(ks[2], (DEPTH, DEC_BATCH, CONV_WIDTH - 1, CONV_CH), 0.5),
        'cache_swa_k': nrm(ks[3], (DEPTH, DEC_BATCH, WINDOW, SWA_KV_HEADS, SWA_HEAD_DIM), 1.0),
        'cache_swa_v': nrm(ks[4], (DEPTH, DEC_BATCH, WINDOW, SWA_KV_HEADS, SWA_HEAD_DIM), 1.0),
        'cache_mem_k': nrm(ks[5], (DEPTH, DEC_BATCH, MEM_LEN, MEM_HEADS, MEM_HEAD_DIM), 1.0),
        'cache_mem_v': nrm(ks[6], (DEPTH, DEC_BATCH, MEM_LEN, MEM_HEADS, MEM_HEAD_DIM), 1.0),
        'mem_prompt': nrm(ks[7], (BATCH, MEM_LEN, D_MODEL), 1.0),
        'rel_bias_table': nrm(ks[8], (REL_BUCKETS, SWA_HEADS), 0.5),
        'norm1_g': gain(ks[9], (DEPTH, D_MODEL)),
        'w_in': nrm(ks[10], (DEPTH, D_MODEL, IN_COLS), D_MODEL ** -0.5),
        'conv_dw_w': nrm(ks[11], (DEPTH, CONV_WIDTH, CONV_CH), CONV_WIDTH ** -0.5),
        'conv_dw_b': nrm(ks[12], (DEPTH, CONV_CH), 0.02),
        'conv_ln_g': gain(ks[13], (DEPTH, CONV_CH)),
        'conv_ln_b': nrm(ks[14], (DEPTH, CONV_CH), 0.02),
        'w_conv_out': nrm(ks[15], (DEPTH, CONV_CH, D_MODEL), CONV_CH ** -0.5),
        'swa_sinks': nrm(ks[16], (DEPTH, SWA_HEADS), 1.0),
        'w_swa_out': nrm(ks[17], (DEPTH, W_Q, D_MODEL), W_Q ** -0.5),
        'mem_norm_g': gain(ks[18], (DEPTH, D_MODEL)),
        'w_mem_kv': nrm(ks[19], (DEPTH, D_MODEL, 2 * W_QM), D_MODEL ** -0.5),
        'w_mem_out': nrm(ks[20], (DEPTH, W_QM, D_MODEL), W_QM ** -0.5),
        'w_out': nrm(ks[21], (DEPTH, D_MODEL, D_MODEL), D_MODEL ** -0.5),
        'norm2_g': gain(ks[22], (DEPTH, D_MODEL)),
        'peer_w_q': nrm(ks[23], (DEPTH, D_MODEL, PEER_HEADS * PEER_DK), D_MODEL ** -0.5),
        'peer_keys': nrm(ks[24], (DEPTH, PEER_HEADS, 2, PEER_N_KEYS, PEER_DK_HALF), PEER_DK_HALF ** -0.5),
        'peer_w_down': nrm(ks[25], (DEPTH, PEER_N_EXPERTS, D_MODEL), D_MODEL ** -0.5),
        'peer_w_up': nrm(ks[26], (DEPTH, PEER_N_EXPERTS, D_MODEL), PEER_HEADS ** -0.5),
        'final_norm_g': gain(ks[27], (D_MODEL,)),
    }


def reference(x_prompt, x_sample, cache_conv, cache_swa_k, cache_swa_v, cache_mem_k, cache_mem_v, mem_prompt,
              rel_bias_table, norm1_g, w_in, conv_dw_w, conv_dw_b, conv_ln_g, conv_ln_b, w_conv_out, swa_sinks,
              w_swa_out, mem_norm_g, w_mem_kv, w_mem_out, w_out, norm2_g, peer_w_q, peer_keys, peer_w_down,
              peer_w_up, final_norm_g):
    yp, ys = x_prompt, x_sample
    conv_p, kp, vp, mkp, mvp, conv_s, kss, vss = [], [], [], [], [], [], [], []
    for l in range(DEPTH):
        shared = (norm1_g[l], w_in[l], conv_dw_w[l], conv_dw_b[l], conv_ln_g[l], conv_ln_b[l], w_conv_out[l],
                  w_swa_out[l], w_mem_out[l], w_out[l], norm2_g[l], peer_w_q[l], peer_keys[l], peer_w_down[l], peer_w_up[l])
        mem_n = rmsnorm(mem_prompt, mem_norm_g[l])
        mkv = (mem_n @ w_mem_kv[l]).reshape(mem_prompt.shape[0], MEM_LEN, 2, MEM_HEADS, MEM_HEAD_DIM)
        mem_k, mem_v = mkv[:, :, 0], mkv[:, :, 1]
        conv0 = jnp.zeros((yp.shape[0], CONV_WIDTH - 1, CONV_CH), yp.dtype)
        attn_p = functools.partial(swa_prompt, sinks=swa_sinks[l], table=rel_bias_table)
        yp, c_p, k_p, v_p = decoder_layer(yp, conv0, attn_p, mem_k, mem_v, *shared)
        attn_s = functools.partial(swa_sample, cache_k=cache_swa_k[l], cache_v=cache_swa_v[l],
                                   sinks=swa_sinks[l], table=rel_bias_table)
        ys, c_s, k_s, v_s = decoder_layer(ys, cache_conv[l], attn_s, cache_mem_k[l], cache_mem_v[l], *shared)
        conv_p.append(c_p); kp.append(k_p); vp.append(v_p); mkp.append(mem_k); mvp.append(mem_v)
        conv_s.append(c_s); kss.append(k_s); vss.append(v_s)
    y_prompt = rmsnorm(yp, final_norm_g)
    y_sample = rmsnorm(ys, final_norm_g)
    return (y_prompt, y_sample, jnp.stack(conv_p), jnp.stack(kp), jnp.stack(vp), jnp.stack(mkp), jnp.stack(mvp),
            jnp.stack(conv_s), jnp.stack(kss), jnp.stack(vss))
```

```python
import functools

import jax
import jax.numpy as jnp
from jax import lax
from jax.experimental import pallas as pl
from jax.experimental.pallas import tpu as pltpu

F32 = jnp.float32
BF16 = jnp.bfloat16

D_MODEL = 1024
PAST_LEN = 16384
MEM_LEN = 256
CONV_CH = 512
CONV_WIDTH = 31
SWA_HEADS = 8
SWA_KV_HEADS = 2
SWA_HEAD_DIM = 64
WINDOW = 128
SWA_SCALE = SWA_HEAD_DIM ** -0.5
MEM_HEADS = 4
MEM_HEAD_DIM = 128
MEM_SCALE = MEM_HEAD_DIM ** -0.5
REL_BUCKETS = 32
REL_MAX_DIST = 128
PEER_HEADS = 8
PEER_N_KEYS = 128
PEER_DK_HALF = 64
PEER_TOPK = 16
EPS = 1e-6
NEG_INF = -1e30

W_GLU = 2 * CONV_CH
W_Q = SWA_HEADS * SWA_HEAD_DIM
W_KV = SWA_KV_HEADS * SWA_HEAD_DIM
W_QM = MEM_HEADS * MEM_HEAD_DIM
W_PROJ = W_GLU + W_Q + 2 * W_KV + W_QM

VMEM_LIMIT_BYTES = 56 * 1024 * 1024
LANES = 128
CONV_HALO = 32


def _params(*sem):
    return pltpu.CompilerParams(dimension_semantics=sem, vmem_limit_bytes=VMEM_LIMIT_BYTES)


def _rms(x, g):
    return x * lax.rsqrt(jnp.mean(x * x, axis=-1, keepdims=True) + EPS) * g


def _dot(a, b):
    return jnp.dot(a, b, preferred_element_type=F32)


def _dot_nt(a, b):
    return lax.dot_general(a, b, (((1,), (1,)), ((), ())), preferred_element_type=F32)


def _const_spec(shape):
    zeros = (0,) * len(shape)
    return pl.BlockSpec(shape, lambda *_: zeros)


def _norm_matmul_kernel(x_ref, g_ref, w_ref, o_ref):
    o_ref[...] = _dot(_rms(x_ref[...], g_ref[...]).astype(BF16), w_ref[...])


def _norm_matmul(x, g, w, tm):
    t, n = x.shape[0], w.shape[1]
    return pl.pallas_call(
        _norm_matmul_kernel,
        grid=(t // tm,),
        in_specs=[pl.BlockSpec((tm, D_MODEL), lambda i: (i, 0)), _const_spec((1, D_MODEL)),
                  _const_spec((D_MODEL, n))],
        out_specs=pl.BlockSpec((tm, n), lambda i: (i, 0)),
        out_shape=jax.ShapeDtypeStruct((t, n), F32),
        compiler_params=_params("parallel"),
        name="memkv",
    )(x, g, w)


def _in_proj_kernel(x_ref, g_ref, w_ref, u_ref, q_ref, k_ref, v_ref, qm_ref):
    z = _dot(_rms(x_ref[...], g_ref[...]).astype(BF16), w_ref[...])
    a, b = z[:, :CONV_CH], z[:, CONV_CH:W_GLU]
    u_ref[...] = a * jax.nn.sigmoid(b)
    c = W_GLU
    q_ref[...] = (z[:, c:c + W_Q] * SWA_SCALE).astype(BF16)
    c += W_Q
    k_ref[...] = z[:, c:c + W_KV]
    c += W_KV
    v_ref[...] = z[:, c:c + W_KV]
    c += W_KV
    qm_ref[...] = z[:, c:c + W_QM].astype(BF16)


def _in_proj(x, g, w, tm):
    t = x.shape[0]
    row = lambda n: pl.BlockSpec((tm, n), lambda i: (i, 0))
    return pl.pallas_call(
        _in_proj_kernel,
        grid=(t // tm,),
        in_specs=[row(D_MODEL), _const_spec((1, D_MODEL)), _const_spec((D_MODEL, W_PROJ))],
        out_specs=[row(CONV_CH), row(W_Q), row(W_KV), row(W_KV), row(W_QM)],
        out_shape=[jax.ShapeDtypeStruct((t, CONV_CH), F32), jax.ShapeDtypeStruct((t, W_Q), BF16),
                   jax.ShapeDtypeStruct((t, W_KV), F32), jax.ShapeDtypeStruct((t, W_KV), F32),
                   jax.ShapeDtypeStruct((t, W_QM), BF16)],
        compiler_params=_params("parallel"),
        name="in_proj",
    )(x, g, w)


def _ln_silu(y, g, b):
    mu = jnp.mean(y, axis=-1, keepdims=True)
    var = jnp.mean(jnp.square(y - mu), axis=-1, keepdims=True)
    y = (y - mu) * lax.rsqrt(var + EPS) * g + b
    return y * jax.nn.sigmoid(y)


def _softmax_rows(s):
    e = jnp.exp(s - jnp.max(s, axis=-1, keepdims=True))
    return e / jnp.sum(e, axis=-1, keepdims=True)


def _branches_prompt_kernel(sinks_ref, u_ref, uh_ref, q_ref, k_ref, kp_ref, v_ref, vp_ref, qm_ref,
                            mk_ref, mv_ref, bias_ref, dww_ref, dwb_ref, lng_ref, lnb_ref,
                            conv_ref, swa_ref, mem_ref, ubuf, kbuf, vbuf, *, tq):
    i = pl.program_id(1)
    first = i == 0

    ubuf[0:CONV_HALO, :] = jnp.where(first, 0.0, uh_ref[...])
    ubuf[CONV_HALO:CONV_HALO + tq, :] = u_ref[...]
    rb = 64
    off = CONV_HALO - (CONV_WIDTH - 1)
    for r in range(tq // rb):
        acc = jnp.broadcast_to(dwb_ref[...], (rb, CONV_CH))
        for j in range(CONV_WIDTH):
            acc = acc + ubuf[r * rb + off + j:r * rb + off + j + rb, :] * dww_ref[j:j + 1, :]
        conv_ref[r * rb:(r + 1) * rb, :] = _ln_silu(acc, lng_ref[...], lnb_ref[...]).astype(BF16)

    lane = lax.broadcasted_iota(jnp.int32, (WINDOW + tq, LANES), 1)
    lo = lane < SWA_HEAD_DIM
    for src_ref, prev_ref, buf in ((k_ref, kp_ref, kbuf), (v_ref, vp_ref, vbuf)):
        full = jnp.concatenate([jnp.where(first, 0.0, prev_ref[...]), src_ref[...]], axis=0)
        rolled = pltpu.roll(full, SWA_HEAD_DIM, 1)
        buf[0] = jnp.where(lo, full, 0.0).astype(BF16)
        buf[1] = jnp.where(lo, 0.0, rolled).astype(BF16)
        buf[2] = jnp.where(lo, rolled, 0.0).astype(BF16)
        buf[3] = jnp.where(lo, 0.0, full).astype(BF16)

    qi = lax.broadcasted_iota(jnp.int32, (WINDOW, 2 * WINDOW), 0)
    ki = lax.broadcasted_iota(jnp.int32, (WINDOW, 2 * WINDOW), 1)
    dist = WINDOW + qi - ki
    band = (dist >= 0) & (dist <= WINDOW)
    nqb = tq // WINDOW
    for jb in range(nqb):
        r0 = jb * WINDOW
        kmin = jnp.where(i * nqb + jb > 0, 0, WINDOW)
        mask = band & (ki >= kmin)
        for p in range(SWA_HEADS // 2):
            g = p // 2
            qp = q_ref[r0:r0 + WINDOW, LANES * p:LANES * (p + 1)]
            o = None
            for half in range(2):
                h = 2 * p + half
                s = _dot_nt(qp, kbuf[2 * g + half, r0:r0 + 2 * WINDOW, :]) + bias_ref[h]
                s = jnp.where(mask, s, NEG_INF)
                sink = sinks_ref[h]
                m = jnp.maximum(jnp.max(s, axis=-1, keepdims=True), sink)
                pr = jnp.exp(s - m)
                pr = pr / (jnp.sum(pr, axis=-1, keepdims=True) + jnp.exp(sink - m))
                t = _dot(pr.astype(BF16), vbuf[2 * g + half, r0:r0 + 2 * WINDOW, :])
                o = t if o is None else o + t
            swa_ref[r0:r0 + WINDOW, LANES * p:LANES * (p + 1)] = o.astype(BF16)

    for hm in range(MEM_HEADS):
        sl = slice(hm * MEM_HEAD_DIM, (hm + 1) * MEM_HEAD_DIM)
        w = _softmax_rows(_dot_nt(qm_ref[:, sl], mk_ref[:, sl].astype(BF16)) * MEM_SCALE)
        mem_ref[:, sl] = _dot(w.astype(BF16), mv_ref[:, sl].astype(BF16)).astype(BF16)


def _branches_prompt(sinks, u, q, k, v, qm, mkv, bias, dww, dwb, lng, lnb, batch, seq, tq):
    t = batch * seq
    nq = seq // tq
    row = lambda n: pl.BlockSpec((tq, n), lambda b, i: (b * nq + i, 0))
    halo = lambda rows, n: pl.BlockSpec(
        (rows, n), lambda b, i: (jnp.maximum(b * (seq // rows) + i * (tq // rows) - 1, 0), 0))
    return pl.pallas_call(
        functools.partial(_branches_prompt_kernel, tq=tq),
        grid=(batch, nq),
        in_specs=[pl.BlockSpec(memory_space=pltpu.SMEM),
                  row(CONV_CH), halo(CONV_HALO, CONV_CH), row(W_Q),
                  row(W_KV), halo(WINDOW, W_KV), row(W_KV), halo(WINDOW, W_KV), row(W_QM),
                  pl.BlockSpec((MEM_LEN, W_QM), lambda b, i: (b, 0)),
                  pl.BlockSpec((MEM_LEN, W_QM), lambda b, i: (b, 1)),
                  _const_spec((SWA_HEADS, WINDOW, 2 * WINDOW)),
                  _const_spec((CONV_WIDTH, CONV_CH)), _const_spec((1, CONV_CH)),
                  _const_spec((1, CONV_CH)), _const_spec((1, CONV_CH))],
        out_specs=[row(CONV_CH), row(W_Q), row(W_QM)],
        out_shape=[jax.ShapeDtypeStruct((t, CONV_CH), BF16), jax.ShapeDtypeStruct((t, W_Q), BF16),
                   jax.ShapeDtypeStruct((t, W_QM), BF16)],
        scratch_shapes=[pltpu.VMEM((CONV_HALO + tq, CONV_CH), F32),
                        pltpu.VMEM((4, WINDOW + tq, LANES), BF16),
                        pltpu.VMEM((4, WINDOW + tq, LANES), BF16)],
        compiler_params=_params("parallel", "arbitrary"),
        name="branches_prompt",
    )(sinks, u, u, q, k, k, v, v, qm, mkv, mkv, bias, dww, dwb, lng, lnb)


def _branches_sample_kernel(sinks_ref, u_ref, cc_ref, qx_ref, kn_ref, vn_ref, ck_ref, cv_ref, qm_ref,
                            cmk_ref, cmv_ref, bias_ref, bias0_ref, dww_ref, dwb_ref, lng_ref, lnb_ref,
                            conv_ref, swa_ref, mem_ref, *, nb):
    hist = CONV_WIDTH - 1
    y = jnp.sum(cc_ref[...] * dww_ref[0:hist, :][None], axis=1)
    y = y + u_ref[...] * dww_ref[hist:hist + 1, :] + dwb_ref[...]
    conv_ref[...] = _ln_silu(y, lng_ref[...], lnb_ref[...]).astype(BF16)

    row = lax.broadcasted_iota(jnp.int32, (SWA_HEADS, W_QM), 0)
    col = lax.broadcasted_iota(jnp.int32, (SWA_HEADS, W_QM), 1)
    own = (col // MEM_HEAD_DIM) == row
    sink = sinks_ref[...]
    for n in range(nb):
        qx = qx_ref[n]
        s = _dot_nt(qx, ck_ref[n].astype(BF16)) + bias_ref[...]
        kn = kn_ref[n:n + 1, :].astype(BF16).astype(F32)
        s_new = jnp.sum(qx.astype(F32) * kn, axis=-1, keepdims=True) + bias0_ref[...]
        m = jnp.maximum(jnp.maximum(jnp.max(s, axis=-1, keepdims=True), s_new), sink)
        pr, pr_new = jnp.exp(s - m), jnp.exp(s_new - m)
        den = jnp.sum(pr, axis=-1, keepdims=True) + pr_new + jnp.exp(sink - m)
        vn = vn_ref[n:n + 1, :].astype(BF16).astype(F32)
        o = _dot((pr / den).astype(BF16), cv_ref[n].astype(BF16))
        swa_ref[n] = o + (pr_new / den).astype(BF16).astype(F32) * vn

        qmx = jnp.where(own, qm_ref[n:n + 1, :].astype(F32), 0.0).astype(BF16)
        w = _softmax_rows(_dot_nt(qmx, cmk_ref[n].astype(BF16)) * MEM_SCALE)
        om = _dot(w.astype(BF16), cmv_ref[n].astype(BF16))
        mem_ref[n:n + 1, :] = jnp.sum(jnp.where(own, om, 0.0), axis=0, keepdims=True).astype(BF16)


def _branches_sample(sinks, u, cache_conv, qx, kn, vn, ck, cv, qm, cmk, cmv, bias, bias0,
                     dww, dwb, lng, lnb, nb):
    n = u.shape[0]
    row = lambda c: pl.BlockSpec((nb, c), lambda i: (i, 0))
    blk3 = lambda a, c: pl.BlockSpec((nb, a, c), lambda i: (i, 0, 0))
    return pl.pallas_call(
        functools.partial(_branches_sample_kernel, nb=nb),
        grid=(n // nb,),
        in_specs=[_const_spec((SWA_HEADS, 1)),
                  row(CONV_CH), blk3(CONV_WIDTH - 1, CONV_CH), blk3(SWA_HEADS, LANES),
                  row(W_KV), row(W_KV), blk3(WINDOW, W_KV), blk3(WINDOW, W_KV), row(W_QM),
                  blk3(MEM_LEN, W_QM), blk3(MEM_LEN, W_QM),
                  _const_spec((SWA_HEADS, WINDOW)), _const_spec((SWA_HEADS, 1)),
                  _const_spec((CONV_WIDTH, CONV_CH)), _const_spec((1, CONV_CH)),
                  _const_spec((1, CONV_CH)), _const_spec((1, CONV_CH))],
        out_specs=[row(CONV_CH), blk3(SWA_HEADS, LANES), row(W_QM)],
        out_shape=[jax.ShapeDtypeStruct((n, CONV_CH), BF16),
                   jax.ShapeDtypeStruct((n, SWA_HEADS, LANES), F32),
                   jax.ShapeDtypeStruct((n, W_QM), BF16)],
        compiler_params=_params("parallel"),
        name="branches_sample",
    )(sinks, u, cache_conv, qx, kn, vn, ck, cv, qm, cmk, cmv, bias, bias0, dww, dwb, lng, lnb)


def _merge_kernel(x_ref, g_ref, conv_ref, swa_ref, mem_ref, wg_ref, wco_ref, wso_ref, wmo_ref, wo_ref,
                  o_ref):
    x = x_ref[...]
    h = _rms(x, g_ref[...]).astype(BF16)
    merged = None
    for br, (a_ref, w_ref) in enumerate(((conv_ref, wco_ref), (swa_ref, wso_ref), (mem_ref, wmo_ref))):
        gate = jax.nn.sigmoid(_dot(h, wg_ref[:, br * D_MODEL:(br + 1) * D_MODEL]))
        term = gate * _dot(a_ref[...], w_ref[...])
        merged = term if merged is None else merged + term
    o_ref[...] = x + _dot(merged.astype(BF16), wo_ref[...])


def _merge(x, g, conv, swa, mem, wg, wco, wso, wmo, wo, tm):
    t = x.shape[0]
    row = lambda n: pl.BlockSpec((tm, n), lambda i: (i, 0))
    return pl.pallas_call(
        _merge_kernel,
        grid=(t // tm,),
        in_specs=[row(D_MODEL), _const_spec((1, D_MODEL)), row(CONV_CH), row(W_Q), row(W_QM),
                  _const_spec((D_MODEL, 3 * D_MODEL)), _const_spec((CONV_CH, D_MODEL)),
                  _const_spec((W_Q, D_MODEL)), _const_spec((W_QM, D_MODEL)),
                  _const_spec((D_MODEL, D_MODEL))],
        out_specs=row(D_MODEL),
        out_shape=jax.ShapeDtypeStruct((t, D_MODEL), F32),
        compiler_params=_params("parallel"),
        name="merge",
    )(x, g, conv, swa, mem, wg, wco, wso, wmo, wo)


def _gelu(x):
    return 0.5 * x * (1.0 + lax.erf(x * (2.0 ** -0.5)))


def _extract_desc(arrs, count):
    out = []
    for _ in range(count):
        m = functools.reduce(jnp.maximum, arrs)
        m = jnp.max(m, axis=0, keepdims=True)
        out.append(m)
        arrs = [jnp.where(a == m, -jnp.inf, a) for a in arrs]
    return out


def _peer_kernel(x_ref, g2_ref, gf_ref, wqt_ref, keys_ref, wd_ref, wut_ref, o_ref,
                 h_s, s0_s, e0_s, s1_s, e1_s, thr_s, sv0_s, sv1_s, at_s, ct_s, acc_s, *, tb, cb):
    c = pl.program_id(1)
    nk = PEER_N_KEYS

    @pl.when(c == 0)
    def _prologue():
        hb = _rms(x_ref[...], g2_ref[...]).astype(BF16)
        h_s[...] = hb
        qt = _dot_nt(wqt_ref[...], hb).astype(BF16)
        for h in range(PEER_HEADS):
            for half, (s_s, e_s, sv_s) in enumerate(((s0_s, e0_s, sv0_s), (s1_s, e1_s, sv1_s))):
                r = (2 * h + half) * PEER_DK_HALF
                s = _dot(keys_ref[h, half], qt[r:r + PEER_DK_HALF, :])
                top = _extract_desc([s[8 * j:8 * j + 8, :] for j in range(nk // 8)], PEER_TOPK)
                for j, row in enumerate(top):
                    sv_s[j:j + 1, :] = row
                s_s[h] = s
                e_s[h] = jnp.exp(s - top[0])
            a0, a1 = sv0_s[0:8, :], sv0_s[8:16, :]
            b0, b1 = sv1_s[0:8, :], sv1_s[8:16, :]
            cands = [a0[0:1] + b0, a0[0:1] + b1] + [a0[a:a + 1] + b0 for a in range(1, 8)] + [a1 + b0[0:1]]
            best = _extract_desc(cands, PEER_TOPK)
            thr = best[PEER_TOPK - 1]
            z = functools.reduce(
                jnp.add, [jnp.where(cd >= thr, jnp.exp(cd - best[0]), 0.0) for cd in cands])
            z = jnp.sum(z, axis=0, keepdims=True)
            thr_s[h:h + 1, :] = thr
            e0_s[h] = e0_s[h] / z

    at_s[...] = _dot_nt(wd_ref[...], h_s[...])

    def row_block(j, carry):
        r = c * cb + j
        g = jnp.zeros((nk, tb), F32)
        for h in range(PEER_HEADS):
            s = s0_s[h, pl.ds(r, 1), :] + s1_s[h]
            g = g + jnp.where(s >= thr_s[h:h + 1, :], e1_s[h], 0.0) * e0_s[h, pl.ds(r, 1), :]
        rows = pl.ds(pl.multiple_of(j * nk, nk), nk)
        ct_s[rows, :] = (_gelu(at_s[rows, :]) * g).astype(BF16)
        return carry

    lax.fori_loop(0, cb, row_block, 0)
    contrib = _dot(wut_ref[...], ct_s[...])

    @pl.when(c == 0)
    def _first():
        acc_s[...] = contrib

    @pl.when(c > 0)
    def _rest():
        acc_s[...] += contrib

    @pl.when(c == pl.num_programs(1) - 1)
    def _epilogue():
        y = x_ref[...] + acc_s[...].T
        o_ref[...] = _rms(y, gf_ref[...])


def _peer(x, g2, gf, wqt, keys, wd, wut, tb, cb):
    t = x.shape[0]
    ne = wd.shape[0]
    ec = cb * PEER_N_KEYS
    stat = pltpu.VMEM((PEER_HEADS, PEER_N_KEYS, tb), F32)
    return pl.pallas_call(
        functools.partial(_peer_kernel, tb=tb, cb=cb),
        grid=(t // tb, ne // ec),
        in_specs=[pl.BlockSpec((tb, D_MODEL), lambda i, c: (i, 0)),
                  _const_spec((1, D_MODEL)), _const_spec((1, D_MODEL)),
                  _const_spec((D_MODEL, D_MODEL)),
                  _const_spec((PEER_HEADS, 2, PEER_N_KEYS, PEER_DK_HALF)),
                  pl.BlockSpec((ec, D_MODEL), lambda i, c: (c, 0)),
                  pl.BlockSpec((D_MODEL, ec), lambda i, c: (0, c))],
        out_specs=pl.BlockSpec((tb, D_MODEL), lambda i, c: (i, 0)),
        out_shape=jax.ShapeDtypeStruct((t, D_MODEL), F32),
        scratch_shapes=[pltpu.VMEM((tb, D_MODEL), BF16), stat, stat, stat, stat,
                        pltpu.VMEM((PEER_HEADS, tb), F32),
                        pltpu.VMEM((PEER_TOPK, tb), F32), pltpu.VMEM((PEER_TOPK, tb), F32),
                        pltpu.VMEM((ec, tb), F32), pltpu.VMEM((ec, tb), BF16),
                        pltpu.VMEM((D_MODEL, tb), F32)],
        compiler_params=_params("parallel", "arbitrary"),
        name="peer",
    )(x, g2, gf, wqt, keys, wd, wut)


def _rel_bucket(dist):
    n = jnp.maximum(dist, 0)
    max_exact = REL_BUCKETS // 2
    nf = jnp.maximum(n, 1).astype(F32)
    large = max_exact + (jnp.log(nf / max_exact) / jnp.log(REL_MAX_DIST / max_exact)
                         * (REL_BUCKETS - max_exact)).astype(jnp.int32)
    return jnp.where(n < max_exact, n, jnp.minimum(large, REL_BUCKETS - 1))


def _tile(t, cap):
    tm = min(t, cap)
    assert t % tm == 0, (t, tm)
    return tm


def kernel(x_prompt, x_sample, cache_conv, cache_swa_k, cache_swa_v, cache_mem_k, cache_mem_v, mem_prompt, rel_bias_table, norm1_g, w_in, conv_dw_w, conv_dw_b, conv_ln_g, conv_ln_b, w_conv_out, swa_sinks, w_swa_out, mem_norm_g, w_mem_kv, w_mem_out, w_out, norm2_g, peer_w_q, peer_keys, peer_w_down, peer_w_up, final_norm_g):
    assert w_in.shape[0] == 1, "single layer"
    batch, seq, _ = x_prompt.shape
    nsamp = x_sample.shape[0]
    assert x_sample.shape[1] == 1 and seq % WINDOW == 0
    row = lambda a: a.reshape(1, -1)

    w_proj = w_in[0, :, :W_PROJ].astype(BF16)
    w_gate = w_in[0, :, W_PROJ:].astype(BF16)
    g1, g2, gf = row(norm1_g[0]), row(norm2_g[0]), row(final_norm_g)
    dww, dwb = conv_dw_w[0], row(conv_dw_b[0])
    lng, lnb = row(conv_ln_g[0]), row(conv_ln_b[0])
    wco, wso = w_conv_out[0].astype(BF16), w_swa_out[0].astype(BF16)
    wmo, wo = w_mem_out[0].astype(BF16), w_out[0].astype(BF16)
    wqt = peer_w_q[0].T.astype(BF16)
    keys = peer_keys[0].astype(BF16)
    wd = peer_w_down[0].astype(BF16)
    wut = peer_w_up[0].T.astype(BF16)
    sinks = swa_sinks[0]

    qi = jnp.arange(WINDOW)[:, None]
    ki = jnp.arange(2 * WINDOW)[None, :]
    bias_p = jnp.moveaxis(rel_bias_table[_rel_bucket(WINDOW + qi - ki)], -1, 0).astype(F32)
    bias_s = rel_bias_table[_rel_bucket(WINDOW - jnp.arange(WINDOW))].T.astype(F32)
    bias_0 = rel_bias_table[_rel_bucket(jnp.zeros((1,), jnp.int32))].T.astype(F32)

    xp = x_prompt.reshape(batch * seq, D_MODEL)
    mkv = _norm_matmul(mem_prompt.reshape(batch * MEM_LEN, D_MODEL), row(mem_norm_g[0]),
                       w_mem_kv[0].astype(BF16), MEM_LEN)
    u_p, q_p, k_p, v_p, qm_p = _in_proj(xp, g1, w_proj, _tile(batch * seq, 512))
    tq = _tile(seq, 512)
    conv_p, swa_p, mem_p = _branches_prompt(sinks, u_p, q_p, k_p, v_p, qm_p, mkv, bias_p,
                                            dww, dwb, lng, lnb, batch, seq, tq)
    x2_p = _merge(xp, g1, conv_p, swa_p, mem_p, w_gate, wco, wso, wmo, wo, _tile(batch * seq, 512))
    y_p = _peer(x2_p, g2, gf, wqt, keys, wd, wut, _tile(batch * seq, 512), 8)

    xs = x_sample.reshape(nsamp, D_MODEL)
    u_s, q_s, k_s, v_s, qm_s = _in_proj(xs, g1, w_proj, _tile(nsamp, 128))
    q4 = q_s.reshape(nsamp, SWA_KV_HEADS, SWA_HEADS // SWA_KV_HEADS, SWA_HEAD_DIM)
    zq = jnp.zeros_like(q4[:, 0])
    qx = jnp.concatenate([jnp.concatenate([q4[:, 0], zq], -1), jnp.concatenate([zq, q4[:, 1]], -1)], 1)
    ck = cache_swa_k[0].reshape(nsamp, WINDOW, W_KV)
    cv = cache_swa_v[0].reshape(nsamp, WINDOW, W_KV)
    cmk = cache_mem_k[0].reshape(nsamp, MEM_LEN, W_QM)
    cmv = cache_mem_v[0].reshape(nsamp, MEM_LEN, W_QM)
    conv_s, swa_x, mem_s = _branches_sample(sinks.reshape(SWA_HEADS, 1), u_s, cache_conv[0], qx, k_s, v_s,
                                            ck, cv, qm_s, cmk, cmv, bias_s, bias_0,
                                            dww, dwb, lng, lnb, _tile(nsamp, 8))
    sx = swa_x.reshape(nsamp, SWA_KV_HEADS, SWA_HEADS // SWA_KV_HEADS, SWA_KV_HEADS, SWA_HEAD_DIM)
    swa_s = jnp.stack([sx[:, g, :, g] for g in range(SWA_KV_HEADS)], 1).reshape(nsamp, W_Q).astype(BF16)
    x2_s = _merge(xs, g1, conv_s, swa_s, mem_s, w_gate, wco, wso, wmo, wo, _tile(nsamp, 128))
    y_s = _peer(x2_s, g2, gf, wqt, keys, wd, wut, _tile(nsamp, 128), 8)

    hist = CONV_WIDTH - 1
    kv_shape = (SWA_KV_HEADS, SWA_HEAD_DIM)
    mkv5 = mkv.reshape(batch, MEM_LEN, 2, MEM_HEADS, MEM_HEAD_DIM)
    conv_state_p = u_p.reshape(batch, seq, CONV_CH)[:, -hist:]
    swa_k_p = k_p.reshape(batch, seq, *kv_shape)[:, -WINDOW:]
    swa_v_p = v_p.reshape(batch, seq, *kv_shape)[:, -WINDOW:]
    conv_state_s = jnp.concatenate([cache_conv[0][:, 1:], u_s[:, None, :]], axis=1)
    swa_k_s = jnp.concatenate([cache_swa_k[0][:, 1:], k_s.reshape(nsamp, 1, *kv_shape)], axis=1)
    swa_v_s = jnp.concatenate([cache_swa_v[0][:, 1:], v_s.reshape(nsamp, 1, *kv_shape)], axis=1)
    return (y_p.reshape(batch, seq, D_MODEL), y_s.reshape(nsamp, 1, D_MODEL),
            conv_state_p[None], swa_k_p[None], swa_v_p[None],
            mkv5[:, :, 0][None], mkv5[:, :, 1][None],
            conv_state_s[None], swa_k_s[None], swa_v_s[None])
```

```python
import functools

import jax
import jax.numpy as jnp
from jax import lax
from jax.experimental import pallas as pl
from jax.experimental.pallas import tpu as pltpu

F32 = jnp.float32
BF16 = jnp.bfloat16

D_MODEL = 1024
PAST_LEN = 16384
MEM_LEN = 256
CONV_CH = 512
CONV_WIDTH = 31
SWA_HEADS = 8
SWA_KV_HEADS = 2
SWA_HEAD_DIM = 64
WINDOW = 128
SWA_SCALE = SWA_HEAD_DIM ** -0.5
MEM_HEADS = 4
MEM_HEAD_DIM = 128
MEM_SCALE = MEM_HEAD_DIM ** -0.5
REL_BUCKETS = 32
REL_MAX_DIST = 128
PEER_HEADS = 8
PEER_N_KEYS = 128
PEER_DK_HALF = 64
PEER_TOPK = 16
EPS = 1e-6
NEG_INF = -1e30

W_GLU = 2 * CONV_CH
W_Q = SWA_HEADS * SWA_HEAD_DIM
W_KV = SWA_KV_HEADS * SWA_HEAD_DIM
W_QM = MEM_HEADS * MEM_HEAD_DIM
W_PROJ = W_GLU + W_Q + 2 * W_KV + W_QM

VMEM_LIMIT_BYTES = 56 * 1024 * 1024
LANES = 128
CONV_HALO = 32


def _params(*sem):
    return pltpu.CompilerParams(dimension_semantics=sem, vmem_limit_bytes=VMEM_LIMIT_BYTES)


def _rms(x, g):
    return x * lax.rsqrt(jnp.mean(x * x, axis=-1, keepdims=True) + EPS) * g


def _dot(a, b):
    return jnp.dot(a, b, preferred_element_type=F32)


def _dot_nt(a, b):
    return lax.dot_general(a, b, (((1,), (1,)), ((), ())), preferred_element_type=F32)


def _const_spec(shape):
    zeros = (0,) * len(shape)
    return pl.BlockSpec(shape, lambda *_: zeros)


def _norm_matmul_kernel(x_ref, g_ref, w_ref, o_ref):
    o_ref[...] = _dot(_rms(x_ref[...], g_ref[...]).astype(BF16), w_ref[...])


def _norm_matmul(x, g, w, tm):
    t, n = x.shape[0], w.shape[1]
    return pl.pallas_call(
        _norm_matmul_kernel,
        grid=(t // tm,),
        in_specs=[pl.BlockSpec((tm, D_MODEL), lambda i: (i, 0)), _const_spec((1, D_MODEL)),
                  _const_spec((D_MODEL, n))],
        out_specs=pl.BlockSpec((tm, n), lambda i: (i, 0)),
        out_shape=jax.ShapeDtypeStruct((t, n), F32),
        compiler_params=_params("parallel"),
        name="memkv",
    )(x, g, w)


def _in_proj_kernel(x_ref, g_ref, w_ref, u_ref, q_ref, k_ref, v_ref, qm_ref):
    z = _dot(_rms(x_ref[...], g_ref[...]).astype(BF16), w_ref[...])
    a, b = z[:, :CONV_CH], z[:, CONV_CH:W_GLU]
    u_ref[...] = a * jax.nn.sigmoid(b)
    c = W_GLU
    q_ref[...] = (z[:, c:c + W_Q] * SWA_SCALE).astype(BF16)
    c += W_Q
    k_ref[...] = z[:, c:c + W_KV]
    c += W_KV
    v_ref[...] = z[:, c:c + W_KV]
    c += W_KV
    qm_ref[...] = z[:, c:c + W_QM].astype(BF16)


def _in_proj(x, g, w, tm):
    t = x.shape[0]
    row = lambda n: pl.BlockSpec((tm, n), lambda i: (i, 0))
    return pl.pallas_call(
        _in_proj_kernel,
        grid=(t // tm,),
        in_specs=[row(D_MODEL), _const_spec((1, D_MODEL)), _const_spec((D_MODEL, W_PROJ))],
        out_specs=[row(CONV_CH), row(W_Q), row(W_KV), row(W_KV), row(W_QM)],
        out_shape=[jax.ShapeDtypeStruct((t, CONV_CH), F32), jax.ShapeDtypeStruct((t, W_Q), BF16),
                   jax.ShapeDtypeStruct((t, W_KV), F32), jax.ShapeDtypeStruct((t, W_KV), F32),
                   jax.ShapeDtypeStruct((t, W_QM), BF16)],
        compiler_params=_params("parallel"),
        name="in_proj",
    )(x, g, w)


def _ln_silu(y, g, b):
    mu = jnp.mean(y, axis=-1, keepdims=True)
    var = jnp.mean(jnp.square(y - mu), axis=-1, keepdims=True)
    y = (y - mu) * lax.rsqrt(var + EPS) * g + b
    return y * jax.nn.sigmoid(y)


def _softmax_rows(s):
    e = jnp.exp(s - jnp.max(s, axis=-1, keepdims=True))
    return e / jnp.sum(e, axis=-1, keepdims=True)


def _branches_prompt_kernel(sinks_ref, u_ref, uh_ref, q_ref, k_ref, kp_ref, v_ref, vp_ref, qm_ref,
                            mk_ref, mv_ref, bias_ref, dww_ref, dwb_ref, lng_ref, lnb_ref,
                            conv_ref, swa_ref, mem_ref, ubuf, kbuf, vbuf, *, tq):
    i = pl.program_id(1)
    first = i == 0

    ubuf[0:CONV_HALO, :] = jnp.where(first, 0.0, uh_ref[...])
    ubuf[CONV_HALO:CONV_HALO + tq, :] = u_ref[...]
    rb = 64
    off = CONV_HALO - (CONV_WIDTH - 1)
    for r in range(tq // rb):
        acc = jnp.broadcast_to(dwb_ref[...], (rb, CONV_CH))
        for j in range(CONV_WIDTH):
            acc = acc + ubuf[r * rb + off + j:r * rb + off + j + rb, :] * dww_ref[j:j + 1, :]
        conv_ref[r * rb:(r + 1) * rb, :] = _ln_silu(acc, lng_ref[...], lnb_ref[...]).astype(BF16)

    lane = lax.broadcasted_iota(jnp.int32, (WINDOW + tq, LANES), 1)
    lo = lane < SWA_HEAD_DIM
    for src_ref, prev_ref, buf in ((k_ref, kp_ref, kbuf), (v_ref, vp_ref, vbuf)):
        full = jnp.concatenate([jnp.where(first, 0.0, prev_ref[...]), src_ref[...]], axis=0)
        rolled = pltpu.roll(full, SWA_HEAD_DIM, 1)
        buf[0] = jnp.where(lo, full, 0.0).astype(BF16)
        buf[1] = jnp.where(lo, 0.0, rolled).astype(BF16)
        buf[2] = jnp.where(lo, rolled, 0.0).astype(BF16)
        buf[3] = jnp.where(lo, 0.0, full).astype(BF16)

    qi = lax.broadcasted_iota(jnp.int32, (WINDOW, 2 * WINDOW), 0)
    ki = lax.broadcasted_iota(jnp.int32, (WINDOW, 2 * WINDOW), 1)
    dist = WINDOW + qi - ki
    band = (dist >= 0) & (dist <= WINDOW)
    nqb = tq // WINDOW
    for jb in range(nqb):
        r0 = jb * WINDOW
        kmin = jnp.where(i * nqb + jb > 0, 0, WINDOW)
        mask = band & (ki >= kmin)
        for p in range(SWA_HEADS // 2):
            g = p // 2
            qp = q_ref[r0:r0 + WINDOW, LANES * p:LANES * (p + 1)]
            o = None
            for half in range(2):
                h = 2 * p + half
                s = _dot_nt(qp, kbuf[2 * g + half, r0:r0 + 2 * WINDOW, :]) + bias_ref[h]
                s = jnp.where(mask, s, NEG_INF)
                sink = sinks_ref[h]
                m = jnp.maximum(jnp.max(s, axis=-1, keepdims=True), sink)
                pr = jnp.exp(s - m)
                pr = pr / (jnp.sum(pr, axis=-1, keepdims=True) + jnp.exp(sink - m))
                t = _dot(pr.astype(BF16), vbuf[2 * g + half, r0:r0 + 2 * WINDOW, :])
                o = t if o is None else o + t
            swa_ref[r0:r0 + WINDOW, LANES * p:LANES * (p + 1)] = o.astype(BF16)

    for hm in range(MEM_HEADS):
        sl = slice(hm * MEM_HEAD_DIM, (hm + 1) * MEM_HEAD_DIM)
        w = _softmax_rows(_dot_nt(qm_ref[:, sl], mk_ref[:, sl].astype(BF16)) * MEM_SCALE)
        mem_ref[:, sl] = _dot(w.astype(BF16), mv_ref[:, sl].astype(BF16)).astype(BF16)


def _branches_prompt(sinks, u, q, k, v, qm, mkv, bias, dww, dwb, lng, lnb, batch, seq, tq):
    t = batch * seq
    nq = seq // tq
    row = lambda n: pl.BlockSpec((tq, n), lambda b, i: (b * nq + i, 0))
    halo = lambda rows, n: pl.BlockSpec(
        (rows, n), lambda b, i: (jnp.maximum(b * (seq // rows) + i * (tq // rows) - 1, 0), 0))
    return pl.pallas_call(
        functools.partial(_branches_prompt_kernel, tq=tq),
        grid=(batch, nq),
        in_specs=[pl.BlockSpec(memory_space=pltpu.SMEM),
                  row(CONV_CH), halo(CONV_HALO, CONV_CH), row(W_Q),
                  row(W_KV), halo(WINDOW, W_KV), row(W_KV), halo(WINDOW, W_KV), row(W_QM),
                  pl.BlockSpec((MEM_LEN, W_QM), lambda b, i: (b, 0)),
                  pl.BlockSpec((MEM_LEN, W_QM), lambda b, i: (b, 1)),
                  _const_spec((SWA_HEADS, WINDOW, 2 * WINDOW)),
                  _const_spec((CONV_WIDTH, CONV_CH)), _const_spec((1, CONV_CH)),
                  _const_spec((1, CONV_CH)), _const_spec((1, CONV_CH))],
        out_specs=[row(CONV_CH), row(W_Q), row(W_QM)],
        out_shape=[jax.ShapeDtypeStruct((t, CONV_CH), BF16), jax.ShapeDtypeStruct((t, W_Q), BF16),
                   jax.ShapeDtypeStruct((t, W_QM), BF16)],
        scratch_shapes=[pltpu.VMEM((CONV_HALO + tq, CONV_CH), F32),
                        pltpu.VMEM((4, WINDOW + tq, LANES), BF16),
                        pltpu.VMEM((4, WINDOW + tq, LANES), BF16)],
        compiler_params=_params("parallel", "arbitrary"),
        name="branches_prompt",
    )(sinks, u, u, q, k, k, v, v, qm, mkv, mkv, bias, dww, dwb, lng, lnb)


def _branches_sample_kernel(sinks_ref, u_ref, cc_ref, qx_ref, kn_ref, vn_ref, ck_ref, cv_ref, qm_ref,
                            cmk_ref, cmv_ref, bias_ref, bias0_ref, dww_ref, dwb_ref, lng_ref, lnb_ref,
                            conv_ref, swa_ref, mem_ref, *, nb):
    hist = CONV_WIDTH - 1
    y = jnp.sum(cc_ref[...] * dww_ref[0:hist, :][None], axis=1)
    y = y + u_ref[...] * dww_ref[hist:hist + 1, :] + dwb_ref[...]
    conv_ref[...] = _ln_silu(y, lng_ref[...], lnb_ref[...]).astype(BF16)

    row = lax.broadcasted_iota(jnp.int32, (SWA_HEADS, W_QM), 0)
    col = lax.broadcasted_iota(jnp.int32, (SWA_HEADS, W_QM), 1)
    own = (col // MEM_HEAD_DIM) == row
    sink = sinks_ref[...]
    for n in range(nb):
        qx = qx_ref[n]
        s = _dot_nt(qx, ck_ref[n].astype(BF16)) + bias_ref[...]
        kn = kn_ref[n:n + 1, :].astype(BF16).astype(F32)
        s_new = jnp.sum(qx.astype(F32) * kn, axis=-1, keepdims=True) + bias0_ref[...]
        m = jnp.maximum(jnp.maximum(jnp.max(s, axis=-1, keepdims=True), s_new), sink)
        pr, pr_new = jnp.exp(s - m), jnp.exp(s_new - m)
        den = jnp.sum(pr, axis=-1, keepdims=True) + pr_new + jnp.exp(sink - m)
        vn = vn_ref[n:n + 1, :].astype(BF16).astype(F32)
        o = _dot((pr / den).astype(BF16), cv_ref[n].astype(BF16))
        swa_ref[n] = o + (pr_new / den).astype(BF16).astype(F32) * vn

        qmx = jnp.where(own, qm_ref[n:n + 1, :].astype(F32), 0.0).astype(BF16)
        w = _softmax_rows(_dot_nt(qmx, cmk_ref[n].astype(BF16)) * MEM_SCALE)
        om = _dot(w.astype(BF16), cmv_ref[n].astype(BF16))
        mem_ref[n:n + 1, :] = jnp.sum(jnp.where(own, om, 0.0), axis=0, keepdims=True).astype(BF16)


def _branches_sample(sinks, u, cache_conv, qx, kn, vn, ck, cv, qm, cmk, cmv, bias, bias0,
                     dww, dwb, lng, lnb, nb):
    n = u.shape[0]
    row = lambda c: pl.BlockSpec((nb, c), lambda i: (i, 0))
    blk3 = lambda a, c: pl.BlockSpec((nb, a, c), lambda i: (i, 0, 0))
    return pl.pallas_call(
        functools.partial(_branches_sample_kernel, nb=nb),
        grid=(n // nb,),
        in_specs=[_const_spec((SWA_HEADS, 1)),
                  row(CONV_CH), blk3(CONV_WIDTH - 1, CONV_CH), blk3(SWA_HEADS, LANES),
                  row(W_KV), row(W_KV), blk3(WINDOW, W_KV), blk3(WINDOW, W_KV), row(W_QM),
                  blk3(MEM_LEN, W_QM), blk3(MEM_LEN, W_QM),
                  _const_spec((SWA_HEADS, WINDOW)), _const_spec((SWA_HEADS, 1)),
                  _const_spec((CONV_WIDTH, CONV_CH)), _const_spec((1, CONV_CH)),
                  _const_spec((1, CONV_CH)), _const_spec((1, CONV_CH))],
        out_specs=[row(CONV_CH), blk3(SWA_HEADS, LANES), row(W_QM)],
        out_shape=[jax.ShapeDtypeStruct((n, CONV_CH), BF16),
                   jax.ShapeDtypeStruct((n, SWA_HEADS, LANES), F32),
                   jax.ShapeDtypeStruct((n, W_QM), BF16)],
        compiler_params=_params("parallel"),
        name="branches_sample",
    )(sinks, u, cache_conv, qx, kn, vn, ck, cv, qm, cmk, cmv, bias, bias0, dww, dwb, lng, lnb)


def _merge_kernel(x_ref, g_ref, conv_ref, swa_ref, mem_ref, wg_ref, wco_ref, wso_ref, wmo_ref, wo_ref,
                  o_ref):
    x = x_ref[...]
    h = _rms(x, g_ref[...]).astype(BF16)
    merged = None
    for br, (a_ref, w_ref) in enumerate(((conv_ref, wco_ref), (swa_ref, wso_ref), (mem_ref, wmo_ref))):
        gate = jax.nn.sigmoid(_dot(h, wg_ref[:, br * D_MODEL:(br + 1) * D_MODEL]))
        term = gate * _dot(a_ref[...], w_ref[...])
        merged = term if merged is None else merged + term
    o_ref[...] = x + _dot(merged.astype(BF16), wo_ref[...])


def _merge(x, g, conv, swa, mem, wg, wco, wso, wmo, wo, tm):
    t = x.shape[0]
    row = lambda n: pl.BlockSpec((tm, n), lambda i: (i, 0))
    return pl.pallas_call(
        _merge_kernel,
        grid=(t // tm,),
        in_specs=[row(D_MODEL), _const_spec((1, D_MODEL)), row(CONV_CH), row(W_Q), row(W_QM),
                  _const_spec((D_MODEL, 3 * D_MODEL)), _const_spec((CONV_CH, D_MODEL)),
                  _const_spec((W_Q, D_MODEL)), _const_spec((W_QM, D_MODEL)),
                  _const_spec((D_MODEL, D_MODEL))],
        out_specs=row(D_MODEL),
        out_shape=jax.ShapeDtypeStruct((t, D_MODEL), F32),
        compiler_params=_params("parallel"),
        name="merge",
    )(x, g, conv, swa, mem, wg, wco, wso, wmo, wo)


def _gelu(x):
    return 0.5 * x * (1.0 + lax.erf(x * (2.0 ** -0.5)))


def _top_values(arrs, count, with_rank=False):
    out = []
    ranks = [jnp.full(a.shape, float(count), F32) for a in arrs] if with_rank else None
    for it in range(count):
        m = jnp.max(functools.reduce(jnp.maximum, arrs), axis=0, keepdims=True)
        out.append(m)
        hit = [a == m for a in arrs]
        if with_rank:
            ranks = [jnp.where(hh, float(it), rk) for hh, rk in zip(hit, ranks)]
        arrs = [jnp.where(hh, -jnp.inf, a) for hh, a in zip(hit, arrs)]
    return (out, ranks) if with_rank else out


def _peer_route(h, qt, keys_ref, l_s, e0_s, r1_s, e1_s, sv0_s, sv1_s):
    nk, tb = PEER_N_KEYS, qt.shape[1]
    blocks = lambda a: [a[8 * j:8 * j + 8, :] for j in range(nk // 8)]
    r = 2 * h * PEER_DK_HALF
    s0 = _dot(keys_ref[h, 0], qt[r:r + PEER_DK_HALF, :])
    s1 = _dot(keys_ref[h, 1], qt[r + PEER_DK_HALF:r + 2 * PEER_DK_HALF, :])
    top0 = _top_values(blocks(s0), PEER_TOPK)
    top1, rank1 = _top_values(blocks(s1), PEER_TOPK, with_rank=True)
    for j in range(PEER_TOPK):
        sv0_s[j:j + 1, :] = top0[j]
        sv1_s[j:j + 1, :] = top1[j]
    a0, a1 = sv0_s[0:8, :], sv0_s[8:16, :]
    b0, b1 = sv1_s[0:8, :], sv1_s[8:16, :]
    cands = [a0[0:1] + b0, a0[0:1] + b1] + [a0[a:a + 1] + b0 for a in range(1, 8)] + [a1 + b0[0:1]]
    best = _top_values(cands, PEER_TOPK)
    sel = [cd >= best[PEER_TOPK - 1] for cd in cands]
    z = functools.reduce(jnp.add, [jnp.where(sl, jnp.exp(cd - best[0]), 0.0) for sl, cd in zip(sel, cands)])
    z = jnp.sum(z, axis=0, keepdims=True)
    cnt = [jnp.sum(jnp.where(sl, 1.0, 0.0), axis=0, keepdims=True) for sl in sel[:9]]
    counts = [cnt[0] + cnt[1]] + cnt[2:9]
    tail = jnp.where(sel[9], 1.0, 0.0)
    lrow = jnp.zeros((nk, tb), F32)
    for a in range(PEER_TOPK):
        lrow = jnp.where(s0 == top0[a], counts[a] if a < 8 else tail[a - 8:a - 7], lrow)
    l_s[h] = lrow
    e0_s[h] = jnp.exp(s0 - top0[0]) / z
    r1_s[h] = jnp.concatenate(rank1, axis=0).astype(BF16)
    e1_s[h] = jnp.exp(s1 - top1[0]).astype(BF16)


def _peer_kernel(x_ref, g2_ref, gf_ref, wqt_ref, keys_ref, wd_ref, wut_ref, o_ref,
                 h_s, l_s, e0_s, r1_s, e1_s, sv0_s, sv1_s, at_s, ct_s, acc_s, *, cb):
    s = pl.program_id(1)
    nk = PEER_N_KEYS

    @pl.when(s == 0)
    def _first():
        hb = _rms(x_ref[...], g2_ref[...]).astype(BF16)
        h_s[...] = hb
        qt = _dot_nt(wqt_ref[...], hb).astype(BF16)
        for h in range(PEER_HEADS):
            _peer_route(h, qt, keys_ref, l_s, e0_s, r1_s, e1_s, sv0_s, sv1_s)
        at_s[0] = _dot_nt(wd_ref[...], hb)
        acc_s[...] = jnp.zeros_like(acc_s)

    @pl.when(s > 0)
    def _steady():
        slot = lax.rem(s, 2)
        at_s[slot] = _dot_nt(wd_ref[...], h_s[...])
        for j in range(cb):
            r = (s - 1) * cb + j
            g = None
            for h in range(PEER_HEADS):
                lrow = l_s[h, pl.ds(r, 1), :].astype(BF16)
                erow = e0_s[h, pl.ds(r, 1), :].astype(BF16)
                term = jnp.where(r1_s[h] < lrow, e1_s[h], jnp.zeros((), BF16)) * erow
                g = term if g is None else g + term
            rows = slice(j * nk, (j + 1) * nk)
            ct_s[rows, :] = _gelu(at_s[1 - slot, rows, :]).astype(BF16) * g
        acc_s[...] += _dot(wut_ref[...], ct_s[...])

    @pl.when(s == pl.num_programs(1) - 1)
    def _last():
        y = x_ref[...] + acc_s[...].T
        o_ref[...] = _rms(y, gf_ref[...])


def _peer(x, g2, gf, wqt, keys, wd, wut, tb, cb):
    t = x.shape[0]
    ec = cb * PEER_N_KEYS
    nch = wd.shape[0] // ec
    stat = lambda dt: pltpu.VMEM((PEER_HEADS, PEER_N_KEYS, tb), dt)
    return pl.pallas_call(
        functools.partial(_peer_kernel, cb=cb),
        grid=(t // tb, nch + 1),
        in_specs=[pl.BlockSpec((tb, D_MODEL), lambda i, s: (i, 0)),
                  _const_spec((1, D_MODEL)), _const_spec((1, D_MODEL)),
                  _const_spec((D_MODEL, D_MODEL)),
                  _const_spec((PEER_HEADS, 2, PEER_N_KEYS, PEER_DK_HALF)),
                  pl.BlockSpec((ec, D_MODEL), lambda i, s: (jnp.minimum(s, nch - 1), 0)),
                  pl.BlockSpec((D_MODEL, ec), lambda i, s: (0, jnp.maximum(s - 1, 0)))],
        out_specs=pl.BlockSpec((tb, D_MODEL), lambda i, s: (i, 0)),
        out_shape=jax.ShapeDtypeStruct((t, D_MODEL), F32),
        scratch_shapes=[pltpu.VMEM((tb, D_MODEL), BF16), stat(F32), stat(F32), stat(BF16), stat(BF16),
                        pltpu.VMEM((PEER_TOPK, tb), F32), pltpu.VMEM((PEER_TOPK, tb), F32),
                        pltpu.VMEM((2, ec, tb), F32), pltpu.VMEM((ec, tb), BF16),
                        pltpu.VMEM((D_MODEL, tb), F32)],
        compiler_params=_params("parallel", "arbitrary"),
        name="peer",
    )(x, g2, gf, wqt, keys, wd, wut)


def _rel_bucket(dist):
    n = jnp.maximum(dist, 0)
    max_exact = REL_BUCKETS // 2
    nf = jnp.maximum(n, 1).astype(F32)
    large = max_exact + (jnp.log(nf / max_exact) / jnp.log(REL_MAX_DIST / max_exact)
                         * (REL_BUCKETS - max_exact)).astype(jnp.int32)
    return jnp.where(n < max_exact, n, jnp.minimum(large, REL_BUCKETS - 1))


def _tile(t, cap):
    tm = min(t, cap)
    assert t % tm == 0, (t, tm)
    return tm


def kernel(x_prompt, x_sample, cache_conv, cache_swa_k, cache_swa_v, cache_mem_k, cache_mem_v, mem_prompt, rel_bias_table, norm1_g, w_in, conv_dw_w, conv_dw_b, conv_ln_g, conv_ln_b, w_conv_out, swa_sinks, w_swa_out, mem_norm_g, w_mem_kv, w_mem_out, w_out, norm2_g, peer_w_q, peer_keys, peer_w_down, peer_w_up, final_norm_g):
    assert w_in.shape[0] == 1, "single layer"
    batch, seq, _ = x_prompt.shape
    nsamp = x_sample.shape[0]
    assert x_sample.shape[1] == 1 and seq % WINDOW == 0
    row = lambda a: a.reshape(1, -1)

    w_proj = w_in[0, :, :W_PROJ].astype(BF16)
    w_gate = w_in[0, :, W_PROJ:].astype(BF16)
    g1, g2, gf = row(norm1_g[0]), row(norm2_g[0]), row(final_norm_g)
    dww, dwb = conv_dw_w[0], row(conv_dw_b[0])
    lng, lnb = row(conv_ln_g[0]), row(conv_ln_b[0])
    wco, wso = w_conv_out[0].astype(BF16), w_swa_out[0].astype(BF16)
    wmo, wo = w_mem_out[0].astype(BF16), w_out[0].astype(BF16)
    wqt = peer_w_q[0].T.astype(BF16)
    keys = peer_keys[0].astype(BF16)
    wd = peer_w_down[0].astype(BF16)
    wut = peer_w_up[0].T.astype(BF16)
    sinks = swa_sinks[0]

    qi = jnp.arange(WINDOW)[:, None]
    ki = jnp.arange(2 * WINDOW)[None, :]
    bias_p = jnp.moveaxis(rel_bias_table[_rel_bucket(WINDOW + qi - ki)], -1, 0).astype(F32)
    bias_s = rel_bias_table[_rel_bucket(WINDOW - jnp.arange(WINDOW))].T.astype(F32)
    bias_0 = rel_bias_table[_rel_bucket(jnp.zeros((1,), jnp.int32))].T.astype(F32)

    xp = x_prompt.reshape(batch * seq, D_MODEL)
    mkv = _norm_matmul(mem_prompt.reshape(batch * MEM_LEN, D_MODEL), row(mem_norm_g[0]),
                       w_mem_kv[0].astype(BF16), MEM_LEN)
    u_p, q_p, k_p, v_p, qm_p = _in_proj(xp, g1, w_proj, _tile(batch * seq, 512))
    tq = _tile(seq, 512)
    conv_p, swa_p, mem_p = _branches_prompt(sinks, u_p, q_p, k_p, v_p, qm_p, mkv, bias_p,
                                            dww, dwb, lng, lnb, batch, seq, tq)
    x2_p = _merge(xp, g1, conv_p, swa_p, mem_p, w_gate, wco, wso, wmo, wo, _tile(batch * seq, 512))
    y_p = _peer(x2_p, g2, gf, wqt, keys, wd, wut, _tile(batch * seq, 512), 8)

    xs = x_sample.reshape(nsamp, D_MODEL)
    u_s, q_s, k_s, v_s, qm_s = _in_proj(xs, g1, w_proj, _tile(nsamp, 128))
    q4 = q_s.reshape(nsamp, SWA_KV_HEADS, SWA_HEADS // SWA_KV_HEADS, SWA_HEAD_DIM)
    zq = jnp.zeros_like(q4[:, 0])
    qx = jnp.concatenate([jnp.concatenate([q4[:, 0], zq], -1), jnp.concatenate([zq, q4[:, 1]], -1)], 1)
    ck = cache_swa_k[0].reshape(nsamp, WINDOW, W_KV)
    cv = cache_swa_v[0].reshape(nsamp, WINDOW, W_KV)
    cmk = cache_mem_k[0].reshape(nsamp, MEM_LEN, W_QM)
    cmv = cache_mem_v[0].reshape(nsamp, MEM_LEN, W_QM)
    conv_s, swa_x, mem_s = _branches_sample(sinks.reshape(SWA_HEADS, 1), u_s, cache_conv[0], qx, k_s, v_s,
                                            ck, cv, qm_s, cmk, cmv, bias_s, bias_0,
                                            dww, dwb, lng, lnb, _tile(nsamp, 8))
    sx = swa_x.reshape(nsamp, SWA_KV_HEADS, SWA_HEADS // SWA_KV_HEADS, SWA_KV_HEADS, SWA_HEAD_DIM)
    swa_s = jnp.stack([sx[:, g, :, g] for g in range(SWA_KV_HEADS)], 1).reshape(nsamp, W_Q).astype(BF16)
    x2_s = _merge(xs, g1, conv_s, swa_s, mem_s, w_gate, wco, wso, wmo, wo, _tile(nsamp, 128))
    y_s = _peer(x2_s, g2, gf, wqt, keys, wd, wut, _tile(nsamp, 128), 8)

    hist = CONV_WIDTH - 1
    kv_shape = (SWA_KV_HEADS, SWA_HEAD_DIM)
    mkv5 = mkv.reshape(batch, MEM_LEN, 2, MEM_HEADS, MEM_HEAD_DIM)
    conv_state_p = u_p.reshape(batch, seq, CONV_CH)[:, -hist:]
    swa_k_p = k_p.reshape(batch, seq, *kv_shape)[:, -WINDOW:]
    swa_v_p = v_p.reshape(batch, seq, *kv_shape)[:, -WINDOW:]
    conv_state_s = jnp.concatenate([cache_conv[0][:, 1:], u_s[:, None, :]], axis=1)
    swa_k_s = jnp.concatenate([cache_swa_k[0][:, 1:], k_s.reshape(nsamp, 1, *kv_shape)], axis=1)
    swa_v_s = jnp.concatenate([cache_swa_v[0][:, 1:], v_s.reshape(nsamp, 1, *kv_shape)], axis=1)
    return (y_p.reshape(batch, seq, D_MODEL), y_s.reshape(nsamp, 1, D_MODEL),
            conv_state_p[None], swa_k_p[None], swa_v_p[None],
            mkv5[:, :, 0][None], mkv5[:, :, 1][None],
            conv_state_s[None], swa_k_s[None], swa_v_s[None])
```

```python
import functools

import jax
import jax.numpy as jnp
from jax import lax
from jax.experimental import pallas as pl
from jax.experimental.pallas import tpu as pltpu

F32 = jnp.float32
BF16 = jnp.bfloat16

D_MODEL = 1024
PAST_LEN = 16384
MEM_LEN = 256
CONV_CH = 512
CONV_WIDTH = 31
SWA_HEADS = 8
SWA_KV_HEADS = 2
SWA_HEAD_DIM = 64
WINDOW = 128
SWA_SCALE = SWA_HEAD_DIM ** -0.5
MEM_HEADS = 4
MEM_HEAD_DIM = 128
MEM_SCALE = MEM_HEAD_DIM ** -0.5
REL_BUCKETS = 32
REL_MAX_DIST = 128
PEER_HEADS = 8
PEER_N_KEYS = 128
PEER_DK_HALF = 64
PEER_TOPK = 16
EPS = 1e-6
NEG_INF = -1e30

W_GLU = 2 * CONV_CH
W_Q = SWA_HEADS * SWA_HEAD_DIM
W_KV = SWA_KV_HEADS * SWA_HEAD_DIM
W_QM = MEM_HEADS * MEM_HEAD_DIM
W_PROJ = W_GLU + W_Q + 2 * W_KV + W_QM

VMEM_LIMIT_BYTES = 56 * 1024 * 1024
LANES = 128
CONV_HALO = 32


def _params(*sem, flags=None):
    return pltpu.CompilerParams(dimension_semantics=sem, vmem_limit_bytes=VMEM_LIMIT_BYTES, flags=flags)


def _rms(x, g):
    return x * lax.rsqrt(jnp.mean(x * x, axis=-1, keepdims=True) + EPS) * g


def _dot(a, b):
    return jnp.dot(a, b, preferred_element_type=F32)


def _dot_nt(a, b):
    return lax.dot_general(a, b, (((1,), (1,)), ((), ())), preferred_element_type=F32)


def _const_spec(shape):
    zeros = (0,) * len(shape)
    return pl.BlockSpec(shape, lambda *_: zeros)


def _norm_matmul_kernel(x_ref, g_ref, w_ref, o_ref):
    o_ref[...] = _dot(_rms(x_ref[...], g_ref[...]).astype(BF16), w_ref[...])


def _norm_matmul(x, g, w, tm):
    t, n = x.shape[0], w.shape[1]
    return pl.pallas_call(
        _norm_matmul_kernel,
        grid=(t // tm,),
        in_specs=[pl.BlockSpec((tm, D_MODEL), lambda i: (i, 0)), _const_spec((1, D_MODEL)),
                  _const_spec((D_MODEL, n))],
        out_specs=pl.BlockSpec((tm, n), lambda i: (i, 0)),
        out_shape=jax.ShapeDtypeStruct((t, n), F32),
        compiler_params=_params("parallel"),
        name="memkv",
    )(x, g, w)


def _in_proj_kernel(x_ref, g_ref, w_ref, u_ref, q_ref, k_ref, v_ref, qm_ref):
    z = _dot(_rms(x_ref[...], g_ref[...]).astype(BF16), w_ref[...])
    a, b = z[:, :CONV_CH], z[:, CONV_CH:W_GLU]
    u_ref[...] = a * jax.nn.sigmoid(b)
    c = W_GLU
    q_ref[...] = (z[:, c:c + W_Q] * SWA_SCALE).astype(BF16)
    c += W_Q
    k_ref[...] = z[:, c:c + W_KV]
    c += W_KV
    v_ref[...] = z[:, c:c + W_KV]
    c += W_KV
    qm_ref[...] = z[:, c:c + W_QM].astype(BF16)


def _in_proj(x, g, w, tm):
    t = x.shape[0]
    row = lambda n: pl.BlockSpec((tm, n), lambda i: (i, 0))
    return pl.pallas_call(
        _in_proj_kernel,
        grid=(t // tm,),
        in_specs=[row(D_MODEL), _const_spec((1, D_MODEL)), _const_spec((D_MODEL, W_PROJ))],
        out_specs=[row(CONV_CH), row(W_Q), row(W_KV), row(W_KV), row(W_QM)],
        out_shape=[jax.ShapeDtypeStruct((t, CONV_CH), F32), jax.ShapeDtypeStruct((t, W_Q), BF16),
                   jax.ShapeDtypeStruct((t, W_KV), F32), jax.ShapeDtypeStruct((t, W_KV), F32),
                   jax.ShapeDtypeStruct((t, W_QM), BF16)],
        compiler_params=_params("parallel"),
        name="in_proj",
    )(x, g, w)


def _ln_silu(y, g, b):
    mu = jnp.mean(y, axis=-1, keepdims=True)
    var = jnp.mean(jnp.square(y - mu), axis=-1, keepdims=True)
    y = (y - mu) * lax.rsqrt(var + EPS) * g + b
    return y * jax.nn.sigmoid(y)


def _softmax_rows(s):
    e = jnp.exp(s - jnp.max(s, axis=-1, keepdims=True))
    return e / jnp.sum(e, axis=-1, keepdims=True)


def _branches_prompt_kernel(sinks_ref, u_ref, uh_ref, q_ref, k_ref, kp_ref, v_ref, vp_ref, qm_ref,
                            mk_ref, mv_ref, bias_ref, dww_ref, dwb_ref, lng_ref, lnb_ref,
                            conv_ref, swa_ref, mem_ref, ubuf, kbuf, vbuf, *, tq):
    i = pl.program_id(1)
    first = i == 0

    ubuf[0:CONV_HALO, :] = jnp.where(first, 0.0, uh_ref[...])
    ubuf[CONV_HALO:CONV_HALO + tq, :] = u_ref[...]
    rb = 64
    off = CONV_HALO - (CONV_WIDTH - 1)
    for r in range(tq // rb):
        acc = jnp.broadcast_to(dwb_ref[...], (rb, CONV_CH))
        for j in range(CONV_WIDTH):
            acc = acc + ubuf[r * rb + off + j:r * rb + off + j + rb, :] * dww_ref[j:j + 1, :]
        conv_ref[r * rb:(r + 1) * rb, :] = _ln_silu(acc, lng_ref[...], lnb_ref[...]).astype(BF16)

    lane = lax.broadcasted_iota(jnp.int32, (WINDOW + tq, LANES), 1)
    lo = lane < SWA_HEAD_DIM
    for src_ref, prev_ref, buf in ((k_ref, kp_ref, kbuf), (v_ref, vp_ref, vbuf)):
        full = jnp.concatenate([jnp.where(first, 0.0, prev_ref[...]), src_ref[...]], axis=0)
        rolled = pltpu.roll(full, SWA_HEAD_DIM, 1)
        buf[0] = jnp.where(lo, full, 0.0).astype(BF16)
        buf[1] = jnp.where(lo, 0.0, rolled).astype(BF16)
        buf[2] = jnp.where(lo, rolled, 0.0).astype(BF16)
        buf[3] = jnp.where(lo, 0.0, full).astype(BF16)

    qi = lax.broadcasted_iota(jnp.int32, (WINDOW, 2 * WINDOW), 0)
    ki = lax.broadcasted_iota(jnp.int32, (WINDOW, 2 * WINDOW), 1)
    dist = WINDOW + qi - ki
    band = (dist >= 0) & (dist <= WINDOW)
    nqb = tq // WINDOW
    for jb in range(nqb):
        r0 = jb * WINDOW
        kmin = jnp.where(i * nqb + jb > 0, 0, WINDOW)
        mask = band & (ki >= kmin)
        for p in range(SWA_HEADS // 2):
            g = p // 2
            qp = q_ref[r0:r0 + WINDOW, LANES * p:LANES * (p + 1)]
            o = None
            for half in range(2):
                h = 2 * p + half
                s = _dot_nt(qp, kbuf[2 * g + half, r0:r0 + 2 * WINDOW, :]) + bias_ref[h]
                s = jnp.where(mask, s, NEG_INF)
                sink = sinks_ref[h]
                m = jnp.maximum(jnp.max(s, axis=-1, keepdims=True), sink)
                pr = jnp.exp(s - m)
                pr = pr / (jnp.sum(pr, axis=-1, keepdims=True) + jnp.exp(sink - m))
                t = _dot(pr.astype(BF16), vbuf[2 * g + half, r0:r0 + 2 * WINDOW, :])
                o = t if o is None else o + t
            swa_ref[r0:r0 + WINDOW, LANES * p:LANES * (p + 1)] = o.astype(BF16)

    for hm in range(MEM_HEADS):
        sl = slice(hm * MEM_HEAD_DIM, (hm + 1) * MEM_HEAD_DIM)
        w = _softmax_rows(_dot_nt(qm_ref[:, sl], mk_ref[:, sl].astype(BF16)) * MEM_SCALE)
        mem_ref[:, sl] = _dot(w.astype(BF16), mv_ref[:, sl].astype(BF16)).astype(BF16)


def _branches_prompt(sinks, u, q, k, v, qm, mkv, bias, dww, dwb, lng, lnb, batch, seq, tq):
    t = batch * seq
    nq = seq // tq
    row = lambda n: pl.BlockSpec((tq, n), lambda b, i: (b * nq + i, 0))
    halo = lambda rows, n: pl.BlockSpec(
        (rows, n), lambda b, i: (jnp.maximum(b * (seq // rows) + i * (tq // rows) - 1, 0), 0))
    return pl.pallas_call(
        functools.partial(_branches_prompt_kernel, tq=tq),
        grid=(batch, nq),
        in_specs=[pl.BlockSpec(memory_space=pltpu.SMEM),
                  row(CONV_CH), halo(CONV_HALO, CONV_CH), row(W_Q),
                  row(W_KV), halo(WINDOW, W_KV), row(W_KV), halo(WINDOW, W_KV), row(W_QM),
                  pl.BlockSpec((MEM_LEN, W_QM), lambda b, i: (b, 0)),
                  pl.BlockSpec((MEM_LEN, W_QM), lambda b, i: (b, 1)),
                  _const_spec((SWA_HEADS, WINDOW, 2 * WINDOW)),
                  _const_spec((CONV_WIDTH, CONV_CH)), _const_spec((1, CONV_CH)),
                  _const_spec((1, CONV_CH)), _const_spec((1, CONV_CH))],
        out_specs=[row(CONV_CH), row(W_Q), row(W_QM)],
        out_shape=[jax.ShapeDtypeStruct((t, CONV_CH), BF16), jax.ShapeDtypeStruct((t, W_Q), BF16),
                   jax.ShapeDtypeStruct((t, W_QM), BF16)],
        scratch_shapes=[pltpu.VMEM((CONV_HALO + tq, CONV_CH), F32),
                        pltpu.VMEM((4, WINDOW + tq, LANES), BF16),
                        pltpu.VMEM((4, WINDOW + tq, LANES), BF16)],
        compiler_params=_params("parallel", "arbitrary"),
        name="branches_prompt",
    )(sinks, u, u, q, k, k, v, v, qm, mkv, mkv, bias, dww, dwb, lng, lnb)


def _branches_sample_kernel(sinks_ref, u_ref, cc_ref, qx_ref, kn_ref, vn_ref, ck_ref, cv_ref, qm_ref,
                            cmk_ref, cmv_ref, bias_ref, bias0_ref, dww_ref, dwb_ref, lng_ref, lnb_ref,
                            conv_ref, swa_ref, mem_ref, *, nb):
    hist = CONV_WIDTH - 1
    y = jnp.sum(cc_ref[...] * dww_ref[0:hist, :][None], axis=1)
    y = y + u_ref[...] * dww_ref[hist:hist + 1, :] + dwb_ref[...]
    conv_ref[...] = _ln_silu(y, lng_ref[...], lnb_ref[...]).astype(BF16)

    row = lax.broadcasted_iota(jnp.int32, (SWA_HEADS, W_QM), 0)
    col = lax.broadcasted_iota(jnp.int32, (SWA_HEADS, W_QM), 1)
    own = (col // MEM_HEAD_DIM) == row
    sink = sinks_ref[...]
    for n in range(nb):
        qx = qx_ref[n]
        s = _dot_nt(qx, ck_ref[n].astype(BF16)) + bias_ref[...]
        kn = kn_ref[n:n + 1, :].astype(BF16).astype(F32)
        s_new = jnp.sum(qx.astype(F32) * kn, axis=-1, keepdims=True) + bias0_ref[...]
        m = jnp.maximum(jnp.maximum(jnp.max(s, axis=-1, keepdims=True), s_new), sink)
        pr, pr_new = jnp.exp(s - m), jnp.exp(s_new - m)
        den = jnp.sum(pr, axis=-1, keepdims=True) + pr_new + jnp.exp(sink - m)
        vn = vn_ref[n:n + 1, :].astype(BF16).astype(F32)
        o = _dot((pr / den).astype(BF16), cv_ref[n].astype(BF16))
        swa_ref[n] = o + (pr_new / den).astype(BF16).astype(F32) * vn

        qmx = jnp.where(own, qm_ref[n:n + 1, :].astype(F32), 0.0).astype(BF16)
        w = _softmax_rows(_dot_nt(qmx, cmk_ref[n].astype(BF16)) * MEM_SCALE)
        om = _dot(w.astype(BF16), cmv_ref[n].astype(BF16))
        mem_ref[n:n + 1, :] = jnp.sum(jnp.where(own, om, 0.0), axis=0, keepdims=True).astype(BF16)


def _branches_sample(sinks, u, cache_conv, qx, kn, vn, ck, cv, qm, cmk, cmv, bias, bias0,
                     dww, dwb, lng, lnb, nb):
    n = u.shape[0]
    row = lambda c: pl.BlockSpec((nb, c), lambda i: (i, 0))
    blk3 = lambda a, c: pl.BlockSpec((nb, a, c), lambda i: (i, 0, 0))
    return pl.pallas_call(
        functools.partial(_branches_sample_kernel, nb=nb),
        grid=(n // nb,),
        in_specs=[_const_spec((SWA_HEADS, 1)),
                  row(CONV_CH), blk3(CONV_WIDTH - 1, CONV_CH), blk3(SWA_HEADS, LANES),
                  row(W_KV), row(W_KV), blk3(WINDOW, W_KV), blk3(WINDOW, W_KV), row(W_QM),
                  blk3(MEM_LEN, W_QM), blk3(MEM_LEN, W_QM),
                  _const_spec((SWA_HEADS, WINDOW)), _const_spec((SWA_HEADS, 1)),
                  _const_spec((CONV_WIDTH, CONV_CH)), _const_spec((1, CONV_CH)),
                  _const_spec((1, CONV_CH)), _const_spec((1, CONV_CH))],
        out_specs=[row(CONV_CH), blk3(SWA_HEADS, LANES), row(W_QM)],
        out_shape=[jax.ShapeDtypeStruct((n, CONV_CH), BF16),
                   jax.ShapeDtypeStruct((n, SWA_HEADS, LANES), F32),
                   jax.ShapeDtypeStruct((n, W_QM), BF16)],
        compiler_params=_params("parallel"),
        name="branches_sample",
    )(sinks, u, cache_conv, qx, kn, vn, ck, cv, qm, cmk, cmv, bias, bias0, dww, dwb, lng, lnb)


def _merge_kernel(x_ref, g_ref, conv_ref, swa_ref, mem_ref, wg_ref, wco_ref, wso_ref, wmo_ref, wo_ref,
                  o_ref):
    x = x_ref[...]
    h = _rms(x, g_ref[...]).astype(BF16)
    merged = None
    for br, (a_ref, w_ref) in enumerate(((conv_ref, wco_ref), (swa_ref, wso_ref), (mem_ref, wmo_ref))):
        gate = jax.nn.sigmoid(_dot(h, wg_ref[:, br * D_MODEL:(br + 1) * D_MODEL]))
        term = gate * _dot(a_ref[...], w_ref[...])
        merged = term if merged is None else merged + term
    o_ref[...] = x + _dot(merged.astype(BF16), wo_ref[...])


def _merge(x, g, conv, swa, mem, wg, wco, wso, wmo, wo, tm):
    t = x.shape[0]
    row = lambda n: pl.BlockSpec((tm, n), lambda i: (i, 0))
    return pl.pallas_call(
        _merge_kernel,
        grid=(t // tm,),
        in_specs=[row(D_MODEL), _const_spec((1, D_MODEL)), row(CONV_CH), row(W_Q), row(W_QM),
                  _const_spec((D_MODEL, 3 * D_MODEL)), _const_spec((CONV_CH, D_MODEL)),
                  _const_spec((W_Q, D_MODEL)), _const_spec((W_QM, D_MODEL)),
                  _const_spec((D_MODEL, D_MODEL))],
        out_specs=row(D_MODEL),
        out_shape=jax.ShapeDtypeStruct((t, D_MODEL), F32),
        compiler_params=_params("parallel"),
        name="merge",
    )(x, g, conv, swa, mem, wg, wco, wso, wmo, wo)


def _gelu(x):
    return 0.5 * x * (1.0 + lax.erf(x * (2.0 ** -0.5)))


def _top_values(arrs, count, with_rank=False):
    out = []
    ranks = [jnp.full(a.shape, float(count), F32) for a in arrs] if with_rank else None
    for it in range(count):
        m = jnp.max(functools.reduce(jnp.maximum, arrs), axis=0, keepdims=True)
        out.append(m)
        hit = [a == m for a in arrs]
        if with_rank:
            ranks = [jnp.where(hh, float(it), rk) for hh, rk in zip(hit, ranks)]
        arrs = [jnp.where(hh, -jnp.inf, a) for hh, a in zip(hit, arrs)]
    return (out, ranks) if with_rank else out


def _peer_route(h, qt, keys_ref, l_s, e0_s, r1_s, e1_s, sv0_s, sv1_s):
    nk, tb = PEER_N_KEYS, qt.shape[1]
    r = 2 * h * PEER_DK_HALF
    s0_all = _dot(keys_ref[h, 0], qt[r:r + PEER_DK_HALF, :])
    s1_all = _dot(keys_ref[h, 1], qt[r + PEER_DK_HALF:r + 2 * PEER_DK_HALF, :])
    for lg in range(tb // LANES):
        lanes = slice(lg * LANES, (lg + 1) * LANES)
        s0, s1 = s0_all[:, lanes], s1_all[:, lanes]
        blocks = lambda a: [a[8 * j:8 * j + 8, :] for j in range(nk // 8)]
        top0 = _top_values(blocks(s0), PEER_TOPK)
        top1, rank1 = _top_values(blocks(s1), PEER_TOPK, with_rank=True)
        for j in range(PEER_TOPK):
            sv0_s[j:j + 1, lanes] = top0[j]
            sv1_s[j:j + 1, lanes] = top1[j]
        a0, a1 = sv0_s[0:8, lanes], sv0_s[8:16, lanes]
        b0, b1 = sv1_s[0:8, lanes], sv1_s[8:16, lanes]
        cands = [a0[0:1] + b0, a0[0:1] + b1] + [a0[a:a + 1] + b0 for a in range(1, 8)] + [a1 + b0[0:1]]
        best = _top_values(cands, PEER_TOPK)
        sel = [cd >= best[PEER_TOPK - 1] for cd in cands]
        z = functools.reduce(jnp.add, [jnp.where(sl, jnp.exp(cd - best[0]), 0.0) for sl, cd in zip(sel, cands)])
        z = jnp.sum(z, axis=0, keepdims=True)
        cnt = [jnp.sum(jnp.where(sl, 1.0, 0.0), axis=0, keepdims=True) for sl in sel[:9]]
        counts = [cnt[0] + cnt[1]] + cnt[2:9]
        tail = jnp.where(sel[9], 1.0, 0.0)
        lrow = jnp.zeros((nk, LANES), F32)
        for a in range(PEER_TOPK):
            lrow = jnp.where(s0 == top0[a], counts[a] if a < 8 else tail[a - 8:a - 7], lrow)
        l_s[h, :, lanes] = lrow
        e0_s[h, :, lanes] = jnp.exp(s0 - top0[0]) / z
        r1_s[h, :, lanes] = jnp.concatenate(rank1, axis=0).astype(BF16)
        e1_s[h, :, lanes] = jnp.exp(s1 - top1[0]).astype(BF16)


def _peer_gate_chunk(ci, at_ref, ct_ref, lrow_s, erow_s, r1_s, e1_s, cb):
    nk = PEER_N_KEYS
    for j in range(cb):
        g = None
        for h in range(PEER_HEADS):
            lrow = lrow_s[ci, h, j:j + 1, :].astype(BF16)
            erow = erow_s[ci, h, j:j + 1, :].astype(BF16)
            term = jnp.where(r1_s[h] < lrow, e1_s[h], jnp.zeros((), BF16)) * erow
            g = term if g is None else g + term
        rows = slice((ci * cb + j) * nk, (ci * cb + j + 1) * nk)
        ct_ref[rows, :] = _gelu(at_ref[j * nk:(j + 1) * nk, :]).astype(BF16) * g


def _peer_kernel(x_ref, g2_ref, gf_ref, wqt_ref, keys_ref, wda_ref, wdb_ref, wut_ref, o_ref,
                 h_s, l_s, e0_s, r1_s, e1_s, sv0_s, sv1_s, lrow_s, erow_s, at0_s, at1_s, ct_s, acc_s, *, cb):
    s = pl.program_id(1)

    @pl.when(s == 0)
    def _first():
        hb = _rms(x_ref[...], g2_ref[...]).astype(BF16)
        h_s[...] = hb
        qt = _dot_nt(wqt_ref[...], hb).astype(BF16)
        for h in range(PEER_HEADS):
            _peer_route(h, qt, keys_ref, l_s, e0_s, r1_s, e1_s, sv0_s, sv1_s)
        at0_s[...] = _dot_nt(wda_ref[...], hb)
        acc_s[...] = jnp.zeros_like(acc_s)

    @pl.when(s > 0)
    def _steady():
        for ci in range(2):
            base = pl.multiple_of((2 * s - 2 + ci) * cb, cb)
            for h in range(PEER_HEADS):
                lrow_s[ci, h] = l_s[h, pl.ds(base, cb), :]
                erow_s[ci, h] = e0_s[h, pl.ds(base, cb), :]
        gate = functools.partial(_peer_gate_chunk, ct_ref=ct_s, lrow_s=lrow_s, erow_s=erow_s,
                                 r1_s=r1_s, e1_s=e1_s, cb=cb)
        at1_s[...] = _dot_nt(wda_ref[...], h_s[...])
        gate(0, at0_s)
        at0_s[...] = _dot_nt(wdb_ref[...], h_s[...])
        gate(1, at1_s)
        acc_s[...] += _dot(wut_ref[...], ct_s[...])

    @pl.when(s == pl.num_programs(1) - 1)
    def _last():
        y = x_ref[...] + acc_s[...].T
        o_ref[...] = _rms(y, gf_ref[...])


def _peer(x, g2, gf, wqt, keys, wd, wut, tb, cb):
    t = x.shape[0]
    ec = cb * PEER_N_KEYS
    nch = wd.shape[0] // ec
    assert nch % 2 == 0 and tb % LANES == 0
    stat = lambda dt: pltpu.VMEM((PEER_HEADS, PEER_N_KEYS, tb), dt)
    return pl.pallas_call(
        functools.partial(_peer_kernel, cb=cb),
        grid=(t // tb, nch // 2 + 1),
        in_specs=[pl.BlockSpec((tb, D_MODEL), lambda i, s: (i, 0)),
                  _const_spec((1, D_MODEL)), _const_spec((1, D_MODEL)),
                  _const_spec((D_MODEL, D_MODEL)),
                  _const_spec((PEER_HEADS, 2, PEER_N_KEYS, PEER_DK_HALF)),
                  pl.BlockSpec((ec, D_MODEL), lambda i, s: (jnp.maximum(2 * s - 1, 0), 0)),
                  pl.BlockSpec((ec, D_MODEL), lambda i, s: (jnp.minimum(2 * s, nch - 1), 0)),
                  pl.BlockSpec((D_MODEL, 2 * ec), lambda i, s: (0, jnp.maximum(s - 1, 0)))],
        out_specs=pl.BlockSpec((tb, D_MODEL), lambda i, s: (i, 0)),
        out_shape=jax.ShapeDtypeStruct((t, D_MODEL), F32),
        scratch_shapes=[pltpu.VMEM((tb, D_MODEL), BF16), stat(F32), stat(F32), stat(BF16), stat(BF16),
                        pltpu.VMEM((PEER_TOPK, tb), F32), pltpu.VMEM((PEER_TOPK, tb), F32),
                        pltpu.VMEM((2, PEER_HEADS, cb, tb), F32), pltpu.VMEM((2, PEER_HEADS, cb, tb), F32),
                        pltpu.VMEM((ec, tb), F32), pltpu.VMEM((ec, tb), F32),
                        pltpu.VMEM((2 * ec, tb), BF16), pltpu.VMEM((D_MODEL, tb), F32)],
        compiler_params=_params("parallel", "arbitrary"),
        name="peer",
    )(x, g2, gf, wqt, keys, wd, wd, wut)


def _rel_bucket(dist):
    n = jnp.maximum(dist, 0)
    max_exact = REL_BUCKETS // 2
    nf = jnp.maximum(n, 1).astype(F32)
    large = max_exact + (jnp.log(nf / max_exact) / jnp.log(REL_MAX_DIST / max_exact)
                         * (REL_BUCKETS - max_exact)).astype(jnp.int32)
    return jnp.where(n < max_exact, n, jnp.minimum(large, REL_BUCKETS - 1))


def _tile(t, cap):
    tm = min(t, cap)
    assert t % tm == 0, (t, tm)
    return tm


def kernel(x_prompt, x_sample, cache_conv, cache_swa_k, cache_swa_v, cache_mem_k, cache_mem_v, mem_prompt, rel_bias_table, norm1_g, w_in, conv_dw_w, conv_dw_b, conv_ln_g, conv_ln_b, w_conv_out, swa_sinks, w_swa_out, mem_norm_g, w_mem_kv, w_mem_out, w_out, norm2_g, peer_w_q, peer_keys, peer_w_down, peer_w_up, final_norm_g):
    assert w_in.shape[0] == 1, "single layer"
    batch, seq, _ = x_prompt.shape
    nsamp = x_sample.shape[0]
    assert x_sample.shape[1] == 1 and seq % WINDOW == 0
    row = lambda a: a.reshape(1, -1)

    w_proj = w_in[0, :, :W_PROJ].astype(BF16)
    w_gate = w_in[0, :, W_PROJ:].astype(BF16)
    g1, g2, gf = row(norm1_g[0]), row(norm2_g[0]), row(final_norm_g)
    dww, dwb = conv_dw_w[0], row(conv_dw_b[0])
    lng, lnb = row(conv_ln_g[0]), row(conv_ln_b[0])
    wco, wso = w_conv_out[0].astype(BF16), w_swa_out[0].astype(BF16)
    wmo, wo = w_mem_out[0].astype(BF16), w_out[0].astype(BF16)
    wqt = peer_w_q[0].T.astype(BF16)
    keys = peer_keys[0].astype(BF16)
    wd = peer_w_down.reshape(-1, D_MODEL).astype(BF16)
    wut = peer_w_up.reshape(-1, D_MODEL).astype(BF16).T
    sinks = swa_sinks[0]

    qi = jnp.arange(WINDOW)[:, None]
    ki = jnp.arange(2 * WINDOW)[None, :]
    def table_rows(dist):
        onehot = (_rel_bucket(dist)[..., None] == jnp.arange(REL_BUCKETS)).astype(F32)
        return jnp.einsum("...b,bh->h...", onehot, rel_bias_table.astype(F32),
                          precision=lax.Precision.HIGHEST)

    bias_p = table_rows(WINDOW + qi - ki)
    bias_s = table_rows(WINDOW - jnp.arange(WINDOW))
    bias_0 = table_rows(jnp.zeros((1,), jnp.int32))

    xp = x_prompt.reshape(batch * seq, D_MODEL)
    mkv = _norm_matmul(mem_prompt.reshape(batch * MEM_LEN, D_MODEL), row(mem_norm_g[0]),
                       w_mem_kv[0].astype(BF16), MEM_LEN)
    u_p, q_p, k_p, v_p, qm_p = _in_proj(xp, g1, w_proj, _tile(batch * seq, 512))
    tq = _tile(seq, 512)
    conv_p, swa_p, mem_p = _branches_prompt(sinks, u_p, q_p, k_p, v_p, qm_p, mkv, bias_p,
                                            dww, dwb, lng, lnb, batch, seq, tq)
    x2_p = _merge(xp, g1, conv_p, swa_p, mem_p, w_gate, wco, wso, wmo, wo, _tile(batch * seq, 512))
    y_p = _peer(x2_p, g2, gf, wqt, keys, wd, wut, _tile(batch * seq, 512), 8)

    xs = x_sample.reshape(nsamp, D_MODEL)
    u_s, q_s, k_s, v_s, qm_s = _in_proj(xs, g1, w_proj, _tile(nsamp, 128))
    q4 = q_s.reshape(nsamp, SWA_KV_HEADS, SWA_HEADS // SWA_KV_HEADS, SWA_HEAD_DIM)
    zq = jnp.zeros_like(q4[:, 0])
    qx = jnp.concatenate([jnp.concatenate([q4[:, 0], zq], -1), jnp.concatenate([zq, q4[:, 1]], -1)], 1)
    ck = cache_swa_k[0].reshape(nsamp, WINDOW, W_KV)
    cv = cache_swa_v[0].reshape(nsamp, WINDOW, W_KV)
    cmk = cache_mem_k[0].reshape(nsamp, MEM_LEN, W_QM)
    cmv = cache_mem_v[0].reshape(nsamp, MEM_LEN, W_QM)
    conv_s, swa_x, mem_s = _branches_sample(sinks.reshape(SWA_HEADS, 1), u_s, cache_conv[0], qx, k_s, v_s,
                                            ck, cv, qm_s, cmk, cmv, bias_s, bias_0,
                                            dww, dwb, lng, lnb, _tile(nsamp, 8))
    sx = swa_x.reshape(nsamp, SWA_KV_HEADS, SWA_HEADS // SWA_KV_HEADS, SWA_KV_HEADS, SWA_HEAD_DIM)
    swa_s = jnp.stack([sx[:, g, :, g] for g in range(SWA_KV_HEADS)], 1).reshape(nsamp, W_Q).astype(BF16)
    x2_s = _merge(xs, g1, conv_s, swa_s, mem_s, w_gate, wco, wso, wmo, wo, _tile(nsamp, 128))
    y_s = _peer(x2_s, g2, gf, wqt, keys, wd, wut, _tile(nsamp, 128), 8)

    hist = CONV_WIDTH - 1
    kv_shape = (SWA_KV_HEADS, SWA_HEAD_DIM)
    mkv5 = mkv.reshape(batch, MEM_LEN, 2, MEM_HEADS, MEM_HEAD_DIM)
    conv_state_p = u_p.reshape(batch, seq, CONV_CH)[:, -hist:]
    swa_k_p = k_p.reshape(batch, seq, *kv_shape)[:, -WINDOW:]
    swa_v_p = v_p.reshape(batch, seq, *kv_shape)[:, -WINDOW:]
    conv_state_s = jnp.concatenate([cache_conv[0][:, 1:], u_s[:, None, :]], axis=1)
    swa_k_s = jnp.concatenate([cache_swa_k[0][:, 1:], k_s.reshape(nsamp, 1, *kv_shape)], axis=1)
    swa_v_s = jnp.concatenate([cache_swa_v[0][:, 1:], v_s.reshape(nsamp, 1, *kv_shape)], axis=1)
    return (y_p.reshape(batch, seq, D_MODEL), y_s.reshape(nsamp, 1, D_MODEL),
            conv_state_p[None], swa_k_p[None], swa_v_p[None],
            mkv5[:, :, 0][None], mkv5[:, :, 1][None],
            conv_state_s[None], swa_k_s[None], swa_v_s[None])
```

```python
import functools

import jax
import jax.numpy as jnp
from jax import lax
from jax.experimental import pallas as pl
from jax.experimental.pallas import tpu as pltpu

F32 = jnp.float32
BF16 = jnp.bfloat16

D_MODEL = 1024
PAST_LEN = 16384
MEM_LEN = 256
CONV_CH = 512
CONV_WIDTH = 31
SWA_HEADS = 8
SWA_KV_HEADS = 2
SWA_HEAD_DIM = 64
WINDOW = 128
SWA_SCALE = SWA_HEAD_DIM ** -0.5
MEM_HEADS = 4
MEM_HEAD_DIM = 128
MEM_SCALE = MEM_HEAD_DIM ** -0.5
REL_BUCKETS = 32
REL_MAX_DIST = 128
PEER_HEADS = 8
PEER_N_KEYS = 128
PEER_DK_HALF = 64
PEER_TOPK = 16
EPS = 1e-6
NEG_INF = -1e30

W_GLU = 2 * CONV_CH
W_Q = SWA_HEADS * SWA_HEAD_DIM
W_KV = SWA_KV_HEADS * SWA_HEAD_DIM
W_QM = MEM_HEADS * MEM_HEAD_DIM
W_PROJ = W_GLU + W_Q + 2 * W_KV + W_QM

VMEM_LIMIT_BYTES = 56 * 1024 * 1024
LANES = 128
SUBLANES = 8
CONV_HALO = 32


def _params(*sem, flags=None):
    return pltpu.CompilerParams(dimension_semantics=sem, vmem_limit_bytes=VMEM_LIMIT_BYTES, flags=flags)


def _rms(x, g):
    return x * lax.rsqrt(jnp.mean(x * x, axis=-1, keepdims=True) + EPS) * g


def _dot(a, b):
    return jnp.dot(a, b, preferred_element_type=F32)


def _dot_nt(a, b):
    return lax.dot_general(a, b, (((1,), (1,)), ((), ())), preferred_element_type=F32)


def _const_spec(shape):
    zeros = (0,) * len(shape)
    return pl.BlockSpec(shape, lambda *_: zeros)


def _norm_matmul_kernel(x_ref, g_ref, w_ref, o_ref):
    o_ref[...] = _dot(_rms(x_ref[...], g_ref[...]).astype(BF16), w_ref[...])


def _norm_matmul(x, g, w, tm):
    t, n = x.shape[0], w.shape[1]
    return pl.pallas_call(
        _norm_matmul_kernel,
        grid=(t // tm,),
        in_specs=[pl.BlockSpec((tm, D_MODEL), lambda i: (i, 0)), _const_spec((1, D_MODEL)),
                  _const_spec((D_MODEL, n))],
        out_specs=pl.BlockSpec((tm, n), lambda i: (i, 0)),
        out_shape=jax.ShapeDtypeStruct((t, n), F32),
        compiler_params=_params("parallel"),
        name="memkv",
    )(x, g, w)


def _in_proj_kernel(x_ref, g_ref, w_ref, u_ref, q_ref, k_ref, v_ref, qm_ref):
    z = _dot(_rms(x_ref[...], g_ref[...]).astype(BF16), w_ref[...])
    a, b = z[:, :CONV_CH], z[:, CONV_CH:W_GLU]
    u_ref[...] = a * jax.nn.sigmoid(b)
    c = W_GLU
    q_ref[...] = (z[:, c:c + W_Q] * SWA_SCALE).astype(BF16)
    c += W_Q
    k_ref[...] = z[:, c:c + W_KV]
    c += W_KV
    v_ref[...] = z[:, c:c + W_KV]
    c += W_KV
    qm_ref[...] = z[:, c:c + W_QM].astype(BF16)


def _in_proj(x, g, w, tm):
    t = x.shape[0]
    row = lambda n: pl.BlockSpec((tm, n), lambda i: (i, 0))
    return pl.pallas_call(
        _in_proj_kernel,
        grid=(t // tm,),
        in_specs=[row(D_MODEL), _const_spec((1, D_MODEL)), _const_spec((D_MODEL, W_PROJ))],
        out_specs=[row(CONV_CH), row(W_Q), row(W_KV), row(W_KV), row(W_QM)],
        out_shape=[jax.ShapeDtypeStruct((t, CONV_CH), F32), jax.ShapeDtypeStruct((t, W_Q), BF16),
                   jax.ShapeDtypeStruct((t, W_KV), F32), jax.ShapeDtypeStruct((t, W_KV), F32),
                   jax.ShapeDtypeStruct((t, W_QM), BF16)],
        compiler_params=_params("parallel"),
        name="in_proj",
    )(x, g, w)


def _ln_silu(y, g, b):
    mu = jnp.mean(y, axis=-1, keepdims=True)
    var = jnp.mean(jnp.square(y - mu), axis=-1, keepdims=True)
    y = (y - mu) * lax.rsqrt(var + EPS) * g + b
    return y * jax.nn.sigmoid(y)


def _softmax_rows(s):
    e = jnp.exp(s - jnp.max(s, axis=-1, keepdims=True))
    return e / jnp.sum(e, axis=-1, keepdims=True)


def _branches_prompt_kernel(sinks_ref, u_ref, uh_ref, q_ref, k_ref, kp_ref, v_ref, vp_ref, qm_ref,
                            mk_ref, mv_ref, bias_ref, dww_ref, dwb_ref, lng_ref, lnb_ref,
                            conv_ref, swa_ref, mem_ref, ubuf, ushift, kbuf, vbuf, *, tq):
    i = pl.program_id(1)
    first = i == 0

    ubuf[0:CONV_HALO, :] = jnp.where(first, 0.0, uh_ref[...])
    ubuf[CONV_HALO:CONV_HALO + tq, :] = u_ref[...]
    nshift = ushift.shape[1]
    for b in range(1, SUBLANES):
        ushift[b] = ubuf[b:b + nshift, :]
    rb = 64
    off = CONV_HALO - (CONV_WIDTH - 1)
    for r in range(tq // rb):
        acc = jnp.broadcast_to(dwb_ref[...], (rb, CONV_CH))
        for j in range(CONV_WIDTH):
            a, b = divmod(off + j, SUBLANES)
            start = r * rb + a * SUBLANES
            rows = ubuf[start:start + rb, :] if b == 0 else ushift[b, start:start + rb, :]
            acc = acc + rows * dww_ref[j:j + 1, :]
        conv_ref[r * rb:(r + 1) * rb, :] = _ln_silu(acc, lng_ref[...], lnb_ref[...]).astype(BF16)

    lane = lax.broadcasted_iota(jnp.int32, (WINDOW + tq, LANES), 1)
    lo = lane < SWA_HEAD_DIM
    for src_ref, prev_ref, buf in ((k_ref, kp_ref, kbuf), (v_ref, vp_ref, vbuf)):
        full = jnp.concatenate([jnp.where(first, 0.0, prev_ref[...]), src_ref[...]], axis=0)
        rolled = pltpu.roll(full, SWA_HEAD_DIM, 1)
        buf[0] = jnp.where(lo, full, 0.0).astype(BF16)
        buf[1] = jnp.where(lo, 0.0, rolled).astype(BF16)
        buf[2] = jnp.where(lo, rolled, 0.0).astype(BF16)
        buf[3] = jnp.where(lo, 0.0, full).astype(BF16)

    qi = lax.broadcasted_iota(jnp.int32, (WINDOW, 2 * WINDOW), 0)
    ki = lax.broadcasted_iota(jnp.int32, (WINDOW, 2 * WINDOW), 1)
    dist = WINDOW + qi - ki
    band = (dist >= 0) & (dist <= WINDOW)
    nqb = tq // WINDOW
    for jb in range(nqb):
        r0 = jb * WINDOW
        kmin = jnp.where(i * nqb + jb > 0, 0, WINDOW)
        mask = band & (ki >= kmin)
        for p in range(SWA_HEADS // 2):
            g = p // 2
            qp = q_ref[r0:r0 + WINDOW, LANES * p:LANES * (p + 1)]
            o = None
            for half in range(2):
                h = 2 * p + half
                s = _dot_nt(qp, kbuf[2 * g + half, r0:r0 + 2 * WINDOW, :]) + bias_ref[h]
                s = jnp.where(mask, s, NEG_INF)
                sink = sinks_ref[h]
                m = jnp.maximum(jnp.max(s, axis=-1, keepdims=True), sink)
                pr = jnp.exp(s - m)
                pr = pr / (jnp.sum(pr, axis=-1, keepdims=True) + jnp.exp(sink - m))
                t = _dot(pr.astype(BF16), vbuf[2 * g + half, r0:r0 + 2 * WINDOW, :])
                o = t if o is None else o + t
            swa_ref[r0:r0 + WINDOW, LANES * p:LANES * (p + 1)] = o.astype(BF16)

    for hm in range(MEM_HEADS):
        sl = slice(hm * MEM_HEAD_DIM, (hm + 1) * MEM_HEAD_DIM)
        w = _softmax_rows(_dot_nt(qm_ref[:, sl], mk_ref[:, sl].astype(BF16)) * MEM_SCALE)
        mem_ref[:, sl] = _dot(w.astype(BF16), mv_ref[:, sl].astype(BF16)).astype(BF16)


def _branches_prompt(sinks, u, q, k, v, qm, mkv, bias, dww, dwb, lng, lnb, batch, seq, tq):
    t = batch * seq
    nq = seq // tq
    row = lambda n: pl.BlockSpec((tq, n), lambda b, i: (b * nq + i, 0))
    halo = lambda rows, n: pl.BlockSpec(
        (rows, n), lambda b, i: (jnp.maximum(b * (seq // rows) + i * (tq // rows) - 1, 0), 0))
    return pl.pallas_call(
        functools.partial(_branches_prompt_kernel, tq=tq),
        grid=(batch, nq),
        in_specs=[pl.BlockSpec(memory_space=pltpu.SMEM),
                  row(CONV_CH), halo(CONV_HALO, CONV_CH), row(W_Q),
                  row(W_KV), halo(WINDOW, W_KV), row(W_KV), halo(WINDOW, W_KV), row(W_QM),
                  pl.BlockSpec((MEM_LEN, W_QM), lambda b, i: (b, 0)),
                  pl.BlockSpec((MEM_LEN, W_QM), lambda b, i: (b, 1)),
                  _const_spec((SWA_HEADS, WINDOW, 2 * WINDOW)),
                  _const_spec((CONV_WIDTH, CONV_CH)), _const_spec((1, CONV_CH)),
                  _const_spec((1, CONV_CH)), _const_spec((1, CONV_CH))],
        out_specs=[row(CONV_CH), row(W_Q), row(W_QM)],
        out_shape=[jax.ShapeDtypeStruct((t, CONV_CH), BF16), jax.ShapeDtypeStruct((t, W_Q), BF16),
                   jax.ShapeDtypeStruct((t, W_QM), BF16)],
        scratch_shapes=[pltpu.VMEM((CONV_HALO + tq, CONV_CH), F32),
                        pltpu.VMEM((SUBLANES, CONV_HALO + tq - SUBLANES, CONV_CH), F32),
                        pltpu.VMEM((4, WINDOW + tq, LANES), BF16),
                        pltpu.VMEM((4, WINDOW + tq, LANES), BF16)],
        compiler_params=_params("parallel", "arbitrary"),
        name="branches_prompt",
    )(sinks, u, u, q, k, k, v, v, qm, mkv, mkv, bias, dww, dwb, lng, lnb)


def _branches_sample_kernel(sinks_ref, u_ref, cc_ref, qx_ref, kn_ref, vn_ref, ck_ref, cv_ref, qm_ref,
                            cmk_ref, cmv_ref, bias_ref, bias0_ref, dww_ref, dwb_ref, lng_ref, lnb_ref,
                            conv_ref, swa_ref, mem_ref, *, nb):
    hist = CONV_WIDTH - 1
    y = jnp.sum(cc_ref[...] * dww_ref[0:hist, :][None], axis=1)
    y = y + u_ref[...] * dww_ref[hist:hist + 1, :] + dwb_ref[...]
    conv_ref[...] = _ln_silu(y, lng_ref[...], lnb_ref[...]).astype(BF16)

    row = lax.broadcasted_iota(jnp.int32, (SWA_HEADS, W_QM), 0)
    col = lax.broadcasted_iota(jnp.int32, (SWA_HEADS, W_QM), 1)
    own = (col // MEM_HEAD_DIM) == row
    sink = sinks_ref[...]
    for n in range(nb):
        qx = qx_ref[n]
        s = _dot_nt(qx, ck_ref[n].astype(BF16)) + bias_ref[...]
        kn = kn_ref[n:n + 1, :].astype(BF16).astype(F32)
        s_new = jnp.sum(qx.astype(F32) * kn, axis=-1, keepdims=True) + bias0_ref[...]
        m = jnp.maximum(jnp.maximum(jnp.max(s, axis=-1, keepdims=True), s_new), sink)
        pr, pr_new = jnp.exp(s - m), jnp.exp(s_new - m)
        den = jnp.sum(pr, axis=-1, keepdims=True) + pr_new + jnp.exp(sink - m)
        vn = vn_ref[n:n + 1, :].astype(BF16).astype(F32)
        o = _dot((pr / den).astype(BF16), cv_ref[n].astype(BF16))
        swa_ref[n] = o + (pr_new / den).astype(BF16).astype(F32) * vn

        qmx = jnp.where(own, qm_ref[n:n + 1, :].astype(F32), 0.0).astype(BF16)
        w = _softmax_rows(_dot_nt(qmx, cmk_ref[n].astype(BF16)) * MEM_SCALE)
        om = _dot(w.astype(BF16), cmv_ref[n].astype(BF16))
        mem_ref[n:n + 1, :] = jnp.sum(jnp.where(own, om, 0.0), axis=0, keepdims=True).astype(BF16)


def _branches_sample(sinks, u, cache_conv, qx, kn, vn, ck, cv, qm, cmk, cmv, bias, bias0,
                     dww, dwb, lng, lnb, nb):
    n = u.shape[0]
    row = lambda c: pl.BlockSpec((nb, c), lambda i: (i, 0))
    blk3 = lambda a, c: pl.BlockSpec((nb, a, c), lambda i: (i, 0, 0))
    return pl.pallas_call(
        functools.partial(_branches_sample_kernel, nb=nb),
        grid=(n // nb,),
        in_specs=[_const_spec((SWA_HEADS, 1)),
                  row(CONV_CH), blk3(CONV_WIDTH - 1, CONV_CH), blk3(SWA_HEADS, LANES),
                  row(W_KV), row(W_KV), blk3(WINDOW, W_KV), blk3(WINDOW, W_KV), row(W_QM),
                  blk3(MEM_LEN, W_QM), blk3(MEM_LEN, W_QM),
                  _const_spec((SWA_HEADS, WINDOW)), _const_spec((SWA_HEADS, 1)),
                  _const_spec((CONV_WIDTH, CONV_CH)), _const_spec((1, CONV_CH)),
                  _const_spec((1, CONV_CH)), _const_spec((1, CONV_CH))],
        out_specs=[row(CONV_CH), blk3(SWA_HEADS, LANES), row(W_QM)],
        out_shape=[jax.ShapeDtypeStruct((n, CONV_CH), BF16),
                   jax.ShapeDtypeStruct((n, SWA_HEADS, LANES), F32),
                   jax.ShapeDtypeStruct((n, W_QM), BF16)],
        compiler_params=_params("parallel"),
        name="branches_sample",
    )(sinks, u, cache_conv, qx, kn, vn, ck, cv, qm, cmk, cmv, bias, bias0, dww, dwb, lng, lnb)


def _merge_kernel(x_ref, g_ref, conv_ref, swa_ref, mem_ref, wg_ref, wco_ref, wso_ref, wmo_ref, wo_ref,
                  o_ref):
    x = x_ref[...]
    h = _rms(x, g_ref[...]).astype(BF16)
    merged = None
    for br, (a_ref, w_ref) in enumerate(((conv_ref, wco_ref), (swa_ref, wso_ref), (mem_ref, wmo_ref))):
        gate = jax.nn.sigmoid(_dot(h, wg_ref[:, br * D_MODEL:(br + 1) * D_MODEL]))
        term = gate * _dot(a_ref[...], w_ref[...])
        merged = term if merged is None else merged + term
    o_ref[...] = x + _dot(merged.astype(BF16), wo_ref[...])


def _merge(x, g, conv, swa, mem, wg, wco, wso, wmo, wo, tm):
    t = x.shape[0]
    row = lambda n: pl.BlockSpec((tm, n), lambda i: (i, 0))
    return pl.pallas_call(
        _merge_kernel,
        grid=(t // tm,),
        in_specs=[row(D_MODEL), _const_spec((1, D_MODEL)), row(CONV_CH), row(W_Q), row(W_QM),
                  _const_spec((D_MODEL, 3 * D_MODEL)), _const_spec((CONV_CH, D_MODEL)),
                  _const_spec((W_Q, D_MODEL)), _const_spec((W_QM, D_MODEL)),
                  _const_spec((D_MODEL, D_MODEL))],
        out_specs=row(D_MODEL),
        out_shape=jax.ShapeDtypeStruct((t, D_MODEL), F32),
        compiler_params=_params("parallel"),
        name="merge",
    )(x, g, conv, swa, mem, wg, wco, wso, wmo, wo)


def _gelu(x):
    return 0.5 * x * (1.0 + lax.erf(x * (2.0 ** -0.5)))


def _top_values(arrs, count, with_rank=False):
    out = []
    ranks = [jnp.full(a.shape, float(count), F32) for a in arrs] if with_rank else None
    for it in range(count):
        m = jnp.max(functools.reduce(jnp.maximum, arrs), axis=0, keepdims=True)
        out.append(m)
        hit = [a == m for a in arrs]
        if with_rank:
            ranks = [jnp.where(hh, float(it), rk) for hh, rk in zip(hit, ranks)]
        arrs = [jnp.where(hh, -jnp.inf, a) for hh, a in zip(hit, arrs)]
    return (out, ranks) if with_rank else out


def _peer_route(h, qt, keys_ref, l_s, e0_s, r1_s, e1_s, sv0_s, sv1_s):
    nk, tb = PEER_N_KEYS, qt.shape[1]
    r = 2 * h * PEER_DK_HALF
    s0_all = _dot(keys_ref[h, 0], qt[r:r + PEER_DK_HALF, :])
    s1_all = _dot(keys_ref[h, 1], qt[r + PEER_DK_HALF:r + 2 * PEER_DK_HALF, :])
    for lg in range(tb // LANES):
        lanes = slice(lg * LANES, (lg + 1) * LANES)
        s0, s1 = s0_all[:, lanes], s1_all[:, lanes]
        blocks = lambda a: [a[8 * j:8 * j + 8, :] for j in range(nk // 8)]
        top0 = _top_values(blocks(s0), PEER_TOPK)
        top1, rank1 = _top_values(blocks(s1), PEER_TOPK, with_rank=True)
        for j in range(PEER_TOPK):
            sv0_s[j:j + 1, lanes] = top0[j]
            sv1_s[j:j + 1, lanes] = top1[j]
        a0, a1 = sv0_s[0:8, lanes], sv0_s[8:16, lanes]
        b0, b1 = sv1_s[0:8, lanes], sv1_s[8:16, lanes]
        cands = [a0[0:1] + b0, a0[0:1] + b1] + [a0[a:a + 1] + b0 for a in range(1, 8)] + [a1 + b0[0:1]]
        best = _top_values(cands, PEER_TOPK)
        sel = [cd >= best[PEER_TOPK - 1] for cd in cands]
        z = functools.reduce(jnp.add, [jnp.where(sl, jnp.exp(cd - best[0]), 0.0) for sl, cd in zip(sel, cands)])
        z = jnp.sum(z, axis=0, keepdims=True)
        cnt = [jnp.sum(jnp.where(sl, 1.0, 0.0), axis=0, keepdims=True) for sl in sel[:9]]
        counts = [cnt[0] + cnt[1]] + cnt[2:9]
        tail = jnp.where(sel[9], 1.0, 0.0)
        lrow = jnp.zeros((nk, LANES), F32)
        for a in range(PEER_TOPK):
            lrow = jnp.where(s0 == top0[a], counts[a] if a < 8 else tail[a - 8:a - 7], lrow)
        l_s[h, :, lanes] = lrow
        e0_s[h, :, lanes] = jnp.exp(s0 - top0[0]) / z
        r1_s[h, :, lanes] = jnp.concatenate(rank1, axis=0).astype(BF16)
        e1_s[h, :, lanes] = jnp.exp(s1 - top1[0]).astype(BF16)


def _peer_gate_chunk(ci, at_ref, ct_ref, lrow_s, erow_s, r1_s, e1_s, cb):
    nk = PEER_N_KEYS
    for j in range(cb):
        g = None
        for h in range(PEER_HEADS):
            lrow = lrow_s[ci, h, j:j + 1, :].astype(BF16)
            erow = erow_s[ci, h, j:j + 1, :].astype(BF16)
            term = jnp.where(r1_s[h] < lrow, e1_s[h], jnp.zeros((), BF16)) * erow
            g = term if g is None else g + term
        rows = slice((ci * cb + j) * nk, (ci * cb + j + 1) * nk)
        ct_ref[rows, :] = _gelu(at_ref[j * nk:(j + 1) * nk, :]).astype(BF16) * g


def _peer_kernel(x_ref, g2_ref, gf_ref, wqt_ref, keys_ref, wda_ref, wdb_ref, wut_ref, o_ref,
                 h_s, l_s, e0_s, r1_s, e1_s, sv0_s, sv1_s, lrow_s, erow_s, at0_s, at1_s, ct_s, acc_s, *, cb):
    s = pl.program_id(1)

    @pl.when(s == 0)
    def _first():
        hb = _rms(x_ref[...], g2_ref[...]).astype(BF16)
        h_s[...] = hb
        qt = _dot_nt(wqt_ref[...], hb).astype(BF16)
        for h in range(PEER_HEADS):
            _peer_route(h, qt, keys_ref, l_s, e0_s, r1_s, e1_s, sv0_s, sv1_s)
        at0_s[...] = _dot_nt(wda_ref[...], hb)
        acc_s[...] = jnp.zeros_like(acc_s)

    @pl.when(s > 0)
    def _steady():
        for ci in range(2):
            base = pl.multiple_of((2 * s - 2 + ci) * cb, cb)
            for h in range(PEER_HEADS):
                lrow_s[ci, h] = l_s[h, pl.ds(base, cb), :]
                erow_s[ci, h] = e0_s[h, pl.ds(base, cb), :]
        gate = functools.partial(_peer_gate_chunk, ct_ref=ct_s, lrow_s=lrow_s, erow_s=erow_s,
                                 r1_s=r1_s, e1_s=e1_s, cb=cb)
        at1_s[...] = _dot_nt(wda_ref[...], h_s[...])
        gate(0, at0_s)
        at0_s[...] = _dot_nt(wdb_ref[...], h_s[...])
        gate(1, at1_s)
        acc_s[...] += _dot(wut_ref[...], ct_s[...])

    @pl.when(s == pl.num_programs(1) - 1)
    def _last():
        y = x_ref[...] + acc_s[...].T
        o_ref[...] = _rms(y, gf_ref[...])


def _peer(x, g2, gf, wqt, keys, wd, wut, tb, cb):
    t = x.shape[0]
    ec = cb * PEER_N_KEYS
    nch = wd.shape[0] // ec
    assert nch % 2 == 0 and tb % LANES == 0
    stat = lambda dt: pltpu.VMEM((PEER_HEADS, PEER_N_KEYS, tb), dt)
    return pl.pallas_call(
        functools.partial(_peer_kernel, cb=cb),
        grid=(t // tb, nch // 2 + 1),
        in_specs=[pl.BlockSpec((tb, D_MODEL), lambda i, s: (i, 0)),
                  _const_spec((1, D_MODEL)), _const_spec((1, D_MODEL)),
                  _const_spec((D_MODEL, D_MODEL)),
                  _const_spec((PEER_HEADS, 2, PEER_N_KEYS, PEER_DK_HALF)),
                  pl.BlockSpec((ec, D_MODEL), lambda i, s: (jnp.maximum(2 * s - 1, 0), 0)),
                  pl.BlockSpec((ec, D_MODEL), lambda i, s: (jnp.minimum(2 * s, nch - 1), 0)),
                  pl.BlockSpec((D_MODEL, 2 * ec), lambda i, s: (0, jnp.maximum(s - 1, 0)))],
        out_specs=pl.BlockSpec((tb, D_MODEL), lambda i, s: (i, 0)),
        out_shape=jax.ShapeDtypeStruct((t, D_MODEL), F32),
        scratch_shapes=[pltpu.VMEM((tb, D_MODEL), BF16), stat(F32), stat(F32), stat(BF16), stat(BF16),
                        pltpu.VMEM((PEER_TOPK, tb), F32), pltpu.VMEM((PEER_TOPK, tb), F32),
                        pltpu.VMEM((2, PEER_HEADS, cb, tb), F32), pltpu.VMEM((2, PEER_HEADS, cb, tb), F32),
                        pltpu.VMEM((ec, tb), F32), pltpu.VMEM((ec, tb), F32),
                        pltpu.VMEM((2 * ec, tb), BF16), pltpu.VMEM((D_MODEL, tb), F32)],
        compiler_params=_params("parallel", "arbitrary"),
        name="peer",
    )(x, g2, gf, wqt, keys, wd, wd, wut)


def _rel_bucket(dist):
    n = jnp.maximum(dist, 0)
    max_exact = REL_BUCKETS // 2
    nf = jnp.maximum(n, 1).astype(F32)
    large = max_exact + (jnp.log(nf / max_exact) / jnp.log(REL_MAX_DIST / max_exact)
                         * (REL_BUCKETS - max_exact)).astype(jnp.int32)
    return jnp.where(n < max_exact, n, jnp.minimum(large, REL_BUCKETS - 1))


def _tile(t, cap):
    tm = min(t, cap)
    assert t % tm == 0, (t, tm)
    return tm


def kernel(x_prompt, x_sample, cache_conv, cache_swa_k, cache_swa_v, cache_mem_k, cache_mem_v, mem_prompt, rel_bias_table, norm1_g, w_in, conv_dw_w, conv_dw_b, conv_ln_g, conv_ln_b, w_conv_out, swa_sinks, w_swa_out, mem_norm_g, w_mem_kv, w_mem_out, w_out, norm2_g, peer_w_q, peer_keys, peer_w_down, peer_w_up, final_norm_g):
    assert w_in.shape[0] == 1, "single layer"
    batch, seq, _ = x_prompt.shape
    nsamp = x_sample.shape[0]
    assert x_sample.shape[1] == 1 and seq % WINDOW == 0
    row = lambda a: a.reshape(1, -1)

    w_proj = w_in[0, :, :W_PROJ].astype(BF16)
    w_gate = w_in[0, :, W_PROJ:].astype(BF16)
    g1, g2, gf = row(norm1_g[0]), row(norm2_g[0]), row(final_norm_g)
    dww, dwb = conv_dw_w[0], row(conv_dw_b[0])
    lng, lnb = row(conv_ln_g[0]), row(conv_ln_b[0])
    wco, wso = w_conv_out[0].astype(BF16), w_swa_out[0].astype(BF16)
    wmo, wo = w_mem_out[0].astype(BF16), w_out[0].astype(BF16)
    wqt = peer_w_q[0].T.astype(BF16)
    keys = peer_keys[0].astype(BF16)
    wd = peer_w_down.reshape(-1, D_MODEL).astype(BF16)
    wut = peer_w_up.reshape(-1, D_MODEL).astype(BF16).T
    sinks = swa_sinks[0]

    qi = jnp.arange(WINDOW)[:, None]
    ki = jnp.arange(2 * WINDOW)[None, :]
    def table_rows(dist):
        onehot = (_rel_bucket(dist)[..., None] == jnp.arange(REL_BUCKETS)).astype(F32)
        return jnp.einsum("...b,bh->h...", onehot, rel_bias_table.astype(F32),
                          precision=lax.Precision.HIGHEST)

    bias_p = table_rows(WINDOW + qi - ki)
    bias_s = table_rows(WINDOW - jnp.arange(WINDOW))
    bias_0 = table_rows(jnp.zeros((1,), jnp.int32))

    xp = x_prompt.reshape(batch * seq, D_MODEL)
    mkv = _norm_matmul(mem_prompt.reshape(batch * MEM_LEN, D_MODEL), row(mem_norm_g[0]),
                       w_mem_kv[0].astype(BF16), MEM_LEN)
    u_p, q_p, k_p, v_p, qm_p = _in_proj(xp, g1, w_proj, _tile(batch * seq, 512))
    tq = _tile(seq, 512)
    conv_p, swa_p, mem_p = _branches_prompt(sinks, u_p, q_p, k_p, v_p, qm_p, mkv, bias_p,
                                            dww, dwb, lng, lnb, batch, seq, tq)
    x2_p = _merge(xp, g1, conv_p, swa_p, mem_p, w_gate, wco, wso, wmo, wo, _tile(batch * seq, 512))
    y_p = _peer(x2_p, g2, gf, wqt, keys, wd, wut, _tile(batch * seq, 512), 8)

    xs = x_sample.reshape(nsamp, D_MODEL)
    u_s, q_s, k_s, v_s, qm_s = _in_proj(xs, g1, w_proj, _tile(nsamp, 128))
    q4 = q_s.reshape(nsamp, SWA_KV_HEADS, SWA_HEADS // SWA_KV_HEADS, SWA_HEAD_DIM)
    zq = jnp.zeros_like(q4[:, 0])
    qx = jnp.concatenate([jnp.concatenate([q4[:, 0], zq], -1), jnp.concatenate([zq, q4[:, 1]], -1)], 1)
    ck = cache_swa_k.reshape(nsamp, WINDOW, W_KV)
    cv = cache_swa_v.reshape(nsamp, WINDOW, W_KV)
    cmk = cache_mem_k.reshape(nsamp, MEM_LEN, W_QM)
    cmv = cache_mem_v.reshape(nsamp, MEM_LEN, W_QM)
    cconv = cache_conv.reshape(nsamp, CONV_WIDTH - 1, CONV_CH)
    conv_s, swa_x, mem_s = _branches_sample(sinks.reshape(SWA_HEADS, 1), u_s, cconv, qx, k_s, v_s,
                                            ck, cv, qm_s, cmk, cmv, bias_s, bias_0,
                                            dww, dwb, lng, lnb, _tile(nsamp, 8))
    sx = swa_x.reshape(nsamp, SWA_KV_HEADS, SWA_HEADS // SWA_KV_HEADS, SWA_KV_HEADS, SWA_HEAD_DIM)
    swa_s = jnp.stack([sx[:, g, :, g] for g in range(SWA_KV_HEADS)], 1).reshape(nsamp, W_Q).astype(BF16)
    x2_s = _merge(xs, g1, conv_s, swa_s, mem_s, w_gate, wco, wso, wmo, wo, _tile(nsamp, 128))
    y_s = _peer(x2_s, g2, gf, wqt, keys, wd, wut, _tile(nsamp, 128), 8)

    hist = CONV_WIDTH - 1
    kv_shape = (SWA_KV_HEADS, SWA_HEAD_DIM)
    mkv5 = mkv.reshape(batch, MEM_LEN, 2, MEM_HEADS, MEM_HEAD_DIM)
    conv_state_p = u_p.reshape(batch, seq, CONV_CH)[:, -hist:]
    swa_k_p = k_p.reshape(batch, seq, *kv_shape)[:, -WINDOW:]
    swa_v_p = v_p.reshape(batch, seq, *kv_shape)[:, -WINDOW:]
    conv_state_s = jnp.concatenate([cache_conv[0][:, 1:], u_s[:, None, :]], axis=1)
    swa_k_s = jnp.concatenate([cache_swa_k[0][:, 1:], k_s.reshape(nsamp, 1, *kv_shape)], axis=1)
    swa_v_s = jnp.concatenate([cache_swa_v[0][:, 1:], v_s.reshape(nsamp, 1, *kv_shape)], axis=1)
    return (y_p.reshape(batch, seq, D_MODEL), y_s.reshape(nsamp, 1, D_MODEL),
            conv_state_p[None], swa_k_p[None], swa_v_p[None],
            mkv5[:, :, 0][None], mkv5[:, :, 1][None],
            conv_state_s[None], swa_k_s[None], swa_v_s[None])
```

```python
import functools

import jax
import jax.numpy as jnp
from jax import lax
from jax.experimental import pallas as pl
from jax.experimental.pallas import tpu as pltpu

F32 = jnp.float32
BF16 = jnp.bfloat16

D_MODEL = 1024
PAST_LEN = 16384
MEM_LEN = 256
CONV_CH = 512
CONV_WIDTH = 31
SWA_HEADS = 8
SWA_KV_HEADS = 2
SWA_HEAD_DIM = 64
WINDOW = 128
SWA_SCALE = SWA_HEAD_DIM ** -0.5
MEM_HEADS = 4
MEM_HEAD_DIM = 128
MEM_SCALE = MEM_HEAD_DIM ** -0.5
REL_BUCKETS = 32
REL_MAX_DIST = 128
PEER_HEADS = 8
PEER_N_KEYS = 128
PEER_DK_HALF = 64
PEER_TOPK = 16
EPS = 1e-6
NEG_INF = -1e30

W_GLU = 2 * CONV_CH
W_Q = SWA_HEADS * SWA_HEAD_DIM
W_KV = SWA_KV_HEADS * SWA_HEAD_DIM
W_QM = MEM_HEADS * MEM_HEAD_DIM
W_PROJ = W_GLU + W_Q + 2 * W_KV + W_QM

VMEM_LIMIT_BYTES = 56 * 1024 * 1024
LANES = 128
SUBLANES = 8
CONV_HALO = 32


def _params(*sem, flags=None):
    return pltpu.CompilerParams(dimension_semantics=sem, vmem_limit_bytes=VMEM_LIMIT_BYTES, flags=flags)


def _rms(x, g):
    return x * lax.rsqrt(jnp.mean(x * x, axis=-1, keepdims=True) + EPS) * g


def _dot(a, b):
    return jnp.dot(a, b, preferred_element_type=F32)


def _dot_nt(a, b):
    return lax.dot_general(a, b, (((1,), (1,)), ((), ())), preferred_element_type=F32)


def _const_spec(shape):
    zeros = (0,) * len(shape)
    return pl.BlockSpec(shape, lambda *_: zeros)


def _norm_matmul_kernel(x_ref, g_ref, w_ref, o_ref):
    o_ref[...] = _dot(_rms(x_ref[...], g_ref[...]).astype(BF16), w_ref[...])


def _norm_matmul(x, g, w, tm):
    t, n = x.shape[0], w.shape[1]
    return pl.pallas_call(
        _norm_matmul_kernel,
        grid=(t // tm,),
        in_specs=[pl.BlockSpec((tm, D_MODEL), lambda i: (i, 0)), _const_spec((1, D_MODEL)),
                  _const_spec((D_MODEL, n))],
        out_specs=pl.BlockSpec((tm, n), lambda i: (i, 0)),
        out_shape=jax.ShapeDtypeStruct((t, n), F32),
        compiler_params=_params("parallel"),
        name="memkv",
    )(x, g, w)


def _in_proj_kernel(x_ref, g_ref, w_ref, u_ref, q_ref, k_ref, v_ref, qm_ref):
    z = _dot(_rms(x_ref[...], g_ref[...]).astype(BF16), w_ref[...])
    a, b = z[:, :CONV_CH], z[:, CONV_CH:W_GLU]
    u_ref[...] = a * jax.nn.sigmoid(b)
    c = W_GLU
    q_ref[...] = (z[:, c:c + W_Q] * SWA_SCALE).astype(BF16)
    c += W_Q
    k_ref[...] = z[:, c:c + W_KV]
    c += W_KV
    v_ref[...] = z[:, c:c + W_KV]
    c += W_KV
    qm_ref[...] = z[:, c:c + W_QM].astype(BF16)


def _in_proj(x, g, w, tm):
    t = x.shape[0]
    row = lambda n: pl.BlockSpec((tm, n), lambda i: (i, 0))
    return pl.pallas_call(
        _in_proj_kernel,
        grid=(t // tm,),
        in_specs=[row(D_MODEL), _const_spec((1, D_MODEL)), _const_spec((D_MODEL, W_PROJ))],
        out_specs=[row(CONV_CH), row(W_Q), row(W_KV), row(W_KV), row(W_QM)],
        out_shape=[jax.ShapeDtypeStruct((t, CONV_CH), F32), jax.ShapeDtypeStruct((t, W_Q), BF16),
                   jax.ShapeDtypeStruct((t, W_KV), F32), jax.ShapeDtypeStruct((t, W_KV), F32),
                   jax.ShapeDtypeStruct((t, W_QM), BF16)],
        compiler_params=_params("parallel"),
        name="in_proj",
    )(x, g, w)


def _ln_silu(y, g, b):
    mu = jnp.mean(y, axis=-1, keepdims=True)
    var = jnp.mean(jnp.square(y - mu), axis=-1, keepdims=True)
    y = (y - mu) * lax.rsqrt(var + EPS) * g + b
    return y * jax.nn.sigmoid(y)


def _softmax_rows(s):
    e = jnp.exp(s - jnp.max(s, axis=-1, keepdims=True))
    return e / jnp.sum(e, axis=-1, keepdims=True)


def _branches_prompt_kernel(sinks_ref, u_ref, uh_ref, q_ref, k_ref, kp_ref, v_ref, vp_ref, qm_ref,
                            mk_ref, mv_ref, bias_ref, dww_ref, dwb_ref, lng_ref, lnb_ref,
                            conv_ref, swa_ref, mem_ref, ubuf, ushift, kbuf, vbuf, *, tq):
    i = pl.program_id(1)
    first = i == 0

    ubuf[0:CONV_HALO, :] = jnp.where(first, 0.0, uh_ref[...])
    ubuf[CONV_HALO:CONV_HALO + tq, :] = u_ref[...]
    nshift = ushift.shape[1]
    for b in range(1, SUBLANES):
        ushift[b] = ubuf[b:b + nshift, :]
    rb = 64
    off = CONV_HALO - (CONV_WIDTH - 1)
    for r in range(tq // rb):
        acc = jnp.broadcast_to(dwb_ref[...], (rb, CONV_CH))
        for j in range(CONV_WIDTH):
            a, b = divmod(off + j, SUBLANES)
            start = r * rb + a * SUBLANES
            rows = ubuf[start:start + rb, :] if b == 0 else ushift[b, start:start + rb, :]
            acc = acc + rows * dww_ref[j:j + 1, :]
        conv_ref[r * rb:(r + 1) * rb, :] = _ln_silu(acc, lng_ref[...], lnb_ref[...]).astype(BF16)

    lane = lax.broadcasted_iota(jnp.int32, (WINDOW + tq, LANES), 1)
    lo = lane < SWA_HEAD_DIM
    for src_ref, prev_ref, buf in ((k_ref, kp_ref, kbuf), (v_ref, vp_ref, vbuf)):
        full = jnp.concatenate([jnp.where(first, 0.0, prev_ref[...]), src_ref[...]], axis=0)
        rolled = pltpu.roll(full, SWA_HEAD_DIM, 1)
        buf[0] = jnp.where(lo, full, 0.0).astype(BF16)
        buf[1] = jnp.where(lo, 0.0, rolled).astype(BF16)
        buf[2] = jnp.where(lo, rolled, 0.0).astype(BF16)
        buf[3] = jnp.where(lo, 0.0, full).astype(BF16)

    qi = lax.broadcasted_iota(jnp.int32, (WINDOW, 2 * WINDOW), 0)
    ki = lax.broadcasted_iota(jnp.int32, (WINDOW, 2 * WINDOW), 1)
    dist = WINDOW + qi - ki
    band = (dist >= 0) & (dist <= WINDOW)
    nqb = tq // WINDOW
    for jb in range(nqb):
        r0 = jb * WINDOW
        kmin = jnp.where(i * nqb + jb > 0, 0, WINDOW)
        mask = band & (ki >= kmin)
        for p in range(SWA_HEADS // 2):
            g = p // 2
            qp = q_ref[r0:r0 + WINDOW, LANES * p:LANES * (p + 1)]
            o = None
            for half in range(2):
                h = 2 * p + half
                s = _dot_nt(qp, kbuf[2 * g + half, r0:r0 + 2 * WINDOW, :]) + bias_ref[h]
                s = jnp.where(mask, s, NEG_INF)
                sink = sinks_ref[h]
                m = jnp.maximum(jnp.max(s, axis=-1, keepdims=True), sink)
                pr = jnp.exp(s - m)
                pr = pr / (jnp.sum(pr, axis=-1, keepdims=True) + jnp.exp(sink - m))
                t = _dot(pr.astype(BF16), vbuf[2 * g + half, r0:r0 + 2 * WINDOW, :])
                o = t if o is None else o + t
            swa_ref[r0:r0 + WINDOW, LANES * p:LANES * (p + 1)] = o.astype(BF16)

    for hm in range(MEM_HEADS):
        sl = slice(hm * MEM_HEAD_DIM, (hm + 1) * MEM_HEAD_DIM)
        w = _softmax_rows(_dot_nt(qm_ref[:, sl], mk_ref[:, sl].astype(BF16)) * MEM_SCALE)
        mem_ref[:, sl] = _dot(w.astype(BF16), mv_ref[:, sl].astype(BF16)).astype(BF16)


def _branches_prompt(sinks, u, q, k, v, qm, mkv, bias, dww, dwb, lng, lnb, batch, seq, tq):
    t = batch * seq
    nq = seq // tq
    row = lambda n: pl.BlockSpec((tq, n), lambda b, i: (b * nq + i, 0))
    halo = lambda rows, n: pl.BlockSpec(
        (rows, n), lambda b, i: (jnp.maximum(b * (seq // rows) + i * (tq // rows) - 1, 0), 0))
    return pl.pallas_call(
        functools.partial(_branches_prompt_kernel, tq=tq),
        grid=(batch, nq),
        in_specs=[pl.BlockSpec(memory_space=pltpu.SMEM),
                  row(CONV_CH), halo(CONV_HALO, CONV_CH), row(W_Q),
                  row(W_KV), halo(WINDOW, W_KV), row(W_KV), halo(WINDOW, W_KV), row(W_QM),
                  pl.BlockSpec((MEM_LEN, W_QM), lambda b, i: (b, 0)),
                  pl.BlockSpec((MEM_LEN, W_QM), lambda b, i: (b, 1)),
                  _const_spec((SWA_HEADS, WINDOW, 2 * WINDOW)),
                  _const_spec((CONV_WIDTH, CONV_CH)), _const_spec((1, CONV_CH)),
                  _const_spec((1, CONV_CH)), _const_spec((1, CONV_CH))],
        out_specs=[row(CONV_CH), row(W_Q), row(W_QM)],
        out_shape=[jax.ShapeDtypeStruct((t, CONV_CH), BF16), jax.ShapeDtypeStruct((t, W_Q), BF16),
                   jax.ShapeDtypeStruct((t, W_QM), BF16)],
        scratch_shapes=[pltpu.VMEM((CONV_HALO + tq, CONV_CH), F32),
                        pltpu.VMEM((SUBLANES, CONV_HALO + tq - SUBLANES, CONV_CH), F32),
                        pltpu.VMEM((4, WINDOW + tq, LANES), BF16),
                        pltpu.VMEM((4, WINDOW + tq, LANES), BF16)],
        compiler_params=_params("parallel", "arbitrary"),
        name="branches_prompt",
    )(sinks, u, u, q, k, k, v, v, qm, mkv, mkv, bias, dww, dwb, lng, lnb)


def _branches_sample_kernel(sinks_ref, u_ref, cc_ref, qx_ref, kn_ref, vn_ref, ck_ref, cv_ref, qm_ref,
                            cmk_ref, cmv_ref, bias_ref, bias0_ref, dww_ref, dwb_ref, lng_ref, lnb_ref,
                            conv_ref, swa_ref, mem_ref, *, nb):
    hist = CONV_WIDTH - 1
    y = jnp.sum(cc_ref[...] * dww_ref[0:hist, :][None], axis=1)
    y = y + u_ref[...] * dww_ref[hist:hist + 1, :] + dwb_ref[...]
    conv_ref[...] = _ln_silu(y, lng_ref[...], lnb_ref[...]).astype(BF16)

    row = lax.broadcasted_iota(jnp.int32, (SWA_HEADS, W_QM), 0)
    col = lax.broadcasted_iota(jnp.int32, (SWA_HEADS, W_QM), 1)
    own = (col // MEM_HEAD_DIM) == row
    sink = sinks_ref[...]
    for n in range(nb):
        qx = qx_ref[n]
        s = _dot_nt(qx, ck_ref[n].astype(BF16)) + bias_ref[...]
        kn = kn_ref[n:n + 1, :].astype(BF16).astype(F32)
        s_new = jnp.sum(qx.astype(F32) * kn, axis=-1, keepdims=True) + bias0_ref[...]
        m = jnp.maximum(jnp.maximum(jnp.max(s, axis=-1, keepdims=True), s_new), sink)
        pr, pr_new = jnp.exp(s - m), jnp.exp(s_new - m)
        den = jnp.sum(pr, axis=-1, keepdims=True) + pr_new + jnp.exp(sink - m)
        vn = vn_ref[n:n + 1, :].astype(BF16).astype(F32)
        o = _dot((pr / den).astype(BF16), cv_ref[n].astype(BF16))
        swa_ref[n] = o + (pr_new / den).astype(BF16).astype(F32) * vn

        qmx = jnp.where(own, qm_ref[n:n + 1, :].astype(F32), 0.0).astype(BF16)
        w = _softmax_rows(_dot_nt(qmx, cmk_ref[n].astype(BF16)) * MEM_SCALE)
        om = _dot(w.astype(BF16), cmv_ref[n].astype(BF16))
        mem_ref[n:n + 1, :] = jnp.sum(jnp.where(own, om, 0.0), axis=0, keepdims=True).astype(BF16)


def _branches_sample(sinks, u, cache_conv, qx, kn, vn, ck, cv, qm, cmk, cmv, bias, bias0,
                     dww, dwb, lng, lnb, nb):
    n = u.shape[0]
    row = lambda c: pl.BlockSpec((nb, c), lambda i: (i, 0))
    blk3 = lambda a, c: pl.BlockSpec((nb, a, c), lambda i: (i, 0, 0))
    return pl.pallas_call(
        functools.partial(_branches_sample_kernel, nb=nb),
        grid=(n // nb,),
        in_specs=[_const_spec((SWA_HEADS, 1)),
                  row(CONV_CH), blk3(CONV_WIDTH - 1, CONV_CH), blk3(SWA_HEADS, LANES),
                  row(W_KV), row(W_KV), blk3(WINDOW, W_KV), blk3(WINDOW, W_KV), row(W_QM),
                  blk3(MEM_LEN, W_QM), blk3(MEM_LEN, W_QM),
                  _const_spec((SWA_HEADS, WINDOW)), _const_spec((SWA_HEADS, 1)),
                  _const_spec((CONV_WIDTH, CONV_CH)), _const_spec((1, CONV_CH)),
                  _const_spec((1, CONV_CH)), _const_spec((1, CONV_CH))],
        out_specs=[row(CONV_CH), blk3(SWA_HEADS, LANES), row(W_QM)],
        out_shape=[jax.ShapeDtypeStruct((n, CONV_CH), BF16),
                   jax.ShapeDtypeStruct((n, SWA_HEADS, LANES), F32),
                   jax.ShapeDtypeStruct((n, W_QM), BF16)],
        compiler_params=_params("parallel"),
        name="branches_sample",
    )(sinks, u, cache_conv, qx, kn, vn, ck, cv, qm, cmk, cmv, bias, bias0, dww, dwb, lng, lnb)


def _merge_kernel(x_ref, g_ref, conv_ref, swa_ref, mem_ref, wg_ref, wco_ref, wso_ref, wmo_ref, wo_ref,
                  o_ref):
    x = x_ref[...]
    h = _rms(x, g_ref[...]).astype(BF16)
    merged = None
    for br, (a_ref, w_ref) in enumerate(((conv_ref, wco_ref), (swa_ref, wso_ref), (mem_ref, wmo_ref))):
        gate = jax.nn.sigmoid(_dot(h, wg_ref[:, br * D_MODEL:(br + 1) * D_MODEL]))
        term = gate * _dot(a_ref[...], w_ref[...])
        merged = term if merged is None else merged + term
    o_ref[...] = x + _dot(merged.astype(BF16), wo_ref[...])


def _merge(x, g, conv, swa, mem, wg, wco, wso, wmo, wo, tm):
    t = x.shape[0]
    row = lambda n: pl.BlockSpec((tm, n), lambda i: (i, 0))
    return pl.pallas_call(
        _merge_kernel,
        grid=(t // tm,),
        in_specs=[row(D_MODEL), _const_spec((1, D_MODEL)), row(CONV_CH), row(W_Q), row(W_QM),
                  _const_spec((D_MODEL, 3 * D_MODEL)), _const_spec((CONV_CH, D_MODEL)),
                  _const_spec((W_Q, D_MODEL)), _const_spec((W_QM, D_MODEL)),
                  _const_spec((D_MODEL, D_MODEL))],
        out_specs=row(D_MODEL),
        out_shape=jax.ShapeDtypeStruct((t, D_MODEL), F32),
        compiler_params=_params("parallel"),
        name="merge",
    )(x, g, conv, swa, mem, wg, wco, wso, wmo, wo)


def _gelu(x):
    return 0.5 * x * (1.0 + lax.erf(x * (2.0 ** -0.5)))


def _top_values(arrs, count, with_rank=False):
    out = []
    ranks = [jnp.full(a.shape, float(count), F32) for a in arrs] if with_rank else None
    for it in range(count):
        m = jnp.max(functools.reduce(jnp.maximum, arrs), axis=0, keepdims=True)
        out.append(m)
        hit = [a == m for a in arrs]
        if with_rank:
            ranks = [jnp.where(hh, float(it), rk) for hh, rk in zip(hit, ranks)]
        arrs = [jnp.where(hh, -jnp.inf, a) for hh, a in zip(hit, arrs)]
    return (out, ranks) if with_rank else out


def _oddeven_merge_sort(lo, hi):
    def merge(lo, hi, r):
        step = 2 * r
        if step < hi - lo:
            yield from merge(lo, hi, step)
            yield from merge(lo + r, hi, step)
            yield from ((i, i + r) for i in range(lo + r, hi - r, step))
        else:
            yield (lo, lo + r)

    if hi > lo:
        mid = lo + (hi - lo) // 2
        yield from _oddeven_merge_sort(lo, mid)
        yield from _oddeven_merge_sort(mid + 1, hi)
        yield from merge(lo, hi, 1)


def _compare_exchange(a, i, j):
    a[i], a[j] = jnp.maximum(a[i], a[j]), jnp.minimum(a[i], a[j])


def _top16_sorted(blocks):
    n = PEER_TOPK
    assert len(blocks) == n and blocks[0].shape[0] == SUBLANES
    a = list(blocks)
    for i, j in _oddeven_merge_sort(0, n - 1):
        _compare_exchange(a, i, j)
    for shift in (4, 2, 1):
        b = [pltpu.roll(x, shift, 0) for x in a]
        a = [jnp.maximum(a[i], b[n - 1 - i]) for i in range(n)]
        d = n // 2
        while d:
            for i in range(n):
                if not i & d:
                    _compare_exchange(a, i, i + d)
            d //= 2
    return a


def _peer_route(h, qt, keys_ref, l_s, e0_s, r1_s, e1_s, sv0_s, sv1_s):
    nk, tb = PEER_N_KEYS, qt.shape[1]
    r = 2 * h * PEER_DK_HALF
    s0_all = _dot(keys_ref[h, 0], qt[r:r + PEER_DK_HALF, :])
    s1_all = _dot(keys_ref[h, 1], qt[r + PEER_DK_HALF:r + 2 * PEER_DK_HALF, :])
    for lg in range(tb // LANES):
        lanes = slice(lg * LANES, (lg + 1) * LANES)
        blocks = lambda a: [a[SUBLANES * j:SUBLANES * (j + 1), lanes] for j in range(nk // SUBLANES)]
        s0, s1 = blocks(s0_all), blocks(s1_all)
        top0, top1 = _top16_sorted(s0), _top16_sorted(s1)
        for j in range(PEER_TOPK):
            sv0_s[j:j + 1, lanes] = top0[j][0:1]
            sv1_s[j:j + 1, lanes] = top1[j][0:1]
        a0, a1 = sv0_s[0:8, lanes], sv0_s[8:16, lanes]
        b0, b1 = sv1_s[0:8, lanes], sv1_s[8:16, lanes]
        cands = [a0[0:1] + b0, a0[0:1] + b1] + [a0[a:a + 1] + b0 for a in range(1, 8)] + [a1 + b0[0:1]]
        best = _top_values(cands, PEER_TOPK)
        sel = [cd >= best[PEER_TOPK - 1] for cd in cands]
        z = functools.reduce(jnp.add, [jnp.where(sl, jnp.exp(cd - best[0]), 0.0) for sl, cd in zip(sel, cands)])
        z = jnp.sum(z, axis=0, keepdims=True)
        cnt = [jnp.sum(jnp.where(sl, 1.0, 0.0), axis=0, keepdims=True) for sl in sel[:9]]
        counts = [cnt[0] + cnt[1]] + cnt[2:9]
        tail = jnp.where(sel[9], 1.0, 0.0)
        counts = [jnp.broadcast_to(counts[a] if a < 8 else tail[a - 8:a - 7], (SUBLANES, LANES))
                  for a in range(PEER_TOPK)]
        lrow, rank1 = [], []
        for x0, x1 in zip(s0, s1):
            lx = jnp.zeros_like(x0)
            rx = jnp.full_like(x1, float(PEER_TOPK))
            for a in reversed(range(PEER_TOPK)):
                lx = jnp.where(x0 >= top0[a], counts[a], lx)
                rx = jnp.where(x1 >= top1[a], float(a), rx)
            lrow.append(lx)
            rank1.append(rx)
        l_s[h, :, lanes] = jnp.concatenate(lrow, axis=0)
        e0_s[h, :, lanes] = jnp.exp(jnp.concatenate(s0, axis=0) - top0[0][0:1]) / z
        r1_s[h, :, lanes] = jnp.concatenate(rank1, axis=0).astype(BF16)
        e1_s[h, :, lanes] = jnp.exp(jnp.concatenate(s1, axis=0) - top1[0][0:1]).astype(BF16)


def _peer_gate_chunk(ci, at_ref, ct_ref, lrow_s, erow_s, r1_s, e1_s, cb):
    nk = PEER_N_KEYS
    for j in range(cb):
        g = None
        for h in range(PEER_HEADS):
            lrow = lrow_s[ci, h, j:j + 1, :].astype(BF16)
            erow = erow_s[ci, h, j:j + 1, :].astype(BF16)
            term = jnp.where(r1_s[h] < lrow, e1_s[h], jnp.zeros((), BF16)) * erow
            g = term if g is None else g + term
        rows = slice((ci * cb + j) * nk, (ci * cb + j + 1) * nk)
        ct_ref[rows, :] = _gelu(at_ref[j * nk:(j + 1) * nk, :]).astype(BF16) * g


def _peer_kernel(x_ref, g2_ref, gf_ref, wqt_ref, keys_ref, wda_ref, wdb_ref, wut_ref, o_ref,
                 h_s, l_s, e0_s, r1_s, e1_s, sv0_s, sv1_s, lrow_s, erow_s, at0_s, at1_s, ct_s, acc_s, *, cb):
    s = pl.program_id(1)

    @pl.when(s == 0)
    def _first():
        h_s[...] = _rms(x_ref[...], g2_ref[...]).T.astype(BF16)
        qt = _dot(wqt_ref[...], h_s[...]).astype(BF16)
        for h in range(PEER_HEADS):
            _peer_route(h, qt, keys_ref, l_s, e0_s, r1_s, e1_s, sv0_s, sv1_s)
        at0_s[...] = _dot(wda_ref[...], h_s[...])
        acc_s[...] = jnp.zeros_like(acc_s)

    @pl.when(s > 0)
    def _steady():
        for ci in range(2):
            base = pl.multiple_of((2 * s - 2 + ci) * cb, cb)
            for h in range(PEER_HEADS):
                lrow_s[ci, h] = l_s[h, pl.ds(base, cb), :]
                erow_s[ci, h] = e0_s[h, pl.ds(base, cb), :]
        gate = functools.partial(_peer_gate_chunk, ct_ref=ct_s, lrow_s=lrow_s, erow_s=erow_s,
                                 r1_s=r1_s, e1_s=e1_s, cb=cb)
        gate(0, at0_s)
        at1_s[...] = _dot(wda_ref[...], h_s[...])
        at0_s[...] = _dot(wdb_ref[...], h_s[...])
        gate(1, at1_s)
        acc_s[...] += _dot(wut_ref[...], ct_s[...])

    @pl.when(s == pl.num_programs(1) - 1)
    def _last():
        y = x_ref[...] + acc_s[...].T
        o_ref[...] = _rms(y, gf_ref[...])


def _peer(x, g2, gf, wqt, keys, wd, wut, tb, cb):
    t = x.shape[0]
    ec = cb * PEER_N_KEYS
    nch = wd.shape[0] // ec
    assert nch % 2 == 0 and tb % LANES == 0
    stat = lambda dt: pltpu.VMEM((PEER_HEADS, PEER_N_KEYS, tb), dt)
    return pl.pallas_call(
        functools.partial(_peer_kernel, cb=cb),
        grid=(t // tb, nch // 2 + 1),
        in_specs=[pl.BlockSpec((tb, D_MODEL), lambda i, s: (i, 0)),
                  _const_spec((1, D_MODEL)), _const_spec((1, D_MODEL)),
                  _const_spec((D_MODEL, D_MODEL)),
                  _const_spec((PEER_HEADS, 2, PEER_N_KEYS, PEER_DK_HALF)),
                  pl.BlockSpec((ec, D_MODEL), lambda i, s: (jnp.maximum(2 * s - 1, 0), 0)),
                  pl.BlockSpec((ec, D_MODEL), lambda i, s: (jnp.minimum(2 * s, nch - 1), 0)),
                  pl.BlockSpec((D_MODEL, 2 * ec), lambda i, s: (0, jnp.maximum(s - 1, 0)))],
        out_specs=pl.BlockSpec((tb, D_MODEL), lambda i, s: (i, 0)),
        out_shape=jax.ShapeDtypeStruct((t, D_MODEL), F32),
        scratch_shapes=[pltpu.VMEM((D_MODEL, tb), BF16), stat(F32), stat(F32), stat(BF16), stat(BF16),
                        pltpu.VMEM((PEER_TOPK, tb), F32), pltpu.VMEM((PEER_TOPK, tb), F32),
                        pltpu.VMEM((2, PEER_HEADS, cb, tb), F32), pltpu.VMEM((2, PEER_HEADS, cb, tb), F32),
                        pltpu.VMEM((ec, tb), F32), pltpu.VMEM((ec, tb), F32),
                        pltpu.VMEM((2 * ec, tb), BF16), pltpu.VMEM((D_MODEL, tb), F32)],
        compiler_params=_params("parallel", "arbitrary"),
        name="peer",
    )(x, g2, gf, wqt, keys, wd, wd, wut)


def _rel_bucket(dist):
    n = jnp.maximum(dist, 0)
    max_exact = REL_BUCKETS // 2
    nf = jnp.maximum(n, 1).astype(F32)
    large = max_exact + (jnp.log(nf / max_exact) / jnp.log(REL_MAX_DIST / max_exact)
                         * (REL_BUCKETS - max_exact)).astype(jnp.int32)
    return jnp.where(n < max_exact, n, jnp.minimum(large, REL_BUCKETS - 1))


def _tile(t, cap):
    tm = min(t, cap)
    assert t % tm == 0, (t, tm)
    return tm


def kernel(x_prompt, x_sample, cache_conv, cache_swa_k, cache_swa_v, cache_mem_k, cache_mem_v, mem_prompt, rel_bias_table, norm1_g, w_in, conv_dw_w, conv_dw_b, conv_ln_g, conv_ln_b, w_conv_out, swa_sinks, w_swa_out, mem_norm_g, w_mem_kv, w_mem_out, w_out, norm2_g, peer_w_q, peer_keys, peer_w_down, peer_w_up, final_norm_g):
    assert w_in.shape[0] == 1, "single layer"
    batch, seq, _ = x_prompt.shape
    nsamp = x_sample.shape[0]
    assert x_sample.shape[1] == 1 and seq % WINDOW == 0
    row = lambda a: a.reshape(1, -1)

    w_proj = w_in[0, :, :W_PROJ].astype(BF16)
    w_gate = w_in[0, :, W_PROJ:].astype(BF16)
    g1, g2, gf = row(norm1_g[0]), row(norm2_g[0]), row(final_norm_g)
    dww, dwb = conv_dw_w[0], row(conv_dw_b[0])
    lng, lnb = row(conv_ln_g[0]), row(conv_ln_b[0])
    wco, wso = w_conv_out[0].astype(BF16), w_swa_out[0].astype(BF16)
    wmo, wo = w_mem_out[0].astype(BF16), w_out[0].astype(BF16)
    wqt = peer_w_q[0].T.astype(BF16)
    keys = peer_keys[0].astype(BF16)
    wd = peer_w_down.reshape(-1, D_MODEL).astype(BF16)
    wut = peer_w_up.reshape(-1, D_MODEL).astype(BF16).T
    sinks = swa_sinks[0]

    qi = jnp.arange(WINDOW)[:, None]
    ki = jnp.arange(2 * WINDOW)[None, :]
    def table_rows(dist):
        onehot = (_rel_bucket(dist)[..., None] == jnp.arange(REL_BUCKETS)).astype(F32)
        return jnp.einsum("...b,bh->h...", onehot, rel_bias_table.astype(F32),
                          precision=lax.Precision.HIGHEST)

    bias_p = table_rows(WINDOW + qi - ki)
    bias_s = table_rows(WINDOW - jnp.arange(WINDOW))
    bias_0 = table_rows(jnp.zeros((1,), jnp.int32))

    xp = x_prompt.reshape(batch * seq, D_MODEL)
    mkv = _norm_matmul(mem_prompt.reshape(batch * MEM_LEN, D_MODEL), row(mem_norm_g[0]),
                       w_mem_kv[0].astype(BF16), MEM_LEN)
    u_p, q_p, k_p, v_p, qm_p = _in_proj(xp, g1, w_proj, _tile(batch * seq, 512))
    tq = _tile(seq, 512)
    conv_p, swa_p, mem_p = _branches_prompt(sinks, u_p, q_p, k_p, v_p, qm_p, mkv, bias_p,
                                            dww, dwb, lng, lnb, batch, seq, tq)
    x2_p = _merge(xp, g1, conv_p, swa_p, mem_p, w_gate, wco, wso, wmo, wo, _tile(batch * seq, 512))
    y_p = _peer(x2_p, g2, gf, wqt, keys, wd, wut, _tile(batch * seq, 512), 8)

    xs = x_sample.reshape(nsamp, D_MODEL)
    u_s, q_s, k_s, v_s, qm_s = _in_proj(xs, g1, w_proj, _tile(nsamp, 128))
    q4 = q_s.reshape(nsamp, SWA_KV_HEADS, SWA_HEADS // SWA_KV_HEADS, SWA_HEAD_DIM)
    zq = jnp.zeros_like(q4[:, 0])
    qx = jnp.concatenate([jnp.concatenate([q4[:, 0], zq], -1), jnp.concatenate([zq, q4[:, 1]], -1)], 1)
    ck = cache_swa_k.reshape(nsamp, WINDOW, W_KV)
    cv = cache_swa_v.reshape(nsamp, WINDOW, W_KV)
    cmk = cache_mem_k.reshape(nsamp, MEM_LEN, W_QM)
    cmv = cache_mem_v.reshape(nsamp, MEM_LEN, W_QM)
    cconv = cache_conv.reshape(nsamp, CONV_WIDTH - 1, CONV_CH)
    conv_s, swa_x, mem_s = _branches_sample(sinks.reshape(SWA_HEADS, 1), u_s, cconv, qx, k_s, v_s,
                                            ck, cv, qm_s, cmk, cmv, bias_s, bias_0,
                                            dww, dwb, lng, lnb, _tile(nsamp, 8))
    sx = swa_x.reshape(nsamp, SWA_KV_HEADS, SWA_HEADS // SWA_KV_HEADS, SWA_KV_HEADS, SWA_HEAD_DIM)
    swa_s = jnp.stack([sx[:, g, :, g] for g in range(SWA_KV_HEADS)], 1).reshape(nsamp, W_Q).astype(BF16)
    x2_s = _merge(xs, g1, conv_s, swa_s, mem_s, w_gate, wco, wso, wmo, wo, _tile(nsamp, 128))
    y_s = _peer(x2_s, g2, gf, wqt, keys, wd, wut, _tile(nsamp, 128), 8)

    hist = CONV_WIDTH - 1
    kv_shape = (SWA_KV_HEADS, SWA_HEAD_DIM)
    mkv5 = mkv.reshape(batch, MEM_LEN, 2, MEM_HEADS, MEM_HEAD_DIM)
    conv_state_p = u_p.reshape(batch, seq, CONV_CH)[:, -hist:]
    swa_k_p = k_p.reshape(batch, seq, *kv_shape)[:, -WINDOW:]
    swa_v_p = v_p.reshape(batch, seq, *kv_shape)[:, -WINDOW:]
    conv_state_s = jnp.concatenate([cache_conv[0][:, 1:], u_s[:, None, :]], axis=1)
    swa_k_s = jnp.concatenate([cache_swa_k[0][:, 1:], k_s.reshape(nsamp, 1, *kv_shape)], axis=1)
    swa_v_s = jnp.concatenate([cache_swa_v[0][:, 1:], v_s.reshape(nsamp, 1, *kv_shape)], axis=1)
    return (y_p.reshape(batch, seq, D_MODEL), y_s.reshape(nsamp, 1, D_MODEL),
            conv_state_p[None], swa_k_p[None], swa_v_p[None],
            mkv5[:, :, 0][None], mkv5[:, :, 1][None],
            conv_state_s[None], swa_k_s[None], swa_v_s[None])
```

```python
import functools

import jax
import jax.numpy as jnp
from jax import lax
from jax.experimental import pallas as pl
from jax.experimental.pallas import tpu as pltpu

F32 = jnp.float32
BF16 = jnp.bfloat16

D_MODEL = 1024
PAST_LEN = 16384
MEM_LEN = 256
CONV_CH = 512
CONV_WIDTH = 31
SWA_HEADS = 8
SWA_KV_HEADS = 2
SWA_HEAD_DIM = 64
WINDOW = 128
SWA_SCALE = SWA_HEAD_DIM ** -0.5
MEM_HEADS = 4
MEM_HEAD_DIM = 128
MEM_SCALE = MEM_HEAD_DIM ** -0.5
REL_BUCKETS = 32
REL_MAX_DIST = 128
PEER_HEADS = 8
PEER_N_KEYS = 128
PEER_DK_HALF = 64
PEER_TOPK = 16
EPS = 1e-6
NEG_INF = -1e30

W_GLU = 2 * CONV_CH
W_Q = SWA_HEADS * SWA_HEAD_DIM
W_KV = SWA_KV_HEADS * SWA_HEAD_DIM
W_QM = MEM_HEADS * MEM_HEAD_DIM
W_PROJ = W_GLU + W_Q + 2 * W_KV + W_QM

VMEM_LIMIT_BYTES = 56 * 1024 * 1024
LANES = 128
SUBLANES = 8
CONV_HALO = 32


def _params(*sem, flags=None):
    return pltpu.CompilerParams(dimension_semantics=sem, vmem_limit_bytes=VMEM_LIMIT_BYTES, flags=flags)


def _rms(x, g):
    return x * lax.rsqrt(jnp.mean(x * x, axis=-1, keepdims=True) + EPS) * g


def _dot(a, b):
    return jnp.dot(a, b, preferred_element_type=F32)


def _dot_nt(a, b):
    return lax.dot_general(a, b, (((1,), (1,)), ((), ())), preferred_element_type=F32)


def _const_spec(shape):
    zeros = (0,) * len(shape)
    return pl.BlockSpec(shape, lambda *_: zeros)


def _norm_matmul_kernel(x_ref, g_ref, w_ref, o_ref):
    o_ref[...] = _dot(_rms(x_ref[...], g_ref[...]).astype(BF16), w_ref[...])


def _norm_matmul(x, g, w, tm):
    t, n = x.shape[0], w.shape[1]
    return pl.pallas_call(
        _norm_matmul_kernel,
        grid=(t // tm,),
        in_specs=[pl.BlockSpec((tm, D_MODEL), lambda i: (i, 0)), _const_spec((1, D_MODEL)),
                  _const_spec((D_MODEL, n))],
        out_specs=pl.BlockSpec((tm, n), lambda i: (i, 0)),
        out_shape=jax.ShapeDtypeStruct((t, n), F32),
        compiler_params=_params("parallel"),
        name="memkv",
    )(x, g, w)


def _in_proj_kernel(x_ref, g_ref, w_ref, u_ref, q_ref, k_ref, v_ref, qm_ref):
    z = _dot(_rms(x_ref[...], g_ref[...]).astype(BF16), w_ref[...])
    a, b = z[:, :CONV_CH], z[:, CONV_CH:W_GLU]
    u_ref[...] = a * jax.nn.sigmoid(b)
    c = W_GLU
    q_ref[...] = (z[:, c:c + W_Q] * SWA_SCALE).astype(BF16)
    c += W_Q
    k_ref[...] = z[:, c:c + W_KV]
    c += W_KV
    v_ref[...] = z[:, c:c + W_KV]
    c += W_KV
    qm_ref[...] = z[:, c:c + W_QM].astype(BF16)


def _in_proj(x, g, w, tm):
    t = x.shape[0]
    row = lambda n: pl.BlockSpec((tm, n), lambda i: (i, 0))
    return pl.pallas_call(
        _in_proj_kernel,
        grid=(t // tm,),
        in_specs=[row(D_MODEL), _const_spec((1, D_MODEL)), _const_spec((D_MODEL, W_PROJ))],
        out_specs=[row(CONV_CH), row(W_Q), row(W_KV), row(W_KV), row(W_QM)],
        out_shape=[jax.ShapeDtypeStruct((t, CONV_CH), F32), jax.ShapeDtypeStruct((t, W_Q), BF16),
                   jax.ShapeDtypeStruct((t, W_KV), F32), jax.ShapeDtypeStruct((t, W_KV), F32),
                   jax.ShapeDtypeStruct((t, W_QM), BF16)],
        compiler_params=_params("parallel"),
        name="in_proj",
    )(x, g, w)


def _ln_silu(y, g, b):
    mu = jnp.mean(y, axis=-1, keepdims=True)
    var = jnp.mean(jnp.square(y - mu), axis=-1, keepdims=True)
    y = (y - mu) * lax.rsqrt(var + EPS) * g + b
    return y * jax.nn.sigmoid(y)


def _softmax_rows(s):
    e = jnp.exp(s - jnp.max(s, axis=-1, keepdims=True))
    return e / jnp.sum(e, axis=-1, keepdims=True)


def _branches_prompt_kernel(sinks_ref, u_ref, uh_ref, q_ref, k_ref, kp_ref, v_ref, vp_ref, qm_ref,
                            mk_ref, mv_ref, bias_ref, dww_ref, dwb_ref, lng_ref, lnb_ref,
                            conv_ref, swa_ref, mem_ref, ubuf, ushift, kbuf, vbuf, *, tq):
    i = pl.program_id(1)
    first = i == 0

    ubuf[0:CONV_HALO, :] = jnp.where(first, 0.0, uh_ref[...])
    ubuf[CONV_HALO:CONV_HALO + tq, :] = u_ref[...]
    nshift = ushift.shape[1]
    for b in range(1, SUBLANES):
        ushift[b] = ubuf[b:b + nshift, :]
    rb = 64
    off = CONV_HALO - (CONV_WIDTH - 1)
    for r in range(tq // rb):
        acc = jnp.broadcast_to(dwb_ref[...], (rb, CONV_CH))
        for j in range(CONV_WIDTH):
            a, b = divmod(off + j, SUBLANES)
            start = r * rb + a * SUBLANES
            rows = ubuf[start:start + rb, :] if b == 0 else ushift[b, start:start + rb, :]
            acc = acc + rows * dww_ref[j:j + 1, :]
        conv_ref[r * rb:(r + 1) * rb, :] = _ln_silu(acc, lng_ref[...], lnb_ref[...]).astype(BF16)

    lane = lax.broadcasted_iota(jnp.int32, (WINDOW + tq, LANES), 1)
    lo = lane < SWA_HEAD_DIM
    for src_ref, prev_ref, buf in ((k_ref, kp_ref, kbuf), (v_ref, vp_ref, vbuf)):
        full = jnp.concatenate([jnp.where(first, 0.0, prev_ref[...]), src_ref[...]], axis=0)
        rolled = pltpu.roll(full, SWA_HEAD_DIM, 1)
        buf[0] = jnp.where(lo, full, 0.0).astype(BF16)
        buf[1] = jnp.where(lo, 0.0, rolled).astype(BF16)
        buf[2] = jnp.where(lo, rolled, 0.0).astype(BF16)
        buf[3] = jnp.where(lo, 0.0, full).astype(BF16)

    qi = lax.broadcasted_iota(jnp.int32, (WINDOW, 2 * WINDOW), 0)
    ki = lax.broadcasted_iota(jnp.int32, (WINDOW, 2 * WINDOW), 1)
    dist = WINDOW + qi - ki
    band = (dist >= 0) & (dist <= WINDOW)
    nqb = tq // WINDOW
    for jb in range(nqb):
        r0 = jb * WINDOW
        kmin = jnp.where(i * nqb + jb > 0, 0, WINDOW)
        mask = band & (ki >= kmin)
        for p in range(SWA_HEADS // 2):
            g = p // 2
            qp = q_ref[r0:r0 + WINDOW, LANES * p:LANES * (p + 1)]
            o = None
            for half in range(2):
                h = 2 * p + half
                s = _dot_nt(qp, kbuf[2 * g + half, r0:r0 + 2 * WINDOW, :]) + bias_ref[h]
                s = jnp.where(mask, s, NEG_INF)
                sink = sinks_ref[h]
                m = jnp.maximum(jnp.max(s, axis=-1, keepdims=True), sink)
                pr = jnp.exp(s - m)
                pr = pr / (jnp.sum(pr, axis=-1, keepdims=True) + jnp.exp(sink - m))
                t = _dot(pr.astype(BF16), vbuf[2 * g + half, r0:r0 + 2 * WINDOW, :])
                o = t if o is None else o + t
            swa_ref[r0:r0 + WINDOW, LANES * p:LANES * (p + 1)] = o.astype(BF16)

    for hm in range(MEM_HEADS):
        sl = slice(hm * MEM_HEAD_DIM, (hm + 1) * MEM_HEAD_DIM)
        w = _softmax_rows(_dot_nt(qm_ref[:, sl], mk_ref[:, sl].astype(BF16)) * MEM_SCALE)
        mem_ref[:, sl] = _dot(w.astype(BF16), mv_ref[:, sl].astype(BF16)).astype(BF16)


def _branches_prompt(sinks, u, q, k, v, qm, mkv, bias, dww, dwb, lng, lnb, batch, seq, tq):
    t = batch * seq
    nq = seq // tq
    row = lambda n: pl.BlockSpec((tq, n), lambda b, i: (b * nq + i, 0))
    halo = lambda rows, n: pl.BlockSpec(
        (rows, n), lambda b, i: (jnp.maximum(b * (seq // rows) + i * (tq // rows) - 1, 0), 0))
    return pl.pallas_call(
        functools.partial(_branches_prompt_kernel, tq=tq),
        grid=(batch, nq),
        in_specs=[pl.BlockSpec(memory_space=pltpu.SMEM),
                  row(CONV_CH), halo(CONV_HALO, CONV_CH), row(W_Q),
                  row(W_KV), halo(WINDOW, W_KV), row(W_KV), halo(WINDOW, W_KV), row(W_QM),
                  pl.BlockSpec((MEM_LEN, W_QM), lambda b, i: (b, 0)),
                  pl.BlockSpec((MEM_LEN, W_QM), lambda b, i: (b, 1)),
                  _const_spec((SWA_HEADS, WINDOW, 2 * WINDOW)),
                  _const_spec((CONV_WIDTH, CONV_CH)), _const_spec((1, CONV_CH)),
                  _const_spec((1, CONV_CH)), _const_spec((1, CONV_CH))],
        out_specs=[row(CONV_CH), row(W_Q), row(W_QM)],
        out_shape=[jax.ShapeDtypeStruct((t, CONV_CH), BF16), jax.ShapeDtypeStruct((t, W_Q), BF16),
                   jax.ShapeDtypeStruct((t, W_QM), BF16)],
        scratch_shapes=[pltpu.VMEM((CONV_HALO + tq, CONV_CH), F32),
                        pltpu.VMEM((SUBLANES, CONV_HALO + tq - SUBLANES, CONV_CH), F32),
                        pltpu.VMEM((4, WINDOW + tq, LANES), BF16),
                        pltpu.VMEM((4, WINDOW + tq, LANES), BF16)],
        compiler_params=_params("parallel", "arbitrary"),
        name="branches_prompt",
    )(sinks, u, u, q, k, k, v, v, qm, mkv, mkv, bias, dww, dwb, lng, lnb)


def _branches_sample_kernel(sinks_ref, u_ref, cc_ref, qx_ref, kn_ref, vn_ref, ck_ref, cv_ref, qm_ref,
                            cmk_ref, cmv_ref, bias_ref, bias0_ref, dww_ref, dwb_ref, lng_ref, lnb_ref,
                            conv_ref, swa_ref, mem_ref, *, nb):
    hist = CONV_WIDTH - 1
    y = jnp.sum(cc_ref[...] * dww_ref[0:hist, :][None], axis=1)
    y = y + u_ref[...] * dww_ref[hist:hist + 1, :] + dwb_ref[...]
    conv_ref[...] = _ln_silu(y, lng_ref[...], lnb_ref[...]).astype(BF16)

    sink = sinks_ref[...]
    for n in range(nb):
        qx = qx_ref[n]
        s = _dot_nt(qx, ck_ref[n].astype(BF16)) + bias_ref[...]
        kn = kn_ref[n:n + 1, :].astype(BF16).astype(F32)
        s_new = jnp.sum(qx.astype(F32) * kn, axis=-1, keepdims=True) + bias0_ref[...]
        m = jnp.maximum(jnp.maximum(jnp.max(s, axis=-1, keepdims=True), s_new), sink)
        pr, pr_new = jnp.exp(s - m), jnp.exp(s_new - m)
        den = jnp.sum(pr, axis=-1, keepdims=True) + pr_new + jnp.exp(sink - m)
        vn = vn_ref[n:n + 1, :].astype(BF16).astype(F32)
        o = _dot((pr / den).astype(BF16), cv_ref[n].astype(BF16))
        swa_ref[n] = o + (pr_new / den).astype(BF16).astype(F32) * vn

        bf = lambda a: a.astype(BF16).astype(F32)
        sm = jnp.sum(bf(cmk_ref[n]) * qm_ref[n].astype(F32)[None], axis=-1, keepdims=True) * MEM_SCALE
        e = jnp.exp(sm - jnp.max(sm, axis=0, keepdims=True))
        w = e / jnp.sum(e, axis=0, keepdims=True)
        mem_ref[n] = jnp.sum(bf(w) * bf(cmv_ref[n]), axis=0).astype(BF16)


def _branches_sample(sinks, u, cache_conv, qx, kn, vn, ck, cv, qm, cmk, cmv, bias, bias0,
                     dww, dwb, lng, lnb, nb):
    n = u.shape[0]
    row = lambda c: pl.BlockSpec((nb, c), lambda i: (i, 0))
    blk3 = lambda a, c: pl.BlockSpec((nb, a, c), lambda i: (i, 0, 0))
    return pl.pallas_call(
        functools.partial(_branches_sample_kernel, nb=nb),
        grid=(n // nb,),
        in_specs=[_const_spec((SWA_HEADS, 1)),
                  row(CONV_CH), blk3(CONV_WIDTH - 1, CONV_CH), blk3(SWA_HEADS, LANES),
                  row(W_KV), row(W_KV), blk3(WINDOW, W_KV), blk3(WINDOW, W_KV),
                  blk3(MEM_HEADS, MEM_HEAD_DIM),
                  pl.BlockSpec((nb, MEM_LEN, MEM_HEADS, MEM_HEAD_DIM), lambda i: (i, 0, 0, 0)),
                  pl.BlockSpec((nb, MEM_LEN, MEM_HEADS, MEM_HEAD_DIM), lambda i: (i, 0, 0, 0)),
                  _const_spec((SWA_HEADS, WINDOW)), _const_spec((SWA_HEADS, 1)),
                  _const_spec((CONV_WIDTH, CONV_CH)), _const_spec((1, CONV_CH)),
                  _const_spec((1, CONV_CH)), _const_spec((1, CONV_CH))],
        out_specs=[row(CONV_CH), blk3(SWA_HEADS, LANES), blk3(MEM_HEADS, MEM_HEAD_DIM)],
        out_shape=[jax.ShapeDtypeStruct((n, CONV_CH), BF16),
                   jax.ShapeDtypeStruct((n, SWA_HEADS, LANES), F32),
                   jax.ShapeDtypeStruct((n, MEM_HEADS, MEM_HEAD_DIM), BF16)],
        compiler_params=_params("parallel"),
        name="branches_sample",
    )(sinks, u, cache_conv, qx, kn, vn, ck, cv, qm, cmk, cmv, bias, bias0, dww, dwb, lng, lnb)


def _merge_kernel(x_ref, g_ref, conv_ref, swa_ref, mem_ref, wg_ref, wco_ref, wso_ref, wmo_ref, wo_ref,
                  o_ref):
    x = x_ref[...]
    h = _rms(x, g_ref[...]).astype(BF16)
    merged = None
    for br, (a_ref, w_ref) in enumerate(((conv_ref, wco_ref), (swa_ref, wso_ref), (mem_ref, wmo_ref))):
        gate = jax.nn.sigmoid(_dot(h, wg_ref[:, br * D_MODEL:(br + 1) * D_MODEL]))
        term = gate * _dot(a_ref[...], w_ref[...])
        merged = term if merged is None else merged + term
    o_ref[...] = x + _dot(merged.astype(BF16), wo_ref[...])


def _merge(x, g, conv, swa, mem, wg, wco, wso, wmo, wo, tm):
    t = x.shape[0]
    row = lambda n: pl.BlockSpec((tm, n), lambda i: (i, 0))
    return pl.pallas_call(
        _merge_kernel,
        grid=(t // tm,),
        in_specs=[row(D_MODEL), _const_spec((1, D_MODEL)), row(CONV_CH), row(W_Q), row(W_QM),
                  _const_spec((D_MODEL, 3 * D_MODEL)), _const_spec((CONV_CH, D_MODEL)),
                  _const_spec((W_Q, D_MODEL)), _const_spec((W_QM, D_MODEL)),
                  _const_spec((D_MODEL, D_MODEL))],
        out_specs=row(D_MODEL),
        out_shape=jax.ShapeDtypeStruct((t, D_MODEL), F32),
        compiler_params=_params("parallel"),
        name="merge",
    )(x, g, conv, swa, mem, wg, wco, wso, wmo, wo)


def _gelu(x):
    return 0.5 * x * (1.0 + lax.erf(x * (2.0 ** -0.5)))


def _top_values(arrs, count, with_rank=False):
    out = []
    ranks = [jnp.full(a.shape, float(count), F32) for a in arrs] if with_rank else None
    for it in range(count):
        m = jnp.max(functools.reduce(jnp.maximum, arrs), axis=0, keepdims=True)
        out.append(m)
        hit = [a == m for a in arrs]
        if with_rank:
            ranks = [jnp.where(hh, float(it), rk) for hh, rk in zip(hit, ranks)]
        arrs = [jnp.where(hh, -jnp.inf, a) for hh, a in zip(hit, arrs)]
    return (out, ranks) if with_rank else out


def _oddeven_merge_sort(lo, hi):
    def merge(lo, hi, r):
        step = 2 * r
        if step < hi - lo:
            yield from merge(lo, hi, step)
            yield from merge(lo + r, hi, step)
            yield from ((i, i + r) for i in range(lo + r, hi - r, step))
        else:
            yield (lo, lo + r)

    if hi > lo:
        mid = lo + (hi - lo) // 2
        yield from _oddeven_merge_sort(lo, mid)
        yield from _oddeven_merge_sort(mid + 1, hi)
        yield from merge(lo, hi, 1)


def _compare_exchange(a, i, j):
    a[i], a[j] = jnp.maximum(a[i], a[j]), jnp.minimum(a[i], a[j])


def _top16_sorted(blocks):
    n = PEER_TOPK
    assert len(blocks) == n and blocks[0].shape[0] == SUBLANES
    a = list(blocks)
    for i, j in _oddeven_merge_sort(0, n - 1):
        _compare_exchange(a, i, j)
    for shift in (4, 2, 1):
        b = [pltpu.roll(x, shift, 0) for x in a]
        a = [jnp.maximum(a[i], b[n - 1 - i]) for i in range(n)]
        d = n // 2
        while d:
            for i in range(n):
                if not i & d:
                    _compare_exchange(a, i, i + d)
            d //= 2
    return a


def _peer_route(h, qt, keys_ref, l_s, e0_s, r1_s, e1_s, sv0_s, sv1_s):
    nk, tb = PEER_N_KEYS, qt.shape[1]
    r = 2 * h * PEER_DK_HALF
    s0_all = _dot(keys_ref[h, 0], qt[r:r + PEER_DK_HALF, :])
    s1_all = _dot(keys_ref[h, 1], qt[r + PEER_DK_HALF:r + 2 * PEER_DK_HALF, :])
    for lg in range(tb // LANES):
        lanes = slice(lg * LANES, (lg + 1) * LANES)
        blocks = lambda a: [a[SUBLANES * j:SUBLANES * (j + 1), lanes] for j in range(nk // SUBLANES)]
        s0, s1 = blocks(s0_all), blocks(s1_all)
        top0, top1 = _top16_sorted(s0), _top16_sorted(s1)
        for j in range(PEER_TOPK):
            sv0_s[j:j + 1, lanes] = top0[j][0:1]
            sv1_s[j:j + 1, lanes] = top1[j][0:1]
        a0, a1 = sv0_s[0:8, lanes], sv0_s[8:16, lanes]
        b0, b1 = sv1_s[0:8, lanes], sv1_s[8:16, lanes]
        cands = [a0[0:1] + b0, a0[0:1] + b1] + [a0[a:a + 1] + b0 for a in range(1, 8)] + [a1 + b0[0:1]]
        best = _top_values(cands, PEER_TOPK)
        sel = [cd >= best[PEER_TOPK - 1] for cd in cands]
        z = functools.reduce(jnp.add, [jnp.where(sl, jnp.exp(cd - best[0]), 0.0) for sl, cd in zip(sel, cands)])
        z = jnp.sum(z, axis=0, keepdims=True)
        cnt = [jnp.sum(jnp.where(sl, 1.0, 0.0), axis=0, keepdims=True) for sl in sel[:9]]
        counts = [cnt[0] + cnt[1]] + cnt[2:9]
        tail = jnp.where(sel[9], 1.0, 0.0)
        counts = [jnp.broadcast_to(counts[a] if a < 8 else tail[a - 8:a - 7], (SUBLANES, LANES))
                  for a in range(PEER_TOPK)]
        lrow, rank1 = [], []
        for x0, x1 in zip(s0, s1):
            lx = jnp.zeros_like(x0)
            rx = jnp.full_like(x1, float(PEER_TOPK))
            for a in reversed(range(PEER_TOPK)):
                lx = jnp.where(x0 >= top0[a], counts[a], lx)
                rx = jnp.where(x1 >= top1[a], float(a), rx)
            lrow.append(lx)
            rank1.append(rx)
        l_s[h, :, lanes] = jnp.concatenate(lrow, axis=0)
        e0_s[h, :, lanes] = jnp.exp(jnp.concatenate(s0, axis=0) - top0[0][0:1]) / z
        r1_s[h, :, lanes] = jnp.concatenate(rank1, axis=0).astype(BF16)
        e1_s[h, :, lanes] = jnp.exp(jnp.concatenate(s1, axis=0) - top1[0][0:1]).astype(BF16)


def _peer_gate_chunk(ci, at_ref, ct_ref, lrow_s, erow_s, r1_s, e1_s, cb):
    nk = PEER_N_KEYS
    for j in range(cb):
        g = None
        for h in range(PEER_HEADS):
            lrow = lrow_s[ci, h, j:j + 1, :].astype(BF16)
            erow = erow_s[ci, h, j:j + 1, :].astype(BF16)
            term = jnp.where(r1_s[h] < lrow, e1_s[h], jnp.zeros((), BF16)) * erow
            g = term if g is None else g + term
        rows = slice((ci * cb + j) * nk, (ci * cb + j + 1) * nk)
        ct_ref[rows, :] = _gelu(at_ref[j * nk:(j + 1) * nk, :]).astype(BF16) * g


def _peer_kernel(x_ref, g2_ref, gf_ref, wqt_ref, keys_ref, wda_ref, wdb_ref, wut_ref, o_ref,
                 h_s, l_s, e0_s, r1_s, e1_s, sv0_s, sv1_s, lrow_s, erow_s, at0_s, at1_s, ct_s, acc_s, *, cb):
    s = pl.program_id(1)

    @pl.when(s == 0)
    def _first():
        h_s[...] = _rms(x_ref[...], g2_ref[...]).T.astype(BF16)
        qt = _dot(wqt_ref[...], h_s[...]).astype(BF16)
        for h in range(PEER_HEADS):
            _peer_route(h, qt, keys_ref, l_s, e0_s, r1_s, e1_s, sv0_s, sv1_s)
        at0_s[...] = _dot(wda_ref[...], h_s[...])
        acc_s[...] = jnp.zeros_like(acc_s)

    @pl.when(s > 0)
    def _steady():
        for ci in range(2):
            base = pl.multiple_of((2 * s - 2 + ci) * cb, cb)
            for h in range(PEER_HEADS):
                lrow_s[ci, h] = l_s[h, pl.ds(base, cb), :]
                erow_s[ci, h] = e0_s[h, pl.ds(base, cb), :]
        gate = functools.partial(_peer_gate_chunk, ct_ref=ct_s, lrow_s=lrow_s, erow_s=erow_s,
                                 r1_s=r1_s, e1_s=e1_s, cb=cb)
        gate(0, at0_s)
        at1_s[...] = _dot(wda_ref[...], h_s[...])
        at0_s[...] = _dot(wdb_ref[...], h_s[...])
        gate(1, at1_s)
        acc_s[...] += _dot(wut_ref[...], ct_s[...])

    @pl.when(s == pl.num_programs(1) - 1)
    def _last():
        y = x_ref[...] + acc_s[...].T
        o_ref[...] = _rms(y, gf_ref[...])


def _peer(x, g2, gf, wqt, keys, wd, wut, tb, cb):
    t = x.shape[0]
    ec = cb * PEER_N_KEYS
    nch = wd.shape[0] // ec
    assert nch % 2 == 0 and tb % LANES == 0
    stat = lambda dt: pltpu.VMEM((PEER_HEADS, PEER_N_KEYS, tb), dt)
    return pl.pallas_call(
        functools.partial(_peer_kernel, cb=cb),
        grid=(t // tb, nch // 2 + 1),
        in_specs=[pl.BlockSpec((tb, D_MODEL), lambda i, s: (i, 0)),
                  _const_spec((1, D_MODEL)), _const_spec((1, D_MODEL)),
                  _const_spec((D_MODEL, D_MODEL)),
                  _const_spec((PEER_HEADS, 2, PEER_N_KEYS, PEER_DK_HALF)),
                  pl.BlockSpec((ec, D_MODEL), lambda i, s: (jnp.maximum(2 * s - 1, 0), 0)),
                  pl.BlockSpec((ec, D_MODEL), lambda i, s: (jnp.minimum(2 * s, nch - 1), 0)),
                  pl.BlockSpec((D_MODEL, 2 * ec), lambda i, s: (0, jnp.maximum(s - 1, 0)))],
        out_specs=pl.BlockSpec((tb, D_MODEL), lambda i, s: (i, 0)),
        out_shape=jax.ShapeDtypeStruct((t, D_MODEL), F32),
        scratch_shapes=[pltpu.VMEM((D_MODEL, tb), BF16), stat(F32), stat(F32), stat(BF16), stat(BF16),
                        pltpu.VMEM((PEER_TOPK, tb), F32), pltpu.VMEM((PEER_TOPK, tb), F32),
                        pltpu.VMEM((2, PEER_HEADS, cb, tb), F32), pltpu.VMEM((2, PEER_HEADS, cb, tb), F32),
                        pltpu.VMEM((ec, tb), F32), pltpu.VMEM((ec, tb), F32),
                        pltpu.VMEM((2 * ec, tb), BF16), pltpu.VMEM((D_MODEL, tb), F32)],
        compiler_params=_params("parallel", "arbitrary"),
        name="peer",
    )(x, g2, gf, wqt, keys, wd, wd, wut)


def _rel_bucket(dist):
    n = jnp.maximum(dist, 0)
    max_exact = REL_BUCKETS // 2
    nf = jnp.maximum(n, 1).astype(F32)
    large = max_exact + (jnp.log(nf / max_exact) / jnp.log(REL_MAX_DIST / max_exact)
                         * (REL_BUCKETS - max_exact)).astype(jnp.int32)
    return jnp.where(n < max_exact, n, jnp.minimum(large, REL_BUCKETS - 1))


def _tile(t, cap):
    tm = min(t, cap)
    assert t % tm == 0, (t, tm)
    return tm


def kernel(x_prompt, x_sample, cache_conv, cache_swa_k, cache_swa_v, cache_mem_k, cache_mem_v, mem_prompt, rel_bias_table, norm1_g, w_in, conv_dw_w, conv_dw_b, conv_ln_g, conv_ln_b, w_conv_out, swa_sinks, w_swa_out, mem_norm_g, w_mem_kv, w_mem_out, w_out, norm2_g, peer_w_q, peer_keys, peer_w_down, peer_w_up, final_norm_g):
    assert w_in.shape[0] == 1, "single layer"
    batch, seq, _ = x_prompt.shape
    nsamp = x_sample.shape[0]
    assert x_sample.shape[1] == 1 and seq % WINDOW == 0
    row = lambda a: a.reshape(1, -1)

    w_proj = w_in[0, :, :W_PROJ].astype(BF16)
    w_gate = w_in[0, :, W_PROJ:].astype(BF16)
    g1, g2, gf = row(norm1_g[0]), row(norm2_g[0]), row(final_norm_g)
    dww, dwb = conv_dw_w[0], row(conv_dw_b[0])
    lng, lnb = row(conv_ln_g[0]), row(conv_ln_b[0])
    wco, wso = w_conv_out[0].astype(BF16), w_swa_out[0].astype(BF16)
    wmo, wo = w_mem_out[0].astype(BF16), w_out[0].astype(BF16)
    wqt = peer_w_q[0].T.astype(BF16)
    keys = peer_keys[0].astype(BF16)
    wd = peer_w_down.reshape(-1, D_MODEL).astype(BF16)
    wut = peer_w_up.reshape(-1, D_MODEL).astype(BF16).T
    sinks = swa_sinks[0]

    qi = jnp.arange(WINDOW)[:, None]
    ki = jnp.arange(2 * WINDOW)[None, :]
    def table_rows(dist):
        onehot = (_rel_bucket(dist)[..., None] == jnp.arange(REL_BUCKETS)).astype(F32)
        return jnp.einsum("...b,bh->h...", onehot, rel_bias_table.astype(F32),
                          precision=lax.Precision.HIGHEST)

    bias_p = table_rows(WINDOW + qi - ki)
    bias_s = table_rows(WINDOW - jnp.arange(WINDOW))
    bias_0 = table_rows(jnp.zeros((1,), jnp.int32))

    xp = x_prompt.reshape(batch * seq, D_MODEL)
    mkv = _norm_matmul(mem_prompt.reshape(batch * MEM_LEN, D_MODEL), row(mem_norm_g[0]),
                       w_mem_kv[0].astype(BF16), MEM_LEN)
    u_p, q_p, k_p, v_p, qm_p = _in_proj(xp, g1, w_proj, _tile(batch * seq, 512))
    tq = _tile(seq, 512)
    conv_p, swa_p, mem_p = _branches_prompt(sinks, u_p, q_p, k_p, v_p, qm_p, mkv, bias_p,
                                            dww, dwb, lng, lnb, batch, seq, tq)
    x2_p = _merge(xp, g1, conv_p, swa_p, mem_p, w_gate, wco, wso, wmo, wo, _tile(batch * seq, 512))
    y_p = _peer(x2_p, g2, gf, wqt, keys, wd, wut, _tile(batch * seq, 512), 8)

    xs = x_sample.reshape(nsamp, D_MODEL)
    u_s, q_s, k_s, v_s, qm_s = _in_proj(xs, g1, w_proj, _tile(nsamp, 128))
    q4 = q_s.reshape(nsamp, SWA_KV_HEADS, SWA_HEADS // SWA_KV_HEADS, SWA_HEAD_DIM)
    zq = jnp.zeros_like(q4[:, 0])
    qx = jnp.concatenate([jnp.concatenate([q4[:, 0], zq], -1), jnp.concatenate([zq, q4[:, 1]], -1)], 1)
    ck = cache_swa_k.reshape(nsamp, WINDOW, W_KV)
    cv = cache_swa_v.reshape(nsamp, WINDOW, W_KV)
    cmk = cache_mem_k.reshape(nsamp, MEM_LEN, MEM_HEADS, MEM_HEAD_DIM)
    cmv = cache_mem_v.reshape(nsamp, MEM_LEN, MEM_HEADS, MEM_HEAD_DIM)
    cconv = cache_conv.reshape(nsamp, CONV_WIDTH - 1, CONV_CH)
    conv_s, swa_x, mem_x = _branches_sample(sinks.reshape(SWA_HEADS, 1), u_s, cconv, qx, k_s, v_s, ck, cv,
                                            qm_s.reshape(nsamp, MEM_HEADS, MEM_HEAD_DIM), cmk, cmv,
                                            bias_s, bias_0, dww, dwb, lng, lnb, _tile(nsamp, 8))
    mem_s = mem_x.reshape(nsamp, W_QM)
    sx = swa_x.reshape(nsamp, SWA_KV_HEADS, SWA_HEADS // SWA_KV_HEADS, SWA_KV_HEADS, SWA_HEAD_DIM)
    swa_s = jnp.stack([sx[:, g, :, g] for g in range(SWA_KV_HEADS)], 1).reshape(nsamp, W_Q).astype(BF16)
    x2_s = _merge(xs, g1, conv_s, swa_s, mem_s, w_gate, wco, wso, wmo, wo, _tile(nsamp, 128))
    y_s = _peer(x2_s, g2, gf, wqt, keys, wd, wut, _tile(nsamp, 128), 8)

    hist = CONV_WIDTH - 1
    kv_shape = (SWA_KV_HEADS, SWA_HEAD_DIM)
    mkv5 = mkv.reshape(batch, MEM_LEN, 2, MEM_HEADS, MEM_HEAD_DIM)
    conv_state_p = u_p.reshape(batch, seq, CONV_CH)[:, -hist:]
    swa_k_p = k_p.reshape(batch, seq, *kv_shape)[:, -WINDOW:]
    swa_v_p = v_p.reshape(batch, seq, *kv_shape)[:, -WINDOW:]
    conv_state_s = jnp.concatenate([cache_conv[0][:, 1:], u_s[:, None, :]], axis=1)
    swa_k_s = jnp.concatenate([cache_swa_k[0][:, 1:], k_s.reshape(nsamp, 1, *kv_shape)], axis=1)
    swa_v_s = jnp.concatenate([cache_swa_v[0][:, 1:], v_s.reshape(nsamp, 1, *kv_shape)], axis=1)
    return (y_p.reshape(batch, seq, D_MODEL), y_s.reshape(nsamp, 1, D_MODEL),
            conv_state_p[None], swa_k_p[None], swa_v_p[None],
            mkv5[:, :, 0][None], mkv5[:, :, 1][None],
            conv_state_s[None], swa_k_s[None], swa_v_s[None])
```

```python
import functools

import jax
import jax.numpy as jnp
from jax import lax
from jax.experimental import pallas as pl
from jax.experimental.pallas import tpu as pltpu

F32 = jnp.float32
BF16 = jnp.bfloat16

D_MODEL = 1024
PAST_LEN = 16384
MEM_LEN = 256
CONV_CH = 512
CONV_WIDTH = 31
SWA_HEADS = 8
SWA_KV_HEADS = 2
SWA_HEAD_DIM = 64
WINDOW = 128
SWA_SCALE = SWA_HEAD_DIM ** -0.5
MEM_HEADS = 4
MEM_HEAD_DIM = 128
MEM_SCALE = MEM_HEAD_DIM ** -0.5
REL_BUCKETS = 32
REL_MAX_DIST = 128
PEER_HEADS = 8
PEER_N_KEYS = 128
PEER_DK_HALF = 64
PEER_TOPK = 16
EPS = 1e-6
NEG_INF = -1e30

W_GLU = 2 * CONV_CH
W_Q = SWA_HEADS * SWA_HEAD_DIM
W_KV = SWA_KV_HEADS * SWA_HEAD_DIM
W_QM = MEM_HEADS * MEM_HEAD_DIM
W_PROJ = W_GLU + W_Q + 2 * W_KV + W_QM

VMEM_LIMIT_BYTES = 56 * 1024 * 1024
LANES = 128
SUBLANES = 8
BF16_ROWS = 16
CONV_HALO = 32


def _params(*sem, flags=None):
    return pltpu.CompilerParams(dimension_semantics=sem, vmem_limit_bytes=VMEM_LIMIT_BYTES, flags=flags)


def _rms(x, g):
    return x * lax.rsqrt(jnp.mean(x * x, axis=-1, keepdims=True) + EPS) * g


def _dot(a, b):
    return jnp.dot(a, b, preferred_element_type=F32)


def _dot_nt(a, b):
    return lax.dot_general(a, b, (((1,), (1,)), ((), ())), preferred_element_type=F32)


def _const_spec(shape):
    zeros = (0,) * len(shape)
    return pl.BlockSpec(shape, lambda *_: zeros)


def _norm_matmul_kernel(x_ref, g_ref, w_ref, o_ref):
    o_ref[...] = _dot(_rms(x_ref[...], g_ref[...]).astype(BF16), w_ref[...])


def _norm_matmul(x, g, w, tm):
    t, n = x.shape[0], w.shape[1]
    return pl.pallas_call(
        _norm_matmul_kernel,
        grid=(t // tm,),
        in_specs=[pl.BlockSpec((tm, D_MODEL), lambda i: (i, 0)), _const_spec((1, D_MODEL)),
                  _const_spec((D_MODEL, n))],
        out_specs=pl.BlockSpec((tm, n), lambda i: (i, 0)),
        out_shape=jax.ShapeDtypeStruct((t, n), F32),
        compiler_params=_params("parallel"),
        name="memkv",
    )(x, g, w)


def _in_proj_kernel(x_ref, g_ref, w_ref, u_ref, q_ref, k_ref, v_ref, qm_ref):
    z = _dot(_rms(x_ref[...], g_ref[...]).astype(BF16), w_ref[...])
    a, b = z[:, :CONV_CH], z[:, CONV_CH:W_GLU]
    u_ref[...] = a * jax.nn.sigmoid(b)
    c = W_GLU
    q_ref[...] = (z[:, c:c + W_Q] * SWA_SCALE).astype(BF16)
    c += W_Q
    k_ref[...] = z[:, c:c + W_KV]
    c += W_KV
    v_ref[...] = z[:, c:c + W_KV]
    c += W_KV
    qm_ref[...] = z[:, c:c + W_QM].astype(BF16)


def _in_proj(x, g, w, tm):
    t = x.shape[0]
    row = lambda n: pl.BlockSpec((tm, n), lambda i: (i, 0))
    return pl.pallas_call(
        _in_proj_kernel,
        grid=(t // tm,),
        in_specs=[row(D_MODEL), _const_spec((1, D_MODEL)), _const_spec((D_MODEL, W_PROJ))],
        out_specs=[row(CONV_CH), row(W_Q), row(W_KV), row(W_KV), row(W_QM)],
        out_shape=[jax.ShapeDtypeStruct((t, CONV_CH), F32), jax.ShapeDtypeStruct((t, W_Q), BF16),
                   jax.ShapeDtypeStruct((t, W_KV), F32), jax.ShapeDtypeStruct((t, W_KV), F32),
                   jax.ShapeDtypeStruct((t, W_QM), BF16)],
        compiler_params=_params("parallel"),
        name="in_proj",
    )(x, g, w)


def _ln_silu(y, g, b):
    mu = jnp.mean(y, axis=-1, keepdims=True)
    var = jnp.mean(jnp.square(y - mu), axis=-1, keepdims=True)
    y = (y - mu) * lax.rsqrt(var + EPS) * g + b
    return y * jax.nn.sigmoid(y)


def _softmax_rows(s):
    e = jnp.exp(s - jnp.max(s, axis=-1, keepdims=True))
    return e / jnp.sum(e, axis=-1, keepdims=True)


def _branches_prompt_kernel(sinks_ref, u_ref, uh_ref, q_ref, k_ref, kp_ref, v_ref, vp_ref, qm_ref,
                            mk_ref, mv_ref, bias_ref, dww_ref, dwb_ref, lng_ref, lnb_ref,
                            conv_ref, swa_ref, mem_ref, ubuf, ushift, kbuf, vbuf, *, tq):
    i = pl.program_id(1)
    first = i == 0

    ubuf[0:CONV_HALO, :] = jnp.where(first, 0.0, uh_ref[...])
    ubuf[CONV_HALO:CONV_HALO + tq, :] = u_ref[...]
    nshift = ushift.shape[1]
    for b in range(1, SUBLANES):
        ushift[b] = ubuf[b:b + nshift, :]
    rb = 64
    off = CONV_HALO - (CONV_WIDTH - 1)
    for r in range(tq // rb):
        acc = jnp.broadcast_to(dwb_ref[...], (rb, CONV_CH))
        for j in range(CONV_WIDTH):
            a, b = divmod(off + j, SUBLANES)
            start = r * rb + a * SUBLANES
            rows = ubuf[start:start + rb, :] if b == 0 else ushift[b, start:start + rb, :]
            acc = acc + rows * dww_ref[j:j + 1, :]
        conv_ref[r * rb:(r + 1) * rb, :] = _ln_silu(acc, lng_ref[...], lnb_ref[...]).astype(BF16)

    lane = lax.broadcasted_iota(jnp.int32, (WINDOW + tq, LANES), 1)
    lo = lane < SWA_HEAD_DIM
    for src_ref, prev_ref, buf in ((k_ref, kp_ref, kbuf), (v_ref, vp_ref, vbuf)):
        full = jnp.concatenate([jnp.where(first, 0.0, prev_ref[...]), src_ref[...]], axis=0)
        rolled = pltpu.roll(full, SWA_HEAD_DIM, 1)
        buf[0] = jnp.where(lo, full, 0.0).astype(BF16)
        buf[1] = jnp.where(lo, 0.0, rolled).astype(BF16)
        buf[2] = jnp.where(lo, rolled, 0.0).astype(BF16)
        buf[3] = jnp.where(lo, 0.0, full).astype(BF16)

    qi = lax.broadcasted_iota(jnp.int32, (WINDOW, 2 * WINDOW), 0)
    ki = lax.broadcasted_iota(jnp.int32, (WINDOW, 2 * WINDOW), 1)
    dist = WINDOW + qi - ki
    band = (dist >= 0) & (dist <= WINDOW)
    nqb = tq // WINDOW
    for jb in range(nqb):
        r0 = jb * WINDOW
        kmin = jnp.where(i * nqb + jb > 0, 0, WINDOW)
        mask = band & (ki >= kmin)
        for p in range(SWA_HEADS // 2):
            g = p // 2
            qp = q_ref[r0:r0 + WINDOW, LANES * p:LANES * (p + 1)]
            o = None
            for half in range(2):
                h = 2 * p + half
                s = _dot_nt(qp, kbuf[2 * g + half, r0:r0 + 2 * WINDOW, :]) + bias_ref[h]
                s = jnp.where(mask, s, NEG_INF)
                sink = sinks_ref[h]
                m = jnp.maximum(jnp.max(s, axis=-1, keepdims=True), sink)
                pr = jnp.exp(s - m)
                pr = pr / (jnp.sum(pr, axis=-1, keepdims=True) + jnp.exp(sink - m))
                t = _dot(pr.astype(BF16), vbuf[2 * g + half, r0:r0 + 2 * WINDOW, :])
                o = t if o is None else o + t
            swa_ref[r0:r0 + WINDOW, LANES * p:LANES * (p + 1)] = o.astype(BF16)

    for hm in range(MEM_HEADS):
        sl = slice(hm * MEM_HEAD_DIM, (hm + 1) * MEM_HEAD_DIM)
        w = _softmax_rows(_dot_nt(qm_ref[:, sl], mk_ref[:, sl].astype(BF16)) * MEM_SCALE)
        mem_ref[:, sl] = _dot(w.astype(BF16), mv_ref[:, sl].astype(BF16)).astype(BF16)


def _branches_prompt(sinks, u, q, k, v, qm, mkv, bias, dww, dwb, lng, lnb, batch, seq, tq):
    t = batch * seq
    nq = seq // tq
    row = lambda n: pl.BlockSpec((tq, n), lambda b, i: (b * nq + i, 0))
    halo = lambda rows, n: pl.BlockSpec(
        (rows, n), lambda b, i: (jnp.maximum(b * (seq // rows) + i * (tq // rows) - 1, 0), 0))
    return pl.pallas_call(
        functools.partial(_branches_prompt_kernel, tq=tq),
        grid=(batch, nq),
        in_specs=[pl.BlockSpec(memory_space=pltpu.SMEM),
                  row(CONV_CH), halo(CONV_HALO, CONV_CH), row(W_Q),
                  row(W_KV), halo(WINDOW, W_KV), row(W_KV), halo(WINDOW, W_KV), row(W_QM),
                  pl.BlockSpec((MEM_LEN, W_QM), lambda b, i: (b, 0)),
                  pl.BlockSpec((MEM_LEN, W_QM), lambda b, i: (b, 1)),
                  _const_spec((SWA_HEADS, WINDOW, 2 * WINDOW)),
                  _const_spec((CONV_WIDTH, CONV_CH)), _const_spec((1, CONV_CH)),
                  _const_spec((1, CONV_CH)), _const_spec((1, CONV_CH))],
        out_specs=[row(CONV_CH), row(W_Q), row(W_QM)],
        out_shape=[jax.ShapeDtypeStruct((t, CONV_CH), BF16), jax.ShapeDtypeStruct((t, W_Q), BF16),
                   jax.ShapeDtypeStruct((t, W_QM), BF16)],
        scratch_shapes=[pltpu.VMEM((CONV_HALO + tq, CONV_CH), F32),
                        pltpu.VMEM((SUBLANES, CONV_HALO + tq - SUBLANES, CONV_CH), F32),
                        pltpu.VMEM((4, WINDOW + tq, LANES), BF16),
                        pltpu.VMEM((4, WINDOW + tq, LANES), BF16)],
        compiler_params=_params("parallel", "arbitrary"),
        name="branches_prompt",
    )(sinks, u, u, q, k, k, v, v, qm, mkv, mkv, bias, dww, dwb, lng, lnb)


def _branches_sample_kernel(sinks_ref, u_ref, cc_ref, qx_ref, kn_ref, vn_ref, ck_ref, cv_ref, qm_ref,
                            cmk_ref, cmv_ref, bias_ref, bias0_ref, dww_ref, dwb_ref, lng_ref, lnb_ref,
                            conv_ref, swa_ref, mem_ref, *, nb):
    hist = CONV_WIDTH - 1
    y = jnp.sum(cc_ref[...] * dww_ref[0:hist, :][None], axis=1)
    y = y + u_ref[...] * dww_ref[hist:hist + 1, :] + dwb_ref[...]
    conv_ref[...] = _ln_silu(y, lng_ref[...], lnb_ref[...]).astype(BF16)

    sink = sinks_ref[...]
    for n in range(nb):
        qx = qx_ref[n]
        s = _dot_nt(qx, ck_ref[n].astype(BF16)) + bias_ref[...]
        kn = kn_ref[n:n + 1, :].astype(BF16).astype(F32)
        s_new = jnp.sum(qx.astype(F32) * kn, axis=-1, keepdims=True) + bias0_ref[...]
        m = jnp.maximum(jnp.maximum(jnp.max(s, axis=-1, keepdims=True), s_new), sink)
        pr, pr_new = jnp.exp(s - m), jnp.exp(s_new - m)
        den = jnp.sum(pr, axis=-1, keepdims=True) + pr_new + jnp.exp(sink - m)
        vn = vn_ref[n:n + 1, :].astype(BF16).astype(F32)
        o = _dot((pr / den).astype(BF16), cv_ref[n].astype(BF16))
        swa_ref[n] = o + (pr_new / den).astype(BF16).astype(F32) * vn

        bf = lambda a: a.astype(BF16).astype(F32)
        sm = jnp.sum(bf(cmk_ref[n]) * qm_ref[n].astype(F32)[None], axis=-1, keepdims=True) * MEM_SCALE
        e = jnp.exp(sm - jnp.max(sm, axis=0, keepdims=True))
        w = e / jnp.sum(e, axis=0, keepdims=True)
        mem_ref[n] = jnp.sum(bf(w) * bf(cmv_ref[n]), axis=0).astype(BF16)


def _branches_sample(sinks, u, cache_conv, qx, kn, vn, ck, cv, qm, cmk, cmv, bias, bias0,
                     dww, dwb, lng, lnb, nb):
    n = u.shape[0]
    row = lambda c: pl.BlockSpec((nb, c), lambda i: (i, 0))
    blk3 = lambda a, c: pl.BlockSpec((nb, a, c), lambda i: (i, 0, 0))
    return pl.pallas_call(
        functools.partial(_branches_sample_kernel, nb=nb),
        grid=(n // nb,),
        in_specs=[_const_spec((SWA_HEADS, 1)),
                  row(CONV_CH), blk3(CONV_WIDTH - 1, CONV_CH), blk3(SWA_HEADS, LANES),
                  row(W_KV), row(W_KV), blk3(WINDOW, W_KV), blk3(WINDOW, W_KV),
                  blk3(MEM_HEADS, MEM_HEAD_DIM),
                  pl.BlockSpec((nb, MEM_LEN, MEM_HEADS, MEM_HEAD_DIM), lambda i: (i, 0, 0, 0)),
                  pl.BlockSpec((nb, MEM_LEN, MEM_HEADS, MEM_HEAD_DIM), lambda i: (i, 0, 0, 0)),
                  _const_spec((SWA_HEADS, WINDOW)), _const_spec((SWA_HEADS, 1)),
                  _const_spec((CONV_WIDTH, CONV_CH)), _const_spec((1, CONV_CH)),
                  _const_spec((1, CONV_CH)), _const_spec((1, CONV_CH))],
        out_specs=[row(CONV_CH), blk3(SWA_HEADS, LANES), blk3(MEM_HEADS, MEM_HEAD_DIM)],
        out_shape=[jax.ShapeDtypeStruct((n, CONV_CH), BF16),
                   jax.ShapeDtypeStruct((n, SWA_HEADS, LANES), F32),
                   jax.ShapeDtypeStruct((n, MEM_HEADS, MEM_HEAD_DIM), BF16)],
        compiler_params=_params("parallel"),
        name="branches_sample",
    )(sinks, u, cache_conv, qx, kn, vn, ck, cv, qm, cmk, cmv, bias, bias0, dww, dwb, lng, lnb)


def _merge_kernel(x_ref, g_ref, conv_ref, swa_ref, mem_ref, wg_ref, wco_ref, wso_ref, wmo_ref, wo_ref,
                  o_ref):
    x = x_ref[...]
    h = _rms(x, g_ref[...]).astype(BF16)
    merged = None
    for br, (a_ref, w_ref) in enumerate(((conv_ref, wco_ref), (swa_ref, wso_ref), (mem_ref, wmo_ref))):
        gate = jax.nn.sigmoid(_dot(h, wg_ref[:, br * D_MODEL:(br + 1) * D_MODEL]))
        term = gate * _dot(a_ref[...], w_ref[...])
        merged = term if merged is None else merged + term
    o_ref[...] = x + _dot(merged.astype(BF16), wo_ref[...])


def _merge(x, g, conv, swa, mem, wg, wco, wso, wmo, wo, tm):
    t = x.shape[0]
    row = lambda n: pl.BlockSpec((tm, n), lambda i: (i, 0))
    return pl.pallas_call(
        _merge_kernel,
        grid=(t // tm,),
        in_specs=[row(D_MODEL), _const_spec((1, D_MODEL)), row(CONV_CH), row(W_Q), row(W_QM),
                  _const_spec((D_MODEL, 3 * D_MODEL)), _const_spec((CONV_CH, D_MODEL)),
                  _const_spec((W_Q, D_MODEL)), _const_spec((W_QM, D_MODEL)),
                  _const_spec((D_MODEL, D_MODEL))],
        out_specs=row(D_MODEL),
        out_shape=jax.ShapeDtypeStruct((t, D_MODEL), F32),
        compiler_params=_params("parallel"),
        name="merge",
    )(x, g, conv, swa, mem, wg, wco, wso, wmo, wo)


def _gelu(x):
    return 0.5 * x * (1.0 + lax.erf(x * (2.0 ** -0.5)))


def _top_values(arrs, count, with_rank=False):
    out = []
    ranks = [jnp.full(a.shape, float(count), F32) for a in arrs] if with_rank else None
    for it in range(count):
        m = jnp.max(functools.reduce(jnp.maximum, arrs), axis=0, keepdims=True)
        out.append(m)
        hit = [a == m for a in arrs]
        if with_rank:
            ranks = [jnp.where(hh, float(it), rk) for hh, rk in zip(hit, ranks)]
        arrs = [jnp.where(hh, -jnp.inf, a) for hh, a in zip(hit, arrs)]
    return (out, ranks) if with_rank else out


def _oddeven_merge_sort(lo, hi):
    def merge(lo, hi, r):
        step = 2 * r
        if step < hi - lo:
            yield from merge(lo, hi, step)
            yield from merge(lo + r, hi, step)
            yield from ((i, i + r) for i in range(lo + r, hi - r, step))
        else:
            yield (lo, lo + r)

    if hi > lo:
        mid = lo + (hi - lo) // 2
        yield from _oddeven_merge_sort(lo, mid)
        yield from _oddeven_merge_sort(mid + 1, hi)
        yield from merge(lo, hi, 1)


def _compare_exchange(a, i, j):
    a[i], a[j] = jnp.maximum(a[i], a[j]), jnp.minimum(a[i], a[j])


def _top16_sorted(blocks):
    n = PEER_TOPK
    assert len(blocks) == n and blocks[0].shape[0] == SUBLANES
    a = list(blocks)
    for i, j in _oddeven_merge_sort(0, n - 1):
        _compare_exchange(a, i, j)
    for shift in (4, 2, 1):
        b = [pltpu.roll(x, shift, 0) for x in a]
        a = [jnp.maximum(a[i], b[n - 1 - i]) for i in range(n)]
        d = n // 2
        while d:
            for i in range(n):
                if not i & d:
                    _compare_exchange(a, i, i + d)
            d //= 2
    return a


def _peer_route(h, qt, keys_ref, l_s, e0_s, r1_s, e1_s, sv0_s, sv1_s):
    nk, tb = PEER_N_KEYS, qt.shape[1]
    r = 2 * h * PEER_DK_HALF
    s0_all = _dot(keys_ref[h, 0], qt[r:r + PEER_DK_HALF, :])
    s1_all = _dot(keys_ref[h, 1], qt[r + PEER_DK_HALF:r + 2 * PEER_DK_HALF, :])
    for lg in range(tb // LANES):
        lanes = slice(lg * LANES, (lg + 1) * LANES)
        blocks = lambda a: [a[SUBLANES * j:SUBLANES * (j + 1), lanes] for j in range(nk // SUBLANES)]
        s0, s1 = blocks(s0_all), blocks(s1_all)
        top0, top1 = _top16_sorted(s0), _top16_sorted(s1)
        for j in range(PEER_TOPK):
            sv0_s[j:j + 1, lanes] = top0[j][0:1]
            sv1_s[j:j + 1, lanes] = top1[j][0:1]
        a0, a1 = sv0_s[0:8, lanes], sv0_s[8:16, lanes]
        b0, b1 = sv1_s[0:8, lanes], sv1_s[8:16, lanes]
        cands = [a0[0:1] + b0, a0[0:1] + b1] + [a0[a:a + 1] + b0 for a in range(1, 8)] + [a1 + b0[0:1]]
        best = _top_values(cands, PEER_TOPK)
        sel = [cd >= best[PEER_TOPK - 1] for cd in cands]
        z = functools.reduce(jnp.add, [jnp.where(sl, jnp.exp(cd - best[0]), 0.0) for sl, cd in zip(sel, cands)])
        z = jnp.sum(z, axis=0, keepdims=True)
        cnt = [jnp.sum(jnp.where(sl, 1.0, 0.0), axis=0, keepdims=True) for sl in sel[:9]]
        counts = [cnt[0] + cnt[1]] + cnt[2:9]
        tail = jnp.where(sel[9], 1.0, 0.0)
        counts = [jnp.broadcast_to(counts[a] if a < 8 else tail[a - 8:a - 7], (SUBLANES, LANES))
                  for a in range(PEER_TOPK)]
        lrow, rank1 = [], []
        for x0, x1 in zip(s0, s1):
            lx = jnp.zeros_like(x0)
            rx = jnp.full_like(x1, float(PEER_TOPK))
            for a in reversed(range(PEER_TOPK)):
                lx = jnp.where(x0 >= top0[a], counts[a], lx)
                rx = jnp.where(x1 >= top1[a], float(a), rx)
            lrow.append(lx)
            rank1.append(rx)
        l_s[h, :, lanes] = jnp.concatenate(lrow, axis=0)
        e0_s[h, :, lanes] = jnp.exp(jnp.concatenate(s0, axis=0) - top0[0][0:1]) / z
        r1_s[h, :, lanes] = jnp.concatenate(rank1, axis=0).astype(BF16)
        e1_s[h, :, lanes] = jnp.exp(jnp.concatenate(s1, axis=0) - top1[0][0:1]).astype(BF16)


def _peer_gate_chunk(ci, at_ref, ct_ref, lrow_s, erow_s, r1_s, e1_s, cb):
    nk, tb = PEER_N_KEYS, ct_ref.shape[1]
    tile = (nk // BF16_ROWS, BF16_ROWS, tb)
    row = lambda ref, h, j: jnp.broadcast_to(ref[ci, h, j:j + 1, :], tile[1:]).astype(BF16)[None]
    for j in range(cb):
        g = None
        for h in range(PEER_HEADS):
            term = jnp.where(r1_s[h].reshape(tile) < row(lrow_s, h, j), e1_s[h].reshape(tile),
                             jnp.zeros((), BF16)) * row(erow_s, h, j)
            g = term if g is None else g + term
        rows = slice((ci * cb + j) * nk, (ci * cb + j + 1) * nk)
        ct_ref[rows, :] = (_gelu(at_ref[j * nk:(j + 1) * nk, :].reshape(tile)) * g).reshape(nk, tb)


def _peer_kernel(x_ref, g2_ref, gf_ref, wqt_ref, keys_ref, wda_ref, wdb_ref, wut_ref, o_ref,
                 h_s, l_s, e0_s, r1_s, e1_s, sv0_s, sv1_s, lrow_s, erow_s, at0_s, at1_s, ct_s, acc_s, *, cb):
    s = pl.program_id(1)

    @pl.when(s == 0)
    def _first():
        h_s[...] = _rms(x_ref[...], g2_ref[...]).T.astype(BF16)
        qt = _dot(wqt_ref[...], h_s[...]).astype(BF16)
        for h in range(PEER_HEADS):
            _peer_route(h, qt, keys_ref, l_s, e0_s, r1_s, e1_s, sv0_s, sv1_s)
        at0_s[...] = _dot(wda_ref[...], h_s[...]).astype(BF16)
        acc_s[...] = jnp.zeros_like(acc_s)

    @pl.when(s > 0)
    def _steady():
        for ci in range(2):
            base = pl.multiple_of((2 * s - 2 + ci) * cb, cb)
            for h in range(PEER_HEADS):
                lrow_s[ci, h] = l_s[h, pl.ds(base, cb), :]
                erow_s[ci, h] = e0_s[h, pl.ds(base, cb), :]
        gate = functools.partial(_peer_gate_chunk, ct_ref=ct_s, lrow_s=lrow_s, erow_s=erow_s,
                                 r1_s=r1_s, e1_s=e1_s, cb=cb)
        gate(0, at0_s)
        at1_s[...] = _dot(wda_ref[...], h_s[...]).astype(BF16)
        at0_s[...] = _dot(wdb_ref[...], h_s[...]).astype(BF16)
        gate(1, at1_s)
        acc_s[...] += _dot(wut_ref[...], ct_s[...])

    @pl.when(s == pl.num_programs(1) - 1)
    def _last():
        y = x_ref[...] + acc_s[...].T
        o_ref[...] = _rms(y, gf_ref[...])


def _peer(x, g2, gf, wqt, keys, wd, wut, tb, cb):
    t = x.shape[0]
    ec = cb * PEER_N_KEYS
    nch = wd.shape[0] // ec
    assert nch % 2 == 0 and tb % LANES == 0
    stat = lambda dt: pltpu.VMEM((PEER_HEADS, PEER_N_KEYS, tb), dt)
    return pl.pallas_call(
        functools.partial(_peer_kernel, cb=cb),
        grid=(t // tb, nch // 2 + 1),
        in_specs=[pl.BlockSpec((tb, D_MODEL), lambda i, s: (i, 0)),
                  _const_spec((1, D_MODEL)), _const_spec((1, D_MODEL)),
                  _const_spec((D_MODEL, D_MODEL)),
                  _const_spec((PEER_HEADS, 2, PEER_N_KEYS, PEER_DK_HALF)),
                  pl.BlockSpec((ec, D_MODEL), lambda i, s: (jnp.maximum(2 * s - 1, 0), 0)),
                  pl.BlockSpec((ec, D_MODEL), lambda i, s: (jnp.minimum(2 * s, nch - 1), 0)),
                  pl.BlockSpec((D_MODEL, 2 * ec), lambda i, s: (0, jnp.maximum(s - 1, 0)))],
        out_specs=pl.BlockSpec((tb, D_MODEL), lambda i, s: (i, 0)),
        out_shape=jax.ShapeDtypeStruct((t, D_MODEL), F32),
        scratch_shapes=[pltpu.VMEM((D_MODEL, tb), BF16), stat(F32), stat(F32), stat(BF16), stat(BF16),
                        pltpu.VMEM((PEER_TOPK, tb), F32), pltpu.VMEM((PEER_TOPK, tb), F32),
                        pltpu.VMEM((2, PEER_HEADS, cb, tb), F32), pltpu.VMEM((2, PEER_HEADS, cb, tb), F32),
                        pltpu.VMEM((ec, tb), BF16), pltpu.VMEM((ec, tb), BF16),
                        pltpu.VMEM((2 * ec, tb), BF16), pltpu.VMEM((D_MODEL, tb), F32)],
        compiler_params=_params("parallel", "arbitrary"),
        name="peer",
    )(x, g2, gf, wqt, keys, wd, wd, wut)


def _rel_bucket(dist):
    n = jnp.maximum(dist, 0)
    max_exact = REL_BUCKETS // 2
    nf = jnp.maximum(n, 1).astype(F32)
    large = max_exact + (jnp.log(nf / max_exact) / jnp.log(REL_MAX_DIST / max_exact)
                         * (REL_BUCKETS - max_exact)).astype(jnp.int32)
    return jnp.where(n < max_exact, n, jnp.minimum(large, REL_BUCKETS - 1))


def _tile(t, cap):
    tm = min(t, cap)
    assert t % tm == 0, (t, tm)
    return tm


def kernel(x_prompt, x_sample, cache_conv, cache_swa_k, cache_swa_v, cache_mem_k, cache_mem_v, mem_prompt, rel_bias_table, norm1_g, w_in, conv_dw_w, conv_dw_b, conv_ln_g, conv_ln_b, w_conv_out, swa_sinks, w_swa_out, mem_norm_g, w_mem_kv, w_mem_out, w_out, norm2_g, peer_w_q, peer_keys, peer_w_down, peer_w_up, final_norm_g):
    assert w_in.shape[0] == 1, "single layer"
    batch, seq, _ = x_prompt.shape
    nsamp = x_sample.shape[0]
    assert x_sample.shape[1] == 1 and seq % WINDOW == 0
    row = lambda a: a.reshape(1, -1)

    w_proj = w_in[0, :, :W_PROJ].astype(BF16)
    w_gate = w_in[0, :, W_PROJ:].astype(BF16)
    g1, g2, gf = row(norm1_g[0]), row(norm2_g[0]), row(final_norm_g)
    dww, dwb = conv_dw_w[0], row(conv_dw_b[0])
    lng, lnb = row(conv_ln_g[0]), row(conv_ln_b[0])
    wco, wso = w_conv_out[0].astype(BF16), w_swa_out[0].astype(BF16)
    wmo, wo = w_mem_out[0].astype(BF16), w_out[0].astype(BF16)
    wqt = peer_w_q[0].T.astype(BF16)
    keys = peer_keys[0].astype(BF16)
    wd = peer_w_down.reshape(-1, D_MODEL).astype(BF16)
    wut = peer_w_up.reshape(-1, D_MODEL).astype(BF16).T
    sinks = swa_sinks[0]

    qi = jnp.arange(WINDOW)[:, None]
    ki = jnp.arange(2 * WINDOW)[None, :]
    def table_rows(dist):
        onehot = (_rel_bucket(dist)[..., None] == jnp.arange(REL_BUCKETS)).astype(F32)
        return jnp.einsum("...b,bh->h...", onehot, rel_bias_table.astype(F32),
                          precision=lax.Precision.HIGHEST)

    bias_p = table_rows(WINDOW + qi - ki)
    bias_s = table_rows(WINDOW - jnp.arange(WINDOW))
    bias_0 = table_rows(jnp.zeros((1,), jnp.int32))

    xp = x_prompt.reshape(batch * seq, D_MODEL)
    mkv = _norm_matmul(mem_prompt.reshape(batch * MEM_LEN, D_MODEL), row(mem_norm_g[0]),
                       w_mem_kv[0].astype(BF16), MEM_LEN)
    u_p, q_p, k_p, v_p, qm_p = _in_proj(xp, g1, w_proj, _tile(batch * seq, 512))
    tq = _tile(seq, 512)
    conv_p, swa_p, mem_p = _branches_prompt(sinks, u_p, q_p, k_p, v_p, qm_p, mkv, bias_p,
                                            dww, dwb, lng, lnb, batch, seq, tq)
    x2_p = _merge(xp, g1, conv_p, swa_p, mem_p, w_gate, wco, wso, wmo, wo, _tile(batch * seq, 512))
    y_p = _peer(x2_p, g2, gf, wqt, keys, wd, wut, _tile(batch * seq, 512), 8)

    xs = x_sample.reshape(nsamp, D_MODEL)
    u_s, q_s, k_s, v_s, qm_s = _in_proj(xs, g1, w_proj, _tile(nsamp, 128))
    q4 = q_s.reshape(nsamp, SWA_KV_HEADS, SWA_HEADS // SWA_KV_HEADS, SWA_HEAD_DIM)
    zq = jnp.zeros_like(q4[:, 0])
    qx = jnp.concatenate([jnp.concatenate([q4[:, 0], zq], -1), jnp.concatenate([zq, q4[:, 1]], -1)], 1)
    ck = cache_swa_k.reshape(nsamp, WINDOW, W_KV)
    cv = cache_swa_v.reshape(nsamp, WINDOW, W_KV)
    cmk = cache_mem_k.reshape(nsamp, MEM_LEN, MEM_HEADS, MEM_HEAD_DIM)
    cmv = cache_mem_v.reshape(nsamp, MEM_LEN, MEM_HEADS, MEM_HEAD_DIM)
    cconv = cache_conv.reshape(nsamp, CONV_WIDTH - 1, CONV_CH)
    conv_s, swa_x, mem_x = _branches_sample(sinks.reshape(SWA_HEADS, 1), u_s, cconv, qx, k_s, v_s, ck, cv,
                                            qm_s.reshape(nsamp, MEM_HEADS, MEM_HEAD_DIM), cmk, cmv,
                                            bias_s, bias_0, dww, dwb, lng, lnb, _tile(nsamp, 8))
    mem_s = mem_x.reshape(nsamp, W_QM)
    sx = swa_x.reshape(nsamp, SWA_KV_HEADS, SWA_HEADS // SWA_KV_HEADS, SWA_KV_HEADS, SWA_HEAD_DIM)
    swa_s = jnp.stack([sx[:, g, :, g] for g in range(SWA_KV_HEADS)], 1).reshape(nsamp, W_Q).astype(BF16)
    x2_s = _merge(xs, g1, conv_s, swa_s, mem_s, w_gate, wco, wso, wmo, wo, _tile(nsamp, 128))
    y_s = _peer(x2_s, g2, gf, wqt, keys, wd, wut, _tile(nsamp, 128), 8)

    hist = CONV_WIDTH - 1
    kv_shape = (SWA_KV_HEADS, SWA_HEAD_DIM)
    mkv5 = mkv.reshape(batch, MEM_LEN, 2, MEM_HEADS, MEM_HEAD_DIM)
    conv_state_p = u_p.reshape(batch, seq, CONV_CH)[:, -hist:]
    swa_k_p = k_p.reshape(batch, seq, *kv_shape)[:, -WINDOW:]
    swa_v_p = v_p.reshape(batch, seq, *kv_shape)[:, -WINDOW:]
    conv_state_s = jnp.concatenate([cache_conv[0][:, 1:], u_s[:, None, :]], axis=1)
    swa_k_s = jnp.concatenate([cache_swa_k[0][:, 1:], k_s.reshape(nsamp, 1, *kv_shape)], axis=1)
    swa_v_s = jnp.concatenate([cache_swa_v[0][:, 1:], v_s.reshape(nsamp, 1, *kv_shape)], axis=1)
    return (y_p.reshape(batch, seq, D_MODEL), y_s.reshape(nsamp, 1, D_MODEL),
            conv_state_p[None], swa_k_p[None], swa_v_p[None],
            mkv5[:, :, 0][None], mkv5[:, :, 1][None],
            conv_state_s[None], swa_k_s[None], swa_v_s[None])
```

```python
import functools

import jax
import jax.numpy as jnp
from jax import lax
from jax.experimental import pallas as pl
from jax.experimental.pallas import tpu as pltpu

F32 = jnp.float32
BF16 = jnp.bfloat16

D_MODEL = 1024
PAST_LEN = 16384
MEM_LEN = 256
CONV_CH = 512
CONV_WIDTH = 31
SWA_HEADS = 8
SWA_KV_HEADS = 2
SWA_HEAD_DIM = 64
WINDOW = 128
SWA_SCALE = SWA_HEAD_DIM ** -0.5
MEM_HEADS = 4
MEM_HEAD_DIM = 128
MEM_SCALE = MEM_HEAD_DIM ** -0.5
REL_BUCKETS = 32
REL_MAX_DIST = 128
PEER_HEADS = 8
PEER_N_KEYS = 128
PEER_DK_HALF = 64
PEER_TOPK = 16
EPS = 1e-6
NEG_INF = -1e30

W_GLU = 2 * CONV_CH
W_Q = SWA_HEADS * SWA_HEAD_DIM
W_KV = SWA_KV_HEADS * SWA_HEAD_DIM
W_QM = MEM_HEADS * MEM_HEAD_DIM
W_PROJ = W_GLU + W_Q + 2 * W_KV + W_QM

VMEM_LIMIT_BYTES = 56 * 1024 * 1024
LANES = 128
SUBLANES = 8
BF16_ROWS = 16
CONV_HALO = 32


def _params(*sem, flags=None):
    return pltpu.CompilerParams(dimension_semantics=sem, vmem_limit_bytes=VMEM_LIMIT_BYTES, flags=flags)


def _rms(x, g):
    return x * lax.rsqrt(jnp.mean(x * x, axis=-1, keepdims=True) + EPS) * g


def _dot(a, b):
    return jnp.dot(a, b, preferred_element_type=F32)


def _dot_nt(a, b):
    return lax.dot_general(a, b, (((1,), (1,)), ((), ())), preferred_element_type=F32)


def _const_spec(shape):
    zeros = (0,) * len(shape)
    return pl.BlockSpec(shape, lambda *_: zeros)


def _norm_matmul_kernel(x_ref, g_ref, w_ref, o_ref):
    o_ref[...] = _dot(_rms(x_ref[...], g_ref[...]).astype(BF16), w_ref[...])


def _norm_matmul(x, g, w, tm):
    t, n = x.shape[0], w.shape[1]
    return pl.pallas_call(
        _norm_matmul_kernel,
        grid=(t // tm,),
        in_specs=[pl.BlockSpec((tm, D_MODEL), lambda i: (i, 0)), _const_spec((1, D_MODEL)),
                  _const_spec((D_MODEL, n))],
        out_specs=pl.BlockSpec((tm, n), lambda i: (i, 0)),
        out_shape=jax.ShapeDtypeStruct((t, n), F32),
        compiler_params=_params("parallel"),
        name="memkv",
    )(x, g, w)


def _in_proj_kernel(x_ref, g_ref, w_ref, u_ref, q_ref, k_ref, v_ref, qm_ref):
    z = _dot(_rms(x_ref[...], g_ref[...]).astype(BF16), w_ref[...])
    a, b = z[:, :CONV_CH], z[:, CONV_CH:W_GLU]
    u_ref[...] = a * jax.nn.sigmoid(b)
    c = W_GLU
    q_ref[...] = (z[:, c:c + W_Q] * SWA_SCALE).astype(BF16)
    c += W_Q
    k_ref[...] = z[:, c:c + W_KV]
    c += W_KV
    v_ref[...] = z[:, c:c + W_KV]
    c += W_KV
    qm_ref[...] = z[:, c:c + W_QM].astype(BF16)


def _in_proj(x, g, w, tm):
    t = x.shape[0]
    row = lambda n: pl.BlockSpec((tm, n), lambda i: (i, 0))
    return pl.pallas_call(
        _in_proj_kernel,
        grid=(t // tm,),
        in_specs=[row(D_MODEL), _const_spec((1, D_MODEL)), _const_spec((D_MODEL, W_PROJ))],
        out_specs=[row(CONV_CH), row(W_Q), row(W_KV), row(W_KV), row(W_QM)],
        out_shape=[jax.ShapeDtypeStruct((t, CONV_CH), F32), jax.ShapeDtypeStruct((t, W_Q), BF16),
                   jax.ShapeDtypeStruct((t, W_KV), F32), jax.ShapeDtypeStruct((t, W_KV), F32),
                   jax.ShapeDtypeStruct((t, W_QM), BF16)],
        compiler_params=_params("parallel"),
        name="in_proj",
    )(x, g, w)


def _ln_silu(y, g, b):
    mu = jnp.mean(y, axis=-1, keepdims=True)
    var = jnp.mean(jnp.square(y - mu), axis=-1, keepdims=True)
    y = (y - mu) * lax.rsqrt(var + EPS) * g + b
    return y * jax.nn.sigmoid(y)


def _softmax_rows(s):
    e = jnp.exp(s - jnp.max(s, axis=-1, keepdims=True))
    return e / jnp.sum(e, axis=-1, keepdims=True)


def _branches_prompt_kernel(sinks_ref, u_ref, uh_ref, q_ref, k_ref, kp_ref, v_ref, vp_ref, qm_ref,
                            mk_ref, mv_ref, bias_ref, dww_ref, dwb_ref, lng_ref, lnb_ref,
                            conv_ref, swa_ref, mem_ref, ubuf, ushift, kbuf, vbuf, *, tq):
    i = pl.program_id(1)
    first = i == 0

    ubuf[0:CONV_HALO, :] = jnp.where(first, 0.0, uh_ref[...])
    ubuf[CONV_HALO:CONV_HALO + tq, :] = u_ref[...]
    nshift = ushift.shape[1]
    for b in range(1, SUBLANES):
        ushift[b] = ubuf[b:b + nshift, :]
    rb = 64
    off = CONV_HALO - (CONV_WIDTH - 1)
    for r in range(tq // rb):
        acc = jnp.broadcast_to(dwb_ref[...], (rb, CONV_CH))
        for j in range(CONV_WIDTH):
            a, b = divmod(off + j, SUBLANES)
            start = r * rb + a * SUBLANES
            rows = ubuf[start:start + rb, :] if b == 0 else ushift[b, start:start + rb, :]
            acc = acc + rows * dww_ref[j:j + 1, :]
        conv_ref[r * rb:(r + 1) * rb, :] = _ln_silu(acc, lng_ref[...], lnb_ref[...]).astype(BF16)

    lane = lax.broadcasted_iota(jnp.int32, (WINDOW + tq, LANES), 1)
    lo = lane < SWA_HEAD_DIM
    for src_ref, prev_ref, buf in ((k_ref, kp_ref, kbuf), (v_ref, vp_ref, vbuf)):
        full = jnp.concatenate([jnp.where(first, 0.0, prev_ref[...]), src_ref[...]], axis=0)
        rolled = pltpu.roll(full, SWA_HEAD_DIM, 1)
        buf[0] = jnp.where(lo, full, 0.0).astype(BF16)
        buf[1] = jnp.where(lo, 0.0, rolled).astype(BF16)
        buf[2] = jnp.where(lo, rolled, 0.0).astype(BF16)
        buf[3] = jnp.where(lo, 0.0, full).astype(BF16)

    qi = lax.broadcasted_iota(jnp.int32, (WINDOW, 2 * WINDOW), 0)
    ki = lax.broadcasted_iota(jnp.int32, (WINDOW, 2 * WINDOW), 1)
    dist = WINDOW + qi - ki
    band = (dist >= 0) & (dist <= WINDOW)
    nqb = tq // WINDOW
    for jb in range(nqb):
        r0 = jb * WINDOW
        kmin = jnp.where(i * nqb + jb > 0, 0, WINDOW)
        mask = band & (ki >= kmin)
        for p in range(SWA_HEADS // 2):
            g = p // 2
            qp = q_ref[r0:r0 + WINDOW, LANES * p:LANES * (p + 1)]
            o = None
            for half in range(2):
                h = 2 * p + half
                s = _dot_nt(qp, kbuf[2 * g + half, r0:r0 + 2 * WINDOW, :]) + bias_ref[h]
                s = jnp.where(mask, s, NEG_INF)
                sink = sinks_ref[h]
                m = jnp.maximum(jnp.max(s, axis=-1, keepdims=True), sink)
                pr = jnp.exp(s - m)
                pr = pr / (jnp.sum(pr, axis=-1, keepdims=True) + jnp.exp(sink - m))
                t = _dot(pr.astype(BF16), vbuf[2 * g + half, r0:r0 + 2 * WINDOW, :])
                o = t if o is None else o + t
            swa_ref[r0:r0 + WINDOW, LANES * p:LANES * (p + 1)] = o.astype(BF16)

    for hm in range(MEM_HEADS):
        sl = slice(hm * MEM_HEAD_DIM, (hm + 1) * MEM_HEAD_DIM)
        w = _softmax_rows(_dot_nt(qm_ref[:, sl], mk_ref[:, sl].astype(BF16)) * MEM_SCALE)
        mem_ref[:, sl] = _dot(w.astype(BF16), mv_ref[:, sl].astype(BF16)).astype(BF16)


def _branches_prompt(sinks, u, q, k, v, qm, mkv, bias, dww, dwb, lng, lnb, batch, seq, tq):
    t = batch * seq
    nq = seq // tq
    row = lambda n: pl.BlockSpec((tq, n), lambda b, i: (b * nq + i, 0))
    halo = lambda rows, n: pl.BlockSpec(
        (rows, n), lambda b, i: (jnp.maximum(b * (seq // rows) + i * (tq // rows) - 1, 0), 0))
    return pl.pallas_call(
        functools.partial(_branches_prompt_kernel, tq=tq),
        grid=(batch, nq),
        in_specs=[pl.BlockSpec(memory_space=pltpu.SMEM),
                  row(CONV_CH), halo(CONV_HALO, CONV_CH), row(W_Q),
                  row(W_KV), halo(WINDOW, W_KV), row(W_KV), halo(WINDOW, W_KV), row(W_QM),
                  pl.BlockSpec((MEM_LEN, W_QM), lambda b, i: (b, 0)),
                  pl.BlockSpec((MEM_LEN, W_QM), lambda b, i: (b, 1)),
                  _const_spec((SWA_HEADS, WINDOW, 2 * WINDOW)),
                  _const_spec((CONV_WIDTH, CONV_CH)), _const_spec((1, CONV_CH)),
                  _const_spec((1, CONV_CH)), _const_spec((1, CONV_CH))],
        out_specs=[row(CONV_CH), row(W_Q), row(W_QM)],
        out_shape=[jax.ShapeDtypeStruct((t, CONV_CH), BF16), jax.ShapeDtypeStruct((t, W_Q), BF16),
                   jax.ShapeDtypeStruct((t, W_QM), BF16)],
        scratch_shapes=[pltpu.VMEM((CONV_HALO + tq, CONV_CH), F32),
                        pltpu.VMEM((SUBLANES, CONV_HALO + tq - SUBLANES, CONV_CH), F32),
                        pltpu.VMEM((4, WINDOW + tq, LANES), BF16),
                        pltpu.VMEM((4, WINDOW + tq, LANES), BF16)],
        compiler_params=_params("parallel", "arbitrary"),
        name="branches_prompt",
    )(sinks, u, u, q, k, k, v, v, qm, mkv, mkv, bias, dww, dwb, lng, lnb)


def _branches_sample_kernel(sinks_ref, u_ref, cc_ref, qx_ref, kn_ref, vn_ref, ck_ref, cv_ref, qm_ref,
                            cmk_ref, cmv_ref, bias_ref, bias0_ref, dww_ref, dwb_ref, lng_ref, lnb_ref,
                            conv_ref, swa_ref, mem_ref, *, nb):
    hist = CONV_WIDTH - 1
    y = jnp.sum(cc_ref[...] * dww_ref[0:hist, :][None], axis=1)
    y = y + u_ref[...] * dww_ref[hist:hist + 1, :] + dwb_ref[...]
    conv_ref[...] = _ln_silu(y, lng_ref[...], lnb_ref[...]).astype(BF16)

    sink = sinks_ref[...]
    for n in range(nb):
        qx = qx_ref[n]
        s = _dot_nt(qx, ck_ref[n].astype(BF16)) + bias_ref[...]
        kn = kn_ref[n:n + 1, :].astype(BF16).astype(F32)
        s_new = jnp.sum(qx.astype(F32) * kn, axis=-1, keepdims=True) + bias0_ref[...]
        m = jnp.maximum(jnp.maximum(jnp.max(s, axis=-1, keepdims=True), s_new), sink)
        pr, pr_new = jnp.exp(s - m), jnp.exp(s_new - m)
        den = jnp.sum(pr, axis=-1, keepdims=True) + pr_new + jnp.exp(sink - m)
        vn = vn_ref[n:n + 1, :].astype(BF16).astype(F32)
        o = _dot((pr / den).astype(BF16), cv_ref[n].astype(BF16))
        swa_ref[n] = o + (pr_new / den).astype(BF16).astype(F32) * vn

        bf = lambda a: a.astype(BF16).astype(F32)
        sm = jnp.sum(bf(cmk_ref[n]) * qm_ref[n].astype(F32)[None], axis=-1, keepdims=True) * MEM_SCALE
        e = jnp.exp(sm - jnp.max(sm, axis=0, keepdims=True))
        w = e / jnp.sum(e, axis=0, keepdims=True)
        mem_ref[n] = jnp.sum(bf(w) * bf(cmv_ref[n]), axis=0).astype(BF16)


def _branches_sample(sinks, u, cache_conv, qx, kn, vn, ck, cv, qm, cmk, cmv, bias, bias0,
                     dww, dwb, lng, lnb, nb):
    n = u.shape[0]
    row = lambda c: pl.BlockSpec((nb, c), lambda i: (i, 0))
    blk3 = lambda a, c: pl.BlockSpec((nb, a, c), lambda i: (i, 0, 0))
    return pl.pallas_call(
        functools.partial(_branches_sample_kernel, nb=nb),
        grid=(n // nb,),
        in_specs=[_const_spec((SWA_HEADS, 1)),
                  row(CONV_CH), blk3(CONV_WIDTH - 1, CONV_CH), blk3(SWA_HEADS, LANES),
                  row(W_KV), row(W_KV), blk3(WINDOW, W_KV), blk3(WINDOW, W_KV),
                  blk3(MEM_HEADS, MEM_HEAD_DIM),
                  pl.BlockSpec((nb, MEM_LEN, MEM_HEADS, MEM_HEAD_DIM), lambda i: (i, 0, 0, 0)),
                  pl.BlockSpec((nb, MEM_LEN, MEM_HEADS, MEM_HEAD_DIM), lambda i: (i, 0, 0, 0)),
                  _const_spec((SWA_HEADS, WINDOW)), _const_spec((SWA_HEADS, 1)),
                  _const_spec((CONV_WIDTH, CONV_CH)), _const_spec((1, CONV_CH)),
                  _const_spec((1, CONV_CH)), _const_spec((1, CONV_CH))],
        out_specs=[row(CONV_CH), blk3(SWA_HEADS, LANES), blk3(MEM_HEADS, MEM_HEAD_DIM)],
        out_shape=[jax.ShapeDtypeStruct((n, CONV_CH), BF16),
                   jax.ShapeDtypeStruct((n, SWA_HEADS, LANES), F32),
                   jax.ShapeDtypeStruct((n, MEM_HEADS, MEM_HEAD_DIM), BF16)],
        compiler_params=_params("parallel"),
        name="branches_sample",
    )(sinks, u, cache_conv, qx, kn, vn, ck, cv, qm, cmk, cmv, bias, bias0, dww, dwb, lng, lnb)


def _merge_kernel(x_ref, g_ref, conv_ref, swa_ref, mem_ref, wg_ref, wco_ref, wso_ref, wmo_ref, wo_ref,
                  o_ref):
    x = x_ref[...]
    h = _rms(x, g_ref[...]).astype(BF16)
    merged = None
    for br, (a_ref, w_ref) in enumerate(((conv_ref, wco_ref), (swa_ref, wso_ref), (mem_ref, wmo_ref))):
        gate = jax.nn.sigmoid(_dot(h, wg_ref[:, br * D_MODEL:(br + 1) * D_MODEL]))
        term = gate * _dot(a_ref[...], w_ref[...])
        merged = term if merged is None else merged + term
    o_ref[...] = x + _dot(merged.astype(BF16), wo_ref[...])


def _merge(x, g, conv, swa, mem, wg, wco, wso, wmo, wo, tm):
    t = x.shape[0]
    row = lambda n: pl.BlockSpec((tm, n), lambda i: (i, 0))
    return pl.pallas_call(
        _merge_kernel,
        grid=(t // tm,),
        in_specs=[row(D_MODEL), _const_spec((1, D_MODEL)), row(CONV_CH), row(W_Q), row(W_QM),
                  _const_spec((D_MODEL, 3 * D_MODEL)), _const_spec((CONV_CH, D_MODEL)),
                  _const_spec((W_Q, D_MODEL)), _const_spec((W_QM, D_MODEL)),
                  _const_spec((D_MODEL, D_MODEL))],
        out_specs=row(D_MODEL),
        out_shape=jax.ShapeDtypeStruct((t, D_MODEL), F32),
        compiler_params=_params("parallel"),
        name="merge",
    )(x, g, conv, swa, mem, wg, wco, wso, wmo, wo)


def _gelu(x):
    return 0.5 * x * (1.0 + lax.erf(x * (2.0 ** -0.5)))


def _top_values(arrs, count, with_rank=False):
    out = []
    ranks = [jnp.full(a.shape, float(count), F32) for a in arrs] if with_rank else None
    for it in range(count):
        m = jnp.max(functools.reduce(jnp.maximum, arrs), axis=0, keepdims=True)
        out.append(m)
        hit = [a == m for a in arrs]
        if with_rank:
            ranks = [jnp.where(hh, float(it), rk) for hh, rk in zip(hit, ranks)]
        arrs = [jnp.where(hh, -jnp.inf, a) for hh, a in zip(hit, arrs)]
    return (out, ranks) if with_rank else out


def _oddeven_merge_sort(lo, hi):
    def merge(lo, hi, r):
        step = 2 * r
        if step < hi - lo:
            yield from merge(lo, hi, step)
            yield from merge(lo + r, hi, step)
            yield from ((i, i + r) for i in range(lo + r, hi - r, step))
        else:
            yield (lo, lo + r)

    if hi > lo:
        mid = lo + (hi - lo) // 2
        yield from _oddeven_merge_sort(lo, mid)
        yield from _oddeven_merge_sort(mid + 1, hi)
        yield from merge(lo, hi, 1)


def _compare_exchange(a, i, j):
    if a[j] is None:
        return
    if a[i] is None:
        a[i], a[j] = a[j], None
    else:
        a[i], a[j] = jnp.maximum(a[i], a[j]), jnp.minimum(a[i], a[j])


def _top16_sorted(blocks):
    n = PEER_TOPK
    assert n // 2 < len(blocks) <= n and blocks[0].shape[0] == SUBLANES
    a = list(blocks) + [None] * (n - len(blocks))
    larger = lambda x, y: x if y is None else y if x is None else jnp.maximum(x, y)
    for i, j in _oddeven_merge_sort(0, n - 1):
        _compare_exchange(a, i, j)
    for shift in (4, 2, 1):
        b = [None if x is None else pltpu.roll(x, shift, 0) for x in a]
        a = [larger(a[i], b[n - 1 - i]) for i in range(n)]
        d = n // 2
        while d:
            for i in range(n):
                if not i & d:
                    _compare_exchange(a, i, i + d)
            d //= 2
    return a


def _rank_in_sorted(x, top):
    assert len(top) == 16
    rank, lo = None, [0] * 1
    conds = []
    for width in (8, 4, 2, 1):
        idx = [b + width - 1 for b in lo]
        thr = [top[i] for i in idx]
        for c in reversed(conds):
            thr = [jnp.where(c, thr[2 * k], thr[2 * k + 1]) for k in range(len(thr) // 2)]
        c = x >= thr[0]
        step = jnp.where(c, 0.0, float(width))
        rank = step if rank is None else rank + step
        conds.append(c)
        lo = [b + off for b in lo for off in (0, width)]
    return jnp.where(x >= top[15], rank, 16.0)


def _peer_route(h, qt, keys_ref, l_s, e0_s, r1_s, e1_s, sv0_s, sv1_s):
    nk, tb = PEER_N_KEYS, qt.shape[1]
    r = 2 * h * PEER_DK_HALF
    s0_all = _dot(keys_ref[h, 0], qt[r:r + PEER_DK_HALF, :])
    s1_all = _dot(keys_ref[h, 1], qt[r + PEER_DK_HALF:r + 2 * PEER_DK_HALF, :])
    for lg in range(tb // LANES):
        lanes = slice(lg * LANES, (lg + 1) * LANES)
        blocks = lambda a: [a[SUBLANES * j:SUBLANES * (j + 1), lanes] for j in range(nk // SUBLANES)]
        s0, s1 = blocks(s0_all), blocks(s1_all)
        top0, top1 = _top16_sorted(s0), _top16_sorted(s1)
        for j in range(PEER_TOPK):
            sv0_s[j:j + 1, lanes] = top0[j][0:1]
            sv1_s[j:j + 1, lanes] = top1[j][0:1]
        a0, a1 = sv0_s[0:8, lanes], sv0_s[8:16, lanes]
        b0, b1 = sv1_s[0:8, lanes], sv1_s[8:16, lanes]
        cands = [a0[0:1] + b0, a0[0:1] + b1] + [a0[a:a + 1] + b0 for a in range(1, 8)] + [a1 + b0[0:1]]
        best = _top16_sorted(cands)
        sel = [cd >= best[PEER_TOPK - 1] for cd in cands]
        z = functools.reduce(jnp.add, [jnp.where(sl, jnp.exp(cd - best[0]), 0.0) for sl, cd in zip(sel, cands)])
        z = jnp.sum(z, axis=0, keepdims=True)
        cnt = [jnp.sum(jnp.where(sl, 1.0, 0.0), axis=0, keepdims=True) for sl in sel[:9]]
        counts = [cnt[0] + cnt[1]] + cnt[2:9]
        tail = jnp.where(sel[9], 1.0, 0.0)
        counts = [jnp.broadcast_to(counts[a] if a < 8 else tail[a - 8:a - 7], (SUBLANES, LANES))
                  for a in range(PEER_TOPK)]
        lrow, rank1 = [], []
        for x0, x1 in zip(s0, s1):
            lx = jnp.zeros_like(x0)
            for a in reversed(range(PEER_TOPK)):
                lx = jnp.where(x0 >= top0[a], counts[a], lx)
            lrow.append(lx)
            rank1.append(_rank_in_sorted(x1, top1))
        l_s[h, :, lanes] = jnp.concatenate(lrow, axis=0)
        e0_s[h, :, lanes] = jnp.exp(jnp.concatenate(s0, axis=0) - top0[0][0:1]) / z
        r1_s[h, :, lanes] = jnp.concatenate(rank1, axis=0).astype(BF16)
        e1_s[h, :, lanes] = jnp.exp(jnp.concatenate(s1, axis=0) - top1[0][0:1]).astype(BF16)


def _peer_gate_chunk(ci, at_ref, ct_ref, lrow_s, erow_s, r1_s, e1_s, cb):
    nk, tb = PEER_N_KEYS, ct_ref.shape[1]
    tile = (nk // BF16_ROWS, BF16_ROWS, tb)
    row = lambda ref, h, j: jnp.broadcast_to(ref[ci, h, j:j + 1, :], tile[1:]).astype(BF16)[None]
    for j in range(cb):
        g = None
        for h in range(PEER_HEADS):
            term = jnp.where(r1_s[h].reshape(tile) < row(lrow_s, h, j), e1_s[h].reshape(tile),
                             jnp.zeros((), BF16)) * row(erow_s, h, j)
            g = term if g is None else g + term
        rows = slice((ci * cb + j) * nk, (ci * cb + j + 1) * nk)
        ct_ref[rows, :] = (_gelu(at_ref[j * nk:(j + 1) * nk, :].reshape(tile)) * g).reshape(nk, tb)


def _peer_kernel(x_ref, g2_ref, gf_ref, wqt_ref, keys_ref, wda_ref, wdb_ref, wut_ref, o_ref,
                 h_s, l_s, e0_s, r1_s, e1_s, sv0_s, sv1_s, lrow_s, erow_s, at0_s, at1_s, ct_s, acc_s, *, cb):
    s = pl.program_id(1)

    @pl.when(s == 0)
    def _first():
        h_s[...] = _rms(x_ref[...], g2_ref[...]).T.astype(BF16)
        qt = _dot(wqt_ref[...], h_s[...]).astype(BF16)
        for h in range(PEER_HEADS):
            _peer_route(h, qt, keys_ref, l_s, e0_s, r1_s, e1_s, sv0_s, sv1_s)
        at0_s[...] = _dot(wda_ref[...], h_s[...]).astype(BF16)
        acc_s[...] = jnp.zeros_like(acc_s)

    @pl.when(s > 0)
    def _steady():
        for ci in range(2):
            base = pl.multiple_of((2 * s - 2 + ci) * cb, cb)
            for h in range(PEER_HEADS):
                lrow_s[ci, h] = l_s[h, pl.ds(base, cb), :]
                erow_s[ci, h] = e0_s[h, pl.ds(base, cb), :]
        gate = functools.partial(_peer_gate_chunk, ct_ref=ct_s, lrow_s=lrow_s, erow_s=erow_s,
                                 r1_s=r1_s, e1_s=e1_s, cb=cb)
        gate(0, at0_s)
        at1_s[...] = _dot(wda_ref[...], h_s[...]).astype(BF16)
        at0_s[...] = _dot(wdb_ref[...], h_s[...]).astype(BF16)
        gate(1, at1_s)
        acc_s[...] += _dot(wut_ref[...], ct_s[...])

    @pl.when(s == pl.num_programs(1) - 1)
    def _last():
        y = x_ref[...] + acc_s[...].T
        o_ref[...] = _rms(y, gf_ref[...])


def _peer(x, g2, gf, wqt, keys, wd, wut, tb, cb):
    t = x.shape[0]
    ec = cb * PEER_N_KEYS
    nch = wd.shape[0] // ec
    assert nch % 2 == 0 and tb % LANES == 0
    stat = lambda dt: pltpu.VMEM((PEER_HEADS, PEER_N_KEYS, tb), dt)
    return pl.pallas_call(
        functools.partial(_peer_kernel, cb=cb),
        grid=(t // tb, nch // 2 + 1),
        in_specs=[pl.BlockSpec((tb, D_MODEL), lambda i, s: (i, 0)),
                  _const_spec((1, D_MODEL)), _const_spec((1, D_MODEL)),
                  _const_spec((D_MODEL, D_MODEL)),
                  _const_spec((PEER_HEADS, 2, PEER_N_KEYS, PEER_DK_HALF)),
                  pl.BlockSpec((ec, D_MODEL), lambda i, s: (jnp.maximum(2 * s - 1, 0), 0)),
                  pl.BlockSpec((ec, D_MODEL), lambda i, s: (jnp.minimum(2 * s, nch - 1), 0)),
                  pl.BlockSpec((D_MODEL, 2 * ec), lambda i, s: (0, jnp.maximum(s - 1, 0)))],
        out_specs=pl.BlockSpec((tb, D_MODEL), lambda i, s: (i, 0)),
        out_shape=jax.ShapeDtypeStruct((t, D_MODEL), F32),
        scratch_shapes=[pltpu.VMEM((D_MODEL, tb), BF16), stat(F32), stat(F32), stat(BF16), stat(BF16),
                        pltpu.VMEM((PEER_TOPK, tb), F32), pltpu.VMEM((PEER_TOPK, tb), F32),
                        pltpu.VMEM((2, PEER_HEADS, cb, tb), F32), pltpu.VMEM((2, PEER_HEADS, cb, tb), F32),
                        pltpu.VMEM((ec, tb), BF16), pltpu.VMEM((ec, tb), BF16),
                        pltpu.VMEM((2 * ec, tb), BF16), pltpu.VMEM((D_MODEL, tb), F32)],
        compiler_params=_params("parallel", "arbitrary"),
        name="peer",
    )(x, g2, gf, wqt, keys, wd, wd, wut)


def _rel_bucket(dist):
    n = jnp.maximum(dist, 0)
    max_exact = REL_BUCKETS // 2
    nf = jnp.maximum(n, 1).astype(F32)
    large = max_exact + (jnp.log(nf / max_exact) / jnp.log(REL_MAX_DIST / max_exact)
                         * (REL_BUCKETS - max_exact)).astype(jnp.int32)
    return jnp.where(n < max_exact, n, jnp.minimum(large, REL_BUCKETS - 1))


def _tile(t, cap):
    tm = min(t, cap)
    assert t % tm == 0, (t, tm)
    return tm


def kernel(x_prompt, x_sample, cache_conv, cache_swa_k, cache_swa_v, cache_mem_k, cache_mem_v, mem_prompt, rel_bias_table, norm1_g, w_in, conv_dw_w, conv_dw_b, conv_ln_g, conv_ln_b, w_conv_out, swa_sinks, w_swa_out, mem_norm_g, w_mem_kv, w_mem_out, w_out, norm2_g, peer_w_q, peer_keys, peer_w_down, peer_w_up, final_norm_g):
    assert w_in.shape[0] == 1, "single layer"
    batch, seq, _ = x_prompt.shape
    nsamp = x_sample.shape[0]
    assert x_sample.shape[1] == 1 and seq % WINDOW == 0
    row = lambda a: a.reshape(1, -1)

    w_proj = w_in[0, :, :W_PROJ].astype(BF16)
    w_gate = w_in[0, :, W_PROJ:].astype(BF16)
    g1, g2, gf = row(norm1_g[0]), row(norm2_g[0]), row(final_norm_g)
    dww, dwb = conv_dw_w[0], row(conv_dw_b[0])
    lng, lnb = row(conv_ln_g[0]), row(conv_ln_b[0])
    wco, wso = w_conv_out[0].astype(BF16), w_swa_out[0].astype(BF16)
    wmo, wo = w_mem_out[0].astype(BF16), w_out[0].astype(BF16)
    wqt = peer_w_q[0].T.astype(BF16)
    keys = peer_keys[0].astype(BF16)
    wd = peer_w_down.reshape(-1, D_MODEL).astype(BF16)
    wut = peer_w_up.reshape(-1, D_MODEL).astype(BF16).T
    sinks = swa_sinks[0]

    qi = jnp.arange(WINDOW)[:, None]
    ki = jnp.arange(2 * WINDOW)[None, :]
    def table_rows(dist):
        onehot = (_rel_bucket(dist)[..., None] == jnp.arange(REL_BUCKETS)).astype(F32)
        return jnp.einsum("...b,bh->h...", onehot, rel_bias_table.astype(F32),
                          precision=lax.Precision.HIGHEST)

    bias_p = table_rows(WINDOW + qi - ki)
    bias_s = table_rows(WINDOW - jnp.arange(WINDOW))
    bias_0 = table_rows(jnp.zeros((1,), jnp.int32))

    xp = x_prompt.reshape(batch * seq, D_MODEL)
    mkv = _norm_matmul(mem_prompt.reshape(batch * MEM_LEN, D_MODEL), row(mem_norm_g[0]),
                       w_mem_kv[0].astype(BF16), MEM_LEN)
    u_p, q_p, k_p, v_p, qm_p = _in_proj(xp, g1, w_proj, _tile(batch * seq, 512))
    tq = _tile(seq, 512)
    conv_p, swa_p, mem_p = _branches_prompt(sinks, u_p, q_p, k_p, v_p, qm_p, mkv, bias_p,
                                            dww, dwb, lng, lnb, batch, seq, tq)
    x2_p = _merge(xp, g1, conv_p, swa_p, mem_p, w_gate, wco, wso, wmo, wo, _tile(batch * seq, 512))
    y_p = _peer(x2_p, g2, gf, wqt, keys, wd, wut, _tile(batch * seq, 512), 8)

    xs = x_sample.reshape(nsamp, D_MODEL)
    u_s, q_s, k_s, v_s, qm_s = _in_proj(xs, g1, w_proj, _tile(nsamp, 128))
    q4 = q_s.reshape(nsamp, SWA_KV_HEADS, SWA_HEADS // SWA_KV_HEADS, SWA_HEAD_DIM)
    zq = jnp.zeros_like(q4[:, 0])
    qx = jnp.concatenate([jnp.concatenate([q4[:, 0], zq], -1), jnp.concatenate([zq, q4[:, 1]], -1)], 1)
    ck = cache_swa_k.reshape(nsamp, WINDOW, W_KV)
    cv = cache_swa_v.reshape(nsamp, WINDOW, W_KV)
    cmk = cache_mem_k.reshape(nsamp, MEM_LEN, MEM_HEADS, MEM_HEAD_DIM)
    cmv = cache_mem_v.reshape(nsamp, MEM_LEN, MEM_HEADS, MEM_HEAD_DIM)
    cconv = cache_conv.reshape(nsamp, CONV_WIDTH - 1, CONV_CH)
    conv_s, swa_x, mem_x = _branches_sample(sinks.reshape(SWA_HEADS, 1), u_s, cconv, qx, k_s, v_s, ck, cv,
                                            qm_s.reshape(nsamp, MEM_HEADS, MEM_HEAD_DIM), cmk, cmv,
                                            bias_s, bias_0, dww, dwb, lng, lnb, _tile(nsamp, 8))
    mem_s = mem_x.reshape(nsamp, W_QM)
    sx = swa_x.reshape(nsamp, SWA_KV_HEADS, SWA_HEADS // SWA_KV_HEADS, SWA_KV_HEADS, SWA_HEAD_DIM)
    swa_s = jnp.stack([sx[:, g, :, g] for g in range(SWA_KV_HEADS)], 1).reshape(nsamp, W_Q).astype(BF16)
    x2_s = _merge(xs, g1, conv_s, swa_s, mem_s, w_gate, wco, wso, wmo, wo, _tile(nsamp, 128))
    y_s = _peer(x2_s, g2, gf, wqt, keys, wd, wut, _tile(nsamp, 128), 8)

    hist = CONV_WIDTH - 1
    kv_shape = (SWA_KV_HEADS, SWA_HEAD_DIM)
    mkv5 = mkv.reshape(batch, MEM_LEN, 2, MEM_HEADS, MEM_HEAD_DIM)
    conv_state_p = u_p.reshape(batch, seq, CONV_CH)[:, -hist:]
    swa_k_p = k_p.reshape(batch, seq, *kv_shape)[:, -WINDOW:]
    swa_v_p = v_p.reshape(batch, seq, *kv_shape)[:, -WINDOW:]
    conv_state_s = jnp.concatenate([cache_conv[0][:, 1:], u_s[:, None, :]], axis=1)
    swa_k_s = jnp.concatenate([cache_swa_k[0][:, 1:], k_s.reshape(nsamp, 1, *kv_shape)], axis=1)
    swa_v_s = jnp.concatenate([cache_swa_v[0][:, 1:], v_s.reshape(nsamp, 1, *kv_shape)], axis=1)
    return (y_p.reshape(batch, seq, D_MODEL), y_s.reshape(nsamp, 1, D_MODEL),
            conv_state_p[None], swa_k_p[None], swa_v_p[None],
            mkv5[:, :, 0][None], mkv5[:, :, 1][None],
            conv_state_s[None], swa_k_s[None], swa_v_s[None])
```

```python
import functools

import jax
import jax.numpy as jnp
from jax import lax
from jax.experimental import pallas as pl
from jax.experimental.pallas import tpu as pltpu

F32 = jnp.float32
BF16 = jnp.bfloat16

D_MODEL = 1024
PAST_LEN = 16384
MEM_LEN = 256
CONV_CH = 512
CONV_WIDTH = 31
SWA_HEADS = 8
SWA_KV_HEADS = 2
SWA_HEAD_DIM = 64
WINDOW = 128
SWA_SCALE = SWA_HEAD_DIM ** -0.5
MEM_HEADS = 4
MEM_HEAD_DIM = 128
MEM_SCALE = MEM_HEAD_DIM ** -0.5
REL_BUCKETS = 32
REL_MAX_DIST = 128
PEER_HEADS = 8
PEER_N_KEYS = 128
PEER_DK_HALF = 64
PEER_TOPK = 16
EPS = 1e-6
NEG_INF = -1e30

W_GLU = 2 * CONV_CH
W_Q = SWA_HEADS * SWA_HEAD_DIM
W_KV = SWA_KV_HEADS * SWA_HEAD_DIM
W_QM = MEM_HEADS * MEM_HEAD_DIM
W_PROJ = W_GLU + W_Q + 2 * W_KV + W_QM

VMEM_LIMIT_BYTES = 56 * 1024 * 1024
LANES = 128
SUBLANES = 8
BF16_ROWS = 16
CONV_HALO = 32


def _params(*sem, flags=None):
    return pltpu.CompilerParams(dimension_semantics=sem, vmem_limit_bytes=VMEM_LIMIT_BYTES, flags=flags)


def _rms(x, g):
    return x * lax.rsqrt(jnp.mean(x * x, axis=-1, keepdims=True) + EPS) * g


def _dot(a, b):
    return jnp.dot(a, b, preferred_element_type=F32)


def _dot_nt(a, b):
    return lax.dot_general(a, b, (((1,), (1,)), ((), ())), preferred_element_type=F32)


def _const_spec(shape):
    zeros = (0,) * len(shape)
    return pl.BlockSpec(shape, lambda *_: zeros)


def _norm_matmul_kernel(x_ref, g_ref, w_ref, o_ref):
    o_ref[...] = _dot(_rms(x_ref[...], g_ref[...]).astype(BF16), w_ref[...])


def _norm_matmul(x, g, w, tm):
    t, n = x.shape[0], w.shape[1]
    return pl.pallas_call(
        _norm_matmul_kernel,
        grid=(t // tm,),
        in_specs=[pl.BlockSpec((tm, D_MODEL), lambda i: (i, 0)), _const_spec((1, D_MODEL)),
                  _const_spec((D_MODEL, n))],
        out_specs=pl.BlockSpec((tm, n), lambda i: (i, 0)),
        out_shape=jax.ShapeDtypeStruct((t, n), F32),
        compiler_params=_params("parallel"),
        name="memkv",
    )(x, g, w)


def _in_proj_kernel(x_ref, g_ref, w_ref, u_ref, q_ref, k_ref, v_ref, qm_ref):
    z = _dot(_rms(x_ref[...], g_ref[...]).astype(BF16), w_ref[...])
    a, b = z[:, :CONV_CH], z[:, CONV_CH:W_GLU]
    u_ref[...] = a * jax.nn.sigmoid(b)
    c = W_GLU
    q_ref[...] = (z[:, c:c + W_Q] * SWA_SCALE).astype(BF16)
    c += W_Q
    k_ref[...] = z[:, c:c + W_KV]
    c += W_KV
    v_ref[...] = z[:, c:c + W_KV]
    c += W_KV
    qm_ref[...] = z[:, c:c + W_QM].astype(BF16)


def _in_proj(x, g, w, tm):
    t = x.shape[0]
    row = lambda n: pl.BlockSpec((tm, n), lambda i: (i, 0))
    return pl.pallas_call(
        _in_proj_kernel,
        grid=(t // tm,),
        in_specs=[row(D_MODEL), _const_spec((1, D_MODEL)), _const_spec((D_MODEL, W_PROJ))],
        out_specs=[row(CONV_CH), row(W_Q), row(W_KV), row(W_KV), row(W_QM)],
        out_shape=[jax.ShapeDtypeStruct((t, CONV_CH), F32), jax.ShapeDtypeStruct((t, W_Q), BF16),
                   jax.ShapeDtypeStruct((t, W_KV), F32), jax.ShapeDtypeStruct((t, W_KV), F32),
                   jax.ShapeDtypeStruct((t, W_QM), BF16)],
        compiler_params=_params("parallel"),
        name="in_proj",
    )(x, g, w)


def _ln_silu(y, g, b):
    mu = jnp.mean(y, axis=-1, keepdims=True)
    var = jnp.mean(jnp.square(y - mu), axis=-1, keepdims=True)
    y = (y - mu) * lax.rsqrt(var + EPS) * g + b
    return y * jax.nn.sigmoid(y)


def _softmax_rows(s):
    e = jnp.exp(s - jnp.max(s, axis=-1, keepdims=True))
    return e / jnp.sum(e, axis=-1, keepdims=True)


def _branches_prompt_kernel(sinks_ref, u_ref, uh_ref, q_ref, k_ref, kp_ref, v_ref, vp_ref, qm_ref,
                            mk_ref, mv_ref, bias_ref, dww_ref, dwb_ref, lng_ref, lnb_ref,
                            conv_ref, swa_ref, mem_ref, ubuf, ushift, kbuf, vbuf, *, tq):
    i = pl.program_id(1)
    first = i == 0

    ubuf[0:CONV_HALO, :] = jnp.where(first, 0.0, uh_ref[...])
    ubuf[CONV_HALO:CONV_HALO + tq, :] = u_ref[...]
    nshift = ushift.shape[1]
    for b in range(1, SUBLANES):
        ushift[b] = ubuf[b:b + nshift, :]
    rb = 64
    off = CONV_HALO - (CONV_WIDTH - 1)
    for r in range(tq // rb):
        acc = jnp.broadcast_to(dwb_ref[...], (rb, CONV_CH))
        for j in range(CONV_WIDTH):
            a, b = divmod(off + j, SUBLANES)
            start = r * rb + a * SUBLANES
            rows = ubuf[start:start + rb, :] if b == 0 else ushift[b, start:start + rb, :]
            acc = acc + rows * dww_ref[j:j + 1, :]
        conv_ref[r * rb:(r + 1) * rb, :] = _ln_silu(acc, lng_ref[...], lnb_ref[...]).astype(BF16)

    lane = lax.broadcasted_iota(jnp.int32, (WINDOW + tq, LANES), 1)
    lo = lane < SWA_HEAD_DIM
    for src_ref, prev_ref, buf in ((k_ref, kp_ref, kbuf), (v_ref, vp_ref, vbuf)):
        full = jnp.concatenate([jnp.where(first, 0.0, prev_ref[...]), src_ref[...]], axis=0)
        rolled = pltpu.roll(full, SWA_HEAD_DIM, 1)
        buf[0] = jnp.where(lo, full, 0.0).astype(BF16)
        buf[1] = jnp.where(lo, 0.0, rolled).astype(BF16)
        buf[2] = jnp.where(lo, rolled, 0.0).astype(BF16)
        buf[3] = jnp.where(lo, 0.0, full).astype(BF16)

    qi = lax.broadcasted_iota(jnp.int32, (WINDOW, 2 * WINDOW), 0)
    ki = lax.broadcasted_iota(jnp.int32, (WINDOW, 2 * WINDOW), 1)
    dist = WINDOW + qi - ki
    band = (dist >= 0) & (dist <= WINDOW)
    nqb = tq // WINDOW
    for jb in range(nqb):
        r0 = jb * WINDOW
        kmin = jnp.where(i * nqb + jb > 0, 0, WINDOW)
        mask = band & (ki >= kmin)
        for p in range(SWA_HEADS // 2):
            g = p // 2
            qp = q_ref[r0:r0 + WINDOW, LANES * p:LANES * (p + 1)]
            o = None
            for half in range(2):
                h = 2 * p + half
                s = _dot_nt(qp, kbuf[2 * g + half, r0:r0 + 2 * WINDOW, :]) + bias_ref[h]
                s = jnp.where(mask, s, NEG_INF)
                sink = sinks_ref[h]
                m = jnp.maximum(jnp.max(s, axis=-1, keepdims=True), sink)
                pr = jnp.exp(s - m)
                pr = pr / (jnp.sum(pr, axis=-1, keepdims=True) + jnp.exp(sink - m))
                t = _dot(pr.astype(BF16), vbuf[2 * g + half, r0:r0 + 2 * WINDOW, :])
                o = t if o is None else o + t
            swa_ref[r0:r0 + WINDOW, LANES * p:LANES * (p + 1)] = o.astype(BF16)

    for hm in range(MEM_HEADS):
        sl = slice(hm * MEM_HEAD_DIM, (hm + 1) * MEM_HEAD_DIM)
        w = _softmax_rows(_dot_nt(qm_ref[:, sl], mk_ref[:, sl].astype(BF16)) * MEM_SCALE)
        mem_ref[:, sl] = _dot(w.astype(BF16), mv_ref[:, sl].astype(BF16)).astype(BF16)


def _branches_prompt(sinks, u, q, k, v, qm, mkv, bias, dww, dwb, lng, lnb, batch, seq, tq):
    t = batch * seq
    nq = seq // tq
    row = lambda n: pl.BlockSpec((tq, n), lambda b, i: (b * nq + i, 0))
    halo = lambda rows, n: pl.BlockSpec(
        (rows, n), lambda b, i: (jnp.maximum(b * (seq // rows) + i * (tq // rows) - 1, 0), 0))
    return pl.pallas_call(
        functools.partial(_branches_prompt_kernel, tq=tq),
        grid=(batch, nq),
        in_specs=[pl.BlockSpec(memory_space=pltpu.SMEM),
                  row(CONV_CH), halo(CONV_HALO, CONV_CH), row(W_Q),
                  row(W_KV), halo(WINDOW, W_KV), row(W_KV), halo(WINDOW, W_KV), row(W_QM),
                  pl.BlockSpec((MEM_LEN, W_QM), lambda b, i: (b, 0)),
                  pl.BlockSpec((MEM_LEN, W_QM), lambda b, i: (b, 1)),
                  _const_spec((SWA_HEADS, WINDOW, 2 * WINDOW)),
                  _const_spec((CONV_WIDTH, CONV_CH)), _const_spec((1, CONV_CH)),
                  _const_spec((1, CONV_CH)), _const_spec((1, CONV_CH))],
        out_specs=[row(CONV_CH), row(W_Q), row(W_QM)],
        out_shape=[jax.ShapeDtypeStruct((t, CONV_CH), BF16), jax.ShapeDtypeStruct((t, W_Q), BF16),
                   jax.ShapeDtypeStruct((t, W_QM), BF16)],
        scratch_shapes=[pltpu.VMEM((CONV_HALO + tq, CONV_CH), F32),
                        pltpu.VMEM((SUBLANES, CONV_HALO + tq - SUBLANES, CONV_CH), F32),
                        pltpu.VMEM((4, WINDOW + tq, LANES), BF16),
                        pltpu.VMEM((4, WINDOW + tq, LANES), BF16)],
        compiler_params=_params("parallel", "arbitrary"),
        name="branches_prompt",
    )(sinks, u, u, q, k, k, v, v, qm, mkv, mkv, bias, dww, dwb, lng, lnb)


def _branches_sample_kernel(sinks_ref, u_ref, cc_ref, qx_ref, kn_ref, vn_ref, ck_ref, cv_ref, qm_ref,
                            cmk_ref, cmv_ref, bias_ref, bias0_ref, dww_ref, dwb_ref, lng_ref, lnb_ref,
                            conv_ref, swa_ref, mem_ref, *, nb):
    hist = CONV_WIDTH - 1
    y = jnp.sum(cc_ref[...] * dww_ref[0:hist, :][None], axis=1)
    y = y + u_ref[...] * dww_ref[hist:hist + 1, :] + dwb_ref[...]
    conv_ref[...] = _ln_silu(y, lng_ref[...], lnb_ref[...]).astype(BF16)

    sink = sinks_ref[...]
    for n in range(nb):
        qx = qx_ref[n]
        s = _dot_nt(qx, ck_ref[n].astype(BF16)) + bias_ref[...]
        kn = kn_ref[n:n + 1, :].astype(BF16).astype(F32)
        s_new = jnp.sum(qx.astype(F32) * kn, axis=-1, keepdims=True) + bias0_ref[...]
        m = jnp.maximum(jnp.maximum(jnp.max(s, axis=-1, keepdims=True), s_new), sink)
        pr, pr_new = jnp.exp(s - m), jnp.exp(s_new - m)
        den = jnp.sum(pr, axis=-1, keepdims=True) + pr_new + jnp.exp(sink - m)
        vn = vn_ref[n:n + 1, :].astype(BF16).astype(F32)
        o = _dot((pr / den).astype(BF16), cv_ref[n].astype(BF16))
        swa_ref[n] = o + (pr_new / den).astype(BF16).astype(F32) * vn

        bf = lambda a: a.astype(BF16).astype(F32)
        sm = jnp.sum(bf(cmk_ref[n]) * qm_ref[n].astype(F32)[None], axis=-1, keepdims=True) * MEM_SCALE
        e = jnp.exp(sm - jnp.max(sm, axis=0, keepdims=True))
        w = e / jnp.sum(e, axis=0, keepdims=True)
        mem_ref[n] = jnp.sum(bf(w) * bf(cmv_ref[n]), axis=0).astype(BF16)


def _branches_sample(sinks, u, cache_conv, qx, kn, vn, ck, cv, qm, cmk, cmv, bias, bias0,
                     dww, dwb, lng, lnb, nb):
    n = u.shape[0]
    row = lambda c: pl.BlockSpec((nb, c), lambda i: (i, 0))
    blk3 = lambda a, c: pl.BlockSpec((nb, a, c), lambda i: (i, 0, 0))
    return pl.pallas_call(
        functools.partial(_branches_sample_kernel, nb=nb),
        grid=(n // nb,),
        in_specs=[_const_spec((SWA_HEADS, 1)),
                  row(CONV_CH), blk3(CONV_WIDTH - 1, CONV_CH), blk3(SWA_HEADS, LANES),
                  row(W_KV), row(W_KV), blk3(WINDOW, W_KV), blk3(WINDOW, W_KV),
                  blk3(MEM_HEADS, MEM_HEAD_DIM),
                  pl.BlockSpec((nb, MEM_LEN, MEM_HEADS, MEM_HEAD_DIM), lambda i: (i, 0, 0, 0)),
                  pl.BlockSpec((nb, MEM_LEN, MEM_HEADS, MEM_HEAD_DIM), lambda i: (i, 0, 0, 0)),
                  _const_spec((SWA_HEADS, WINDOW)), _const_spec((SWA_HEADS, 1)),
                  _const_spec((CONV_WIDTH, CONV_CH)), _const_spec((1, CONV_CH)),
                  _const_spec((1, CONV_CH)), _const_spec((1, CONV_CH))],
        out_specs=[row(CONV_CH), blk3(SWA_HEADS, LANES), blk3(MEM_HEADS, MEM_HEAD_DIM)],
        out_shape=[jax.ShapeDtypeStruct((n, CONV_CH), BF16),
                   jax.ShapeDtypeStruct((n, SWA_HEADS, LANES), F32),
                   jax.ShapeDtypeStruct((n, MEM_HEADS, MEM_HEAD_DIM), BF16)],
        compiler_params=_params("parallel"),
        name="branches_sample",
    )(sinks, u, cache_conv, qx, kn, vn, ck, cv, qm, cmk, cmv, bias, bias0, dww, dwb, lng, lnb)


def _merge_kernel(x_ref, g_ref, conv_ref, swa_ref, mem_ref, wg_ref, wco_ref, wso_ref, wmo_ref, wo_ref,
                  o_ref):
    x = x_ref[...]
    h = _rms(x, g_ref[...]).astype(BF16)
    merged = None
    for br, (a_ref, w_ref) in enumerate(((conv_ref, wco_ref), (swa_ref, wso_ref), (mem_ref, wmo_ref))):
        gate = jax.nn.sigmoid(_dot(h, wg_ref[:, br * D_MODEL:(br + 1) * D_MODEL]))
        term = gate * _dot(a_ref[...], w_ref[...])
        merged = term if merged is None else merged + term
    o_ref[...] = x + _dot(merged.astype(BF16), wo_ref[...])


def _merge(x, g, conv, swa, mem, wg, wco, wso, wmo, wo, tm):
    t = x.shape[0]
    row = lambda n: pl.BlockSpec((tm, n), lambda i: (i, 0))
    return pl.pallas_call(
        _merge_kernel,
        grid=(t // tm,),
        in_specs=[row(D_MODEL), _const_spec((1, D_MODEL)), row(CONV_CH), row(W_Q), row(W_QM),
                  _const_spec((D_MODEL, 3 * D_MODEL)), _const_spec((CONV_CH, D_MODEL)),
                  _const_spec((W_Q, D_MODEL)), _const_spec((W_QM, D_MODEL)),
                  _const_spec((D_MODEL, D_MODEL))],
        out_specs=row(D_MODEL),
        out_shape=jax.ShapeDtypeStruct((t, D_MODEL), F32),
        compiler_params=_params("parallel"),
        name="merge",
    )(x, g, conv, swa, mem, wg, wco, wso, wmo, wo)


def _gelu(x):
    return 0.5 * x * (1.0 + lax.erf(x * (2.0 ** -0.5)))


def _top_values(arrs, count, with_rank=False):
    out = []
    ranks = [jnp.full(a.shape, float(count), F32) for a in arrs] if with_rank else None
    for it in range(count):
        m = jnp.max(functools.reduce(jnp.maximum, arrs), axis=0, keepdims=True)
        out.append(m)
        hit = [a == m for a in arrs]
        if with_rank:
            ranks = [jnp.where(hh, float(it), rk) for hh, rk in zip(hit, ranks)]
        arrs = [jnp.where(hh, -jnp.inf, a) for hh, a in zip(hit, arrs)]
    return (out, ranks) if with_rank else out


def _oddeven_merge_sort(lo, hi):
    def merge(lo, hi, r):
        step = 2 * r
        if step < hi - lo:
            yield from merge(lo, hi, step)
            yield from merge(lo + r, hi, step)
            yield from ((i, i + r) for i in range(lo + r, hi - r, step))
        else:
            yield (lo, lo + r)

    if hi > lo:
        mid = lo + (hi - lo) // 2
        yield from _oddeven_merge_sort(lo, mid)
        yield from _oddeven_merge_sort(mid + 1, hi)
        yield from merge(lo, hi, 1)


def _compare_exchange(a, i, j):
    if a[j] is None:
        return
    if a[i] is None:
        a[i], a[j] = a[j], None
    else:
        a[i], a[j] = jnp.maximum(a[i], a[j]), jnp.minimum(a[i], a[j])


def _top16_sorted(blocks):
    n = PEER_TOPK
    assert n // 2 < len(blocks) <= n and blocks[0].shape[0] == SUBLANES
    a = list(blocks) + [None] * (n - len(blocks))
    larger = lambda x, y: x if y is None else y if x is None else jnp.maximum(x, y)
    for i, j in _oddeven_merge_sort(0, n - 1):
        _compare_exchange(a, i, j)
    for shift in (4, 2, 1):
        b = [None if x is None else pltpu.roll(x, shift, 0) for x in a]
        a = [larger(a[i], b[n - 1 - i]) for i in range(n)]
        d = n // 2
        while d:
            for i in range(n):
                if not i & d:
                    _compare_exchange(a, i, i + d)
            d //= 2
    return a


def _rank_in_sorted(x, top):
    assert len(top) == 16
    rank, lo = None, [0] * 1
    conds = []
    for width in (8, 4, 2, 1):
        idx = [b + width - 1 for b in lo]
        thr = [top[i] for i in idx]
        for c in reversed(conds):
            thr = [jnp.where(c, thr[2 * k], thr[2 * k + 1]) for k in range(len(thr) // 2)]
        c = x >= thr[0]
        step = jnp.where(c, 0.0, float(width))
        rank = step if rank is None else rank + step
        conds.append(c)
        lo = [b + off for b in lo for off in (0, width)]
    return jnp.where(x >= top[15], rank, 16.0)


def _peer_route(h, qt, keys_ref, l_s, e0_s, r1_s, e1_s, sv0_s, sv1_s):
    nk, tb = PEER_N_KEYS, qt.shape[1]
    r = 2 * h * PEER_DK_HALF
    s0_all = _dot(keys_ref[h, 0], qt[r:r + PEER_DK_HALF, :])
    s1_all = _dot(keys_ref[h, 1], qt[r + PEER_DK_HALF:r + 2 * PEER_DK_HALF, :])
    for lg in range(tb // LANES):
        lanes = slice(lg * LANES, (lg + 1) * LANES)
        blocks = lambda a: [a[SUBLANES * j:SUBLANES * (j + 1), lanes] for j in range(nk // SUBLANES)]
        s0, s1 = blocks(s0_all), blocks(s1_all)
        top0, top1 = _top16_sorted(s0), _top16_sorted(s1)
        for j in range(PEER_TOPK):
            sv0_s[j:j + 1, lanes] = top0[j][0:1]
            sv1_s[j:j + 1, lanes] = top1[j][0:1]
        a0, a1 = sv0_s[0:8, lanes], sv0_s[8:16, lanes]
        b0, b1 = sv1_s[0:8, lanes], sv1_s[8:16, lanes]
        cands = [a0[0:1] + b0, a0[0:1] + b1] + [a0[a:a + 1] + b0 for a in range(1, 8)] + [a1 + b0[0:1]]
        best = _top16_sorted(cands)
        sel = [cd >= best[PEER_TOPK - 1] for cd in cands]
        z = functools.reduce(jnp.add, [jnp.where(sl, jnp.exp(cd - best[0]), 0.0) for sl, cd in zip(sel, cands)])
        z = jnp.sum(z, axis=0, keepdims=True)
        cnt = [jnp.sum(jnp.where(sl, 1.0, 0.0), axis=0, keepdims=True) for sl in sel[:9]]
        counts = [cnt[0] + cnt[1]] + cnt[2:9]
        tail = jnp.where(sel[9], 1.0, 0.0)
        counts = [jnp.broadcast_to(counts[a] if a < 8 else tail[a - 8:a - 7], (SUBLANES, LANES))
                  for a in range(PEER_TOPK)]
        lrow, rank1 = [], []
        for x0, x1 in zip(s0, s1):
            lx = jnp.zeros_like(x0)
            for a in reversed(range(PEER_TOPK)):
                lx = jnp.where(x0 >= top0[a], counts[a], lx)
            lrow.append(lx)
            rank1.append(_rank_in_sorted(x1, top1))
        l_s[h, :, lanes] = jnp.concatenate(lrow, axis=0)
        e0_s[h, :, lanes] = jnp.exp(jnp.concatenate(s0, axis=0) - top0[0][0:1]) / z
        r1_s[h, :, lanes] = jnp.concatenate(rank1, axis=0).astype(BF16)
        e1_s[h, :, lanes] = jnp.exp(jnp.concatenate(s1, axis=0) - top1[0][0:1]).astype(BF16)


PEER_GATE_GROUP = 4


def _peer_gate_chunk(ci, at_ref, ct_ref, lrow_s, erow_s, r1_s, e1_s, cb):
    nk, tb, grp = PEER_N_KEYS, ct_ref.shape[1], PEER_GATE_GROUP
    tile = (nk // BF16_ROWS, BF16_ROWS, LANES)
    for lg in range(tb // LANES):
        lanes = slice(lg * LANES, (lg + 1) * LANES)
        row = lambda ref, h, j: jnp.broadcast_to(ref[ci, h, j:j + 1, lanes], tile[1:]).astype(BF16)[None]
        for j0 in range(0, cb, grp):
            g = [None] * grp
            for h in range(PEER_HEADS):
                r1, e1 = r1_s[h, :, lanes].reshape(tile), e1_s[h, :, lanes].reshape(tile)
                for k in range(grp):
                    term = jnp.where(r1 < row(lrow_s, h, j0 + k), e1, jnp.zeros((), BF16)) * row(erow_s, h, j0 + k)
                    g[k] = term if g[k] is None else g[k] + term
            for k in range(grp):
                j = j0 + k
                rows = slice((ci * cb + j) * nk, (ci * cb + j + 1) * nk)
                act = _gelu(at_ref[j * nk:(j + 1) * nk, lanes].reshape(tile))
                ct_ref[rows, lanes] = (act * g[k]).reshape(nk, LANES)


def _peer_kernel(x_ref, g2_ref, gf_ref, wqt_ref, keys_ref, wda_ref, wdb_ref, wut_ref, o_ref,
                 h_s, l_s, e0_s, r1_s, e1_s, sv0_s, sv1_s, lrow_s, erow_s, at0_s, at1_s, ct_s, acc_s, *, cb):
    s = pl.program_id(1)

    @pl.when(s == 0)
    def _first():
        h_s[...] = _rms(x_ref[...], g2_ref[...]).T.astype(BF16)
        qt = _dot(wqt_ref[...], h_s[...]).astype(BF16)
        for h in range(PEER_HEADS):
            _peer_route(h, qt, keys_ref, l_s, e0_s, r1_s, e1_s, sv0_s, sv1_s)
        at0_s[...] = _dot(wda_ref[...], h_s[...]).astype(BF16)
        acc_s[...] = jnp.zeros_like(acc_s)

    @pl.when(s > 0)
    def _steady():
        for ci in range(2):
            base = pl.multiple_of((2 * s - 2 + ci) * cb, cb)
            for h in range(PEER_HEADS):
                lrow_s[ci, h] = l_s[h, pl.ds(base, cb), :]
                erow_s[ci, h] = e0_s[h, pl.ds(base, cb), :]
        gate = functools.partial(_peer_gate_chunk, ct_ref=ct_s, lrow_s=lrow_s, erow_s=erow_s,
                                 r1_s=r1_s, e1_s=e1_s, cb=cb)
        gate(0, at0_s)
        at1_s[...] = _dot(wda_ref[...], h_s[...]).astype(BF16)
        at0_s[...] = _dot(wdb_ref[...], h_s[...]).astype(BF16)
        gate(1, at1_s)
        acc_s[...] += _dot(wut_ref[...], ct_s[...])

    @pl.when(s == pl.num_programs(1) - 1)
    def _last():
        y = x_ref[...] + acc_s[...].T
        o_ref[...] = _rms(y, gf_ref[...])


def _peer(x, g2, gf, wqt, keys, wd, wut, tb, cb):
    t = x.shape[0]
    ec = cb * PEER_N_KEYS
    nch = wd.shape[0] // ec
    assert nch % 2 == 0 and tb % LANES == 0
    stat = lambda dt: pltpu.VMEM((PEER_HEADS, PEER_N_KEYS, tb), dt)
    return pl.pallas_call(
        functools.partial(_peer_kernel, cb=cb),
        grid=(t // tb, nch // 2 + 1),
        in_specs=[pl.BlockSpec((tb, D_MODEL), lambda i, s: (i, 0)),
                  _const_spec((1, D_MODEL)), _const_spec((1, D_MODEL)),
                  _const_spec((D_MODEL, D_MODEL)),
                  _const_spec((PEER_HEADS, 2, PEER_N_KEYS, PEER_DK_HALF)),
                  pl.BlockSpec((ec, D_MODEL), lambda i, s: (jnp.maximum(2 * s - 1, 0), 0)),
                  pl.BlockSpec((ec, D_MODEL), lambda i, s: (jnp.minimum(2 * s, nch - 1), 0)),
                  pl.BlockSpec((D_MODEL, 2 * ec), lambda i, s: (0, jnp.maximum(s - 1, 0)))],
        out_specs=pl.BlockSpec((tb, D_MODEL), lambda i, s: (i, 0)),
        out_shape=jax.ShapeDtypeStruct((t, D_MODEL), F32),
        scratch_shapes=[pltpu.VMEM((D_MODEL, tb), BF16), stat(F32), stat(F32), stat(BF16), stat(BF16),
                        pltpu.VMEM((PEER_TOPK, tb), F32), pltpu.VMEM((PEER_TOPK, tb), F32),
                        pltpu.VMEM((2, PEER_HEADS, cb, tb), F32), pltpu.VMEM((2, PEER_HEADS, cb, tb), F32),
                        pltpu.VMEM((ec, tb), BF16), pltpu.VMEM((ec, tb), BF16),
                        pltpu.VMEM((2 * ec, tb), BF16), pltpu.VMEM((D_MODEL, tb), F32)],
        compiler_params=_params("parallel", "arbitrary"),
        name="peer",
    )(x, g2, gf, wqt, keys, wd, wd, wut)


def _rel_bucket(dist):
    n = jnp.maximum(dist, 0)
    max_exact = REL_BUCKETS // 2
    nf = jnp.maximum(n, 1).astype(F32)
    large = max_exact + (jnp.log(nf / max_exact) / jnp.log(REL_MAX_DIST / max_exact)
                         * (REL_BUCKETS - max_exact)).astype(jnp.int32)
    return jnp.where(n < max_exact, n, jnp.minimum(large, REL_BUCKETS - 1))


def _tile(t, cap):
    tm = min(t, cap)
    assert t % tm == 0, (t, tm)
    return tm


def kernel(x_prompt, x_sample, cache_conv, cache_swa_k, cache_swa_v, cache_mem_k, cache_mem_v, mem_prompt, rel_bias_table, norm1_g, w_in, conv_dw_w, conv_dw_b, conv_ln_g, conv_ln_b, w_conv_out, swa_sinks, w_swa_out, mem_norm_g, w_mem_kv, w_mem_out, w_out, norm2_g, peer_w_q, peer_keys, peer_w_down, peer_w_up, final_norm_g):
    assert w_in.shape[0] == 1, "single layer"
    batch, seq, _ = x_prompt.shape
    nsamp = x_sample.shape[0]
    assert x_sample.shape[1] == 1 and seq % WINDOW == 0
    row = lambda a: a.reshape(1, -1)

    w_proj = w_in[0, :, :W_PROJ].astype(BF16)
    w_gate = w_in[0, :, W_PROJ:].astype(BF16)
    g1, g2, gf = row(norm1_g[0]), row(norm2_g[0]), row(final_norm_g)
    dww, dwb = conv_dw_w[0], row(conv_dw_b[0])
    lng, lnb = row(conv_ln_g[0]), row(conv_ln_b[0])
    wco, wso = w_conv_out[0].astype(BF16), w_swa_out[0].astype(BF16)
    wmo, wo = w_mem_out[0].astype(BF16), w_out[0].astype(BF16)
    wqt = peer_w_q[0].T.astype(BF16)
    keys = peer_keys[0].astype(BF16)
    wd = peer_w_down.reshape(-1, D_MODEL).astype(BF16)
    wut = peer_w_up.reshape(-1, D_MODEL).astype(BF16).T
    sinks = swa_sinks[0]

    qi = jnp.arange(WINDOW)[:, None]
    ki = jnp.arange(2 * WINDOW)[None, :]
    def table_rows(dist):
        onehot = (_rel_bucket(dist)[..., None] == jnp.arange(REL_BUCKETS)).astype(F32)
        return jnp.einsum("...b,bh->h...", onehot, rel_bias_table.astype(F32),
                          precision=lax.Precision.HIGHEST)

    bias_p = table_rows(WINDOW + qi - ki)
    bias_s = table_rows(WINDOW - jnp.arange(WINDOW))
    bias_0 = table_rows(jnp.zeros((1,), jnp.int32))

    xp = x_prompt.reshape(batch * seq, D_MODEL)
    mkv = _norm_matmul(mem_prompt.reshape(batch * MEM_LEN, D_MODEL), row(mem_norm_g[0]),
                       w_mem_kv[0].astype(BF16), MEM_LEN)
    u_p, q_p, k_p, v_p, qm_p = _in_proj(xp, g1, w_proj, _tile(batch * seq, 512))
    tq = _tile(seq, 512)
    conv_p, swa_p, mem_p = _branches_prompt(sinks, u_p, q_p, k_p, v_p, qm_p, mkv, bias_p,
                                            dww, dwb, lng, lnb, batch, seq, tq)
    x2_p = _merge(xp, g1, conv_p, swa_p, mem_p, w_gate, wco, wso, wmo, wo, _tile(batch * seq, 512))
    y_p = _peer(x2_p, g2, gf, wqt, keys, wd, wut, _tile(batch * seq, 512), 8)

    xs = x_sample.reshape(nsamp, D_MODEL)
    u_s, q_s, k_s, v_s, qm_s = _in_proj(xs, g1, w_proj, _tile(nsamp, 128))
    q4 = q_s.reshape(nsamp, SWA_KV_HEADS, SWA_HEADS // SWA_KV_HEADS, SWA_HEAD_DIM)
    zq = jnp.zeros_like(q4[:, 0])
    qx = jnp.concatenate([jnp.concatenate([q4[:, 0], zq], -1), jnp.concatenate([zq, q4[:, 1]], -1)], 1)
    ck = cache_swa_k.reshape(nsamp, WINDOW, W_KV)
    cv = cache_swa_v.reshape(nsamp, WINDOW, W_KV)
    cmk = cache_mem_k.reshape(nsamp, MEM_LEN, MEM_HEADS, MEM_HEAD_DIM)
    cmv = cache_mem_v.reshape(nsamp, MEM_LEN, MEM_HEADS, MEM_HEAD_DIM)
    cconv = cache_conv.reshape(nsamp, CONV_WIDTH - 1, CONV_CH)
    conv_s, swa_x, mem_x = _branches_sample(sinks.reshape(SWA_HEADS, 1), u_s, cconv, qx, k_s, v_s, ck, cv,
                                            qm_s.reshape(nsamp, MEM_HEADS, MEM_HEAD_DIM), cmk, cmv,
                                            bias_s, bias_0, dww, dwb, lng, lnb, _tile(nsamp, 8))
    mem_s = mem_x.reshape(nsamp, W_QM)
    sx = swa_x.reshape(nsamp, SWA_KV_HEADS, SWA_HEADS // SWA_KV_HEADS, SWA_KV_HEADS, SWA_HEAD_DIM)
    swa_s = jnp.stack([sx[:, g, :, g] for g in range(SWA_KV_HEADS)], 1).reshape(nsamp, W_Q).astype(BF16)
    x2_s = _merge(xs, g1, conv_s, swa_s, mem_s, w_gate, wco, wso, wmo, wo, _tile(nsamp, 128))
    y_s = _peer(x2_s, g2, gf, wqt, keys, wd, wut, _tile(nsamp, 128), 8)

    hist = CONV_WIDTH - 1
    kv_shape = (SWA_KV_HEADS, SWA_HEAD_DIM)
    mkv5 = mkv.reshape(batch, MEM_LEN, 2, MEM_HEADS, MEM_HEAD_DIM)
    conv_state_p = u_p.reshape(batch, seq, CONV_CH)[:, -hist:]
    swa_k_p = k_p.reshape(batch, seq, *kv_shape)[:, -WINDOW:]
    swa_v_p = v_p.reshape(batch, seq, *kv_shape)[:, -WINDOW:]
    conv_state_s = jnp.concatenate([cache_conv[0][:, 1:], u_s[:, None, :]], axis=1)
    swa_k_s = jnp.concatenate([cache_swa_k[0][:, 1:], k_s.reshape(nsamp, 1, *kv_shape)], axis=1)
    swa_v_s = jnp.concatenate([cache_swa_v[0][:, 1:], v_s.reshape(nsamp, 1, *kv_shape)], axis=1)
    return (y_p.reshape(batch, seq, D_MODEL), y_s.reshape(nsamp, 1, D_MODEL),
            conv_state_p[None], swa_k_p[None], swa_v_p[None],
            mkv5[:, :, 0][None], mkv5[:, :, 1][None],
            conv_state_s[None], swa_k_s[None], swa_v_s[None])
```

```python
import functools

import jax
import jax.numpy as jnp
from jax import lax
from jax.experimental import pallas as pl
from jax.experimental.pallas import tpu as pltpu

F32 = jnp.float32
BF16 = jnp.bfloat16

D_MODEL = 1024
PAST_LEN = 16384
MEM_LEN = 256
CONV_CH = 512
CONV_WIDTH = 31
SWA_HEADS = 8
SWA_KV_HEADS = 2
SWA_HEAD_DIM = 64
WINDOW = 128
SWA_SCALE = SWA_HEAD_DIM ** -0.5
MEM_HEADS = 4
MEM_HEAD_DIM = 128
MEM_SCALE = MEM_HEAD_DIM ** -0.5
REL_BUCKETS = 32
REL_MAX_DIST = 128
PEER_HEADS = 8
PEER_N_KEYS = 128
PEER_DK_HALF = 64
PEER_TOPK = 16
EPS = 1e-6
NEG_INF = -1e30

W_GLU = 2 * CONV_CH
W_Q = SWA_HEADS * SWA_HEAD_DIM
W_KV = SWA_KV_HEADS * SWA_HEAD_DIM
W_QM = MEM_HEADS * MEM_HEAD_DIM
W_PROJ = W_GLU + W_Q + 2 * W_KV + W_QM

VMEM_LIMIT_BYTES = 56 * 1024 * 1024
LANES = 128
SUBLANES = 8
BF16_ROWS = 16
CONV_HALO = 32


def _params(*sem, flags=None):
    return pltpu.CompilerParams(dimension_semantics=sem, vmem_limit_bytes=VMEM_LIMIT_BYTES, flags=flags)


def _rms(x, g):
    return x * lax.rsqrt(jnp.mean(x * x, axis=-1, keepdims=True) + EPS) * g


def _dot(a, b):
    return jnp.dot(a, b, preferred_element_type=F32)


def _dot_nt(a, b):
    return lax.dot_general(a, b, (((1,), (1,)), ((), ())), preferred_element_type=F32)


def _const_spec(shape):
    zeros = (0,) * len(shape)
    return pl.BlockSpec(shape, lambda *_: zeros)


def _norm_matmul_kernel(x_ref, g_ref, w_ref, o_ref):
    o_ref[...] = _dot(_rms(x_ref[...], g_ref[...]).astype(BF16), w_ref[...])


def _norm_matmul(x, g, w, tm):
    t, n = x.shape[0], w.shape[1]
    return pl.pallas_call(
        _norm_matmul_kernel,
        grid=(t // tm,),
        in_specs=[pl.BlockSpec((tm, D_MODEL), lambda i: (i, 0)), _const_spec((1, D_MODEL)),
                  _const_spec((D_MODEL, n))],
        out_specs=pl.BlockSpec((tm, n), lambda i: (i, 0)),
        out_shape=jax.ShapeDtypeStruct((t, n), F32),
        compiler_params=_params("parallel"),
        name="memkv",
    )(x, g, w)


def _in_proj_kernel(x_ref, g_ref, w_ref, u_ref, q_ref, k_ref, v_ref, qm_ref):
    z = _dot(_rms(x_ref[...], g_ref[...]).astype(BF16), w_ref[...])
    a, b = z[:, :CONV_CH], z[:, CONV_CH:W_GLU]
    u_ref[...] = a * jax.nn.sigmoid(b)
    c = W_GLU
    q_ref[...] = (z[:, c:c + W_Q] * SWA_SCALE).astype(BF16)
    c += W_Q
    k_ref[...] = z[:, c:c + W_KV]
    c += W_KV
    v_ref[...] = z[:, c:c + W_KV]
    c += W_KV
    qm_ref[...] = z[:, c:c + W_QM].astype(BF16)


def _in_proj(x, g, w, tm):
    t = x.shape[0]
    row = lambda n: pl.BlockSpec((tm, n), lambda i: (i, 0))
    return pl.pallas_call(
        _in_proj_kernel,
        grid=(t // tm,),
        in_specs=[row(D_MODEL), _const_spec((1, D_MODEL)), _const_spec((D_MODEL, W_PROJ))],
        out_specs=[row(CONV_CH), row(W_Q), row(W_KV), row(W_KV), row(W_QM)],
        out_shape=[jax.ShapeDtypeStruct((t, CONV_CH), F32), jax.ShapeDtypeStruct((t, W_Q), BF16),
                   jax.ShapeDtypeStruct((t, W_KV), F32), jax.ShapeDtypeStruct((t, W_KV), F32),
                   jax.ShapeDtypeStruct((t, W_QM), BF16)],
        compiler_params=_params("parallel"),
        name="in_proj",
    )(x, g, w)


def _ln_silu(y, g, b):
    mu = jnp.mean(y, axis=-1, keepdims=True)
    var = jnp.mean(jnp.square(y - mu), axis=-1, keepdims=True)
    y = (y - mu) * lax.rsqrt(var + EPS) * g + b
    return y * jax.nn.sigmoid(y)


def _softmax_rows(s):
    e = jnp.exp(s - jnp.max(s, axis=-1, keepdims=True))
    return e / jnp.sum(e, axis=-1, keepdims=True)


def _branches_prompt_kernel(sinks_ref, u_ref, uh_ref, q_ref, k_ref, kp_ref, v_ref, vp_ref, qm_ref,
                            mk_ref, mv_ref, bias_ref, dww_ref, dwb_ref, lng_ref, lnb_ref,
                            conv_ref, swa_ref, mem_ref, ubuf, ushift, kbuf, vbuf, *, tq):
    i = pl.program_id(1)
    first = i == 0

    ubuf[0:CONV_HALO, :] = jnp.where(first, 0.0, uh_ref[...])
    ubuf[CONV_HALO:CONV_HALO + tq, :] = u_ref[...]
    nshift = ushift.shape[1]
    for b in range(1, SUBLANES):
        ushift[b] = ubuf[b:b + nshift, :]
    rb = 64
    off = CONV_HALO - (CONV_WIDTH - 1)
    for r in range(tq // rb):
        acc = jnp.broadcast_to(dwb_ref[...], (rb, CONV_CH))
        for j in range(CONV_WIDTH):
            a, b = divmod(off + j, SUBLANES)
            start = r * rb + a * SUBLANES
            rows = ubuf[start:start + rb, :] if b == 0 else ushift[b, start:start + rb, :]
            acc = acc + rows * dww_ref[j:j + 1, :]
        conv_ref[r * rb:(r + 1) * rb, :] = _ln_silu(acc, lng_ref[...], lnb_ref[...]).astype(BF16)

    lane = lax.broadcasted_iota(jnp.int32, (WINDOW + tq, LANES), 1)
    lo = lane < SWA_HEAD_DIM
    for src_ref, prev_ref, buf in ((k_ref, kp_ref, kbuf), (v_ref, vp_ref, vbuf)):
        full = jnp.concatenate([jnp.where(first, 0.0, prev_ref[...]), src_ref[...]], axis=0)
        rolled = pltpu.roll(full, SWA_HEAD_DIM, 1)
        buf[0] = jnp.where(lo, full, 0.0).astype(BF16)
        buf[1] = jnp.where(lo, 0.0, rolled).astype(BF16)
        buf[2] = jnp.where(lo, rolled, 0.0).astype(BF16)
        buf[3] = jnp.where(lo, 0.0, full).astype(BF16)

    qi = lax.broadcasted_iota(jnp.int32, (WINDOW, 2 * WINDOW), 0)
    ki = lax.broadcasted_iota(jnp.int32, (WINDOW, 2 * WINDOW), 1)
    dist = WINDOW + qi - ki
    band = (dist >= 0) & (dist <= WINDOW)
    nqb = tq // WINDOW
    for jb in range(nqb):
        r0 = jb * WINDOW
        kmin = jnp.where(i * nqb + jb > 0, 0, WINDOW)
        mask = band & (ki >= kmin)
        for p in range(SWA_HEADS // 2):
            g = p // 2
            qp = q_ref[r0:r0 + WINDOW, LANES * p:LANES * (p + 1)]
            o = None
            for half in range(2):
                h = 2 * p + half
                s = _dot_nt(qp, kbuf[2 * g + half, r0:r0 + 2 * WINDOW, :]) + bias_ref[h]
                s = jnp.where(mask, s, NEG_INF)
                sink = sinks_ref[h]
                m = jnp.maximum(jnp.max(s, axis=-1, keepdims=True), sink)
                pr = jnp.exp(s - m)
                pr = pr / (jnp.sum(pr, axis=-1, keepdims=True) + jnp.exp(sink - m))
                t = _dot(pr.astype(BF16), vbuf[2 * g + half, r0:r0 + 2 * WINDOW, :])
                o = t if o is None else o + t
            swa_ref[r0:r0 + WINDOW, LANES * p:LANES * (p + 1)] = o.astype(BF16)

    for hm in range(MEM_HEADS):
        sl = slice(hm * MEM_HEAD_DIM, (hm + 1) * MEM_HEAD_DIM)
        w = _softmax_rows(_dot_nt(qm_ref[:, sl], mk_ref[:, sl].astype(BF16)) * MEM_SCALE)
        mem_ref[:, sl] = _dot(w.astype(BF16), mv_ref[:, sl].astype(BF16)).astype(BF16)


def _branches_prompt(sinks, u, q, k, v, qm, mkv, bias, dww, dwb, lng, lnb, batch, seq, tq):
    t = batch * seq
    nq = seq // tq
    row = lambda n: pl.BlockSpec((tq, n), lambda b, i: (b * nq + i, 0))
    halo = lambda rows, n: pl.BlockSpec(
        (rows, n), lambda b, i: (jnp.maximum(b * (seq // rows) + i * (tq // rows) - 1, 0), 0))
    return pl.pallas_call(
        functools.partial(_branches_prompt_kernel, tq=tq),
        grid=(batch, nq),
        in_specs=[pl.BlockSpec(memory_space=pltpu.SMEM),
                  row(CONV_CH), halo(CONV_HALO, CONV_CH), row(W_Q),
                  row(W_KV), halo(WINDOW, W_KV), row(W_KV), halo(WINDOW, W_KV), row(W_QM),
                  pl.BlockSpec((MEM_LEN, W_QM), lambda b, i: (b, 0)),
                  pl.BlockSpec((MEM_LEN, W_QM), lambda b, i: (b, 1)),
                  _const_spec((SWA_HEADS, WINDOW, 2 * WINDOW)),
                  _const_spec((CONV_WIDTH, CONV_CH)), _const_spec((1, CONV_CH)),
                  _const_spec((1, CONV_CH)), _const_spec((1, CONV_CH))],
        out_specs=[row(CONV_CH), row(W_Q), row(W_QM)],
        out_shape=[jax.ShapeDtypeStruct((t, CONV_CH), BF16), jax.ShapeDtypeStruct((t, W_Q), BF16),
                   jax.ShapeDtypeStruct((t, W_QM), BF16)],
        scratch_shapes=[pltpu.VMEM((CONV_HALO + tq, CONV_CH), F32),
                        pltpu.VMEM((SUBLANES, CONV_HALO + tq - SUBLANES, CONV_CH), F32),
                        pltpu.VMEM((4, WINDOW + tq, LANES), BF16),
                        pltpu.VMEM((4, WINDOW + tq, LANES), BF16)],
        compiler_params=_params("parallel", "arbitrary"),
        name="branches_prompt",
    )(sinks, u, u, q, k, k, v, v, qm, mkv, mkv, bias, dww, dwb, lng, lnb)


def _branches_sample_kernel(sinks_ref, u_ref, cc_ref, qx_ref, kn_ref, vn_ref, ck_ref, cv_ref, qm_ref,
                            cmk_ref, cmv_ref, bias_ref, bias0_ref, dww_ref, dwb_ref, lng_ref, lnb_ref,
                            conv_ref, swa_ref, mem_ref, *, nb):
    hist = CONV_WIDTH - 1
    y = jnp.sum(cc_ref[...] * dww_ref[0:hist, :][None], axis=1)
    y = y + u_ref[...] * dww_ref[hist:hist + 1, :] + dwb_ref[...]
    conv_ref[...] = _ln_silu(y, lng_ref[...], lnb_ref[...]).astype(BF16)

    sink = sinks_ref[...]
    for n in range(nb):
        qx = qx_ref[n]
        s = _dot_nt(qx, ck_ref[n].astype(BF16)) + bias_ref[...]
        kn = kn_ref[n:n + 1, :].astype(BF16).astype(F32)
        s_new = jnp.sum(qx.astype(F32) * kn, axis=-1, keepdims=True) + bias0_ref[...]
        m = jnp.maximum(jnp.maximum(jnp.max(s, axis=-1, keepdims=True), s_new), sink)
        pr, pr_new = jnp.exp(s - m), jnp.exp(s_new - m)
        den = jnp.sum(pr, axis=-1, keepdims=True) + pr_new + jnp.exp(sink - m)
        vn = vn_ref[n:n + 1, :].astype(BF16).astype(F32)
        o = _dot((pr / den).astype(BF16), cv_ref[n].astype(BF16))
        swa_ref[n] = o + (pr_new / den).astype(BF16).astype(F32) * vn

        bf = lambda a: a.astype(BF16).astype(F32)
        sm = jnp.sum(bf(cmk_ref[n]) * qm_ref[n].astype(F32)[None], axis=-1, keepdims=True) * MEM_SCALE
        e = jnp.exp(sm - jnp.max(sm, axis=0, keepdims=True))
        w = e / jnp.sum(e, axis=0, keepdims=True)
        mem_ref[n] = jnp.sum(bf(w) * bf(cmv_ref[n]), axis=0).astype(BF16)


def _branches_sample(sinks, u, cache_conv, qx, kn, vn, ck, cv, qm, cmk, cmv, bias, bias0,
                     dww, dwb, lng, lnb, nb):
    n = u.shape[0]
    row = lambda c: pl.BlockSpec((nb, c), lambda i: (i, 0))
    blk3 = lambda a, c: pl.BlockSpec((nb, a, c), lambda i: (i, 0, 0))
    return pl.pallas_call(
        functools.partial(_branches_sample_kernel, nb=nb),
        grid=(n // nb,),
        in_specs=[_const_spec((SWA_HEADS, 1)),
                  row(CONV_CH), blk3(CONV_WIDTH - 1, CONV_CH), blk3(SWA_HEADS, LANES),
                  row(W_KV), row(W_KV), blk3(WINDOW, W_KV), blk3(WINDOW, W_KV),
                  blk3(MEM_HEADS, MEM_HEAD_DIM),
                  pl.BlockSpec((nb, MEM_LEN, MEM_HEADS, MEM_HEAD_DIM), lambda i: (i, 0, 0, 0)),
                  pl.BlockSpec((nb, MEM_LEN, MEM_HEADS, MEM_HEAD_DIM), lambda i: (i, 0, 0, 0)),
                  _const_spec((SWA_HEADS, WINDOW)), _const_spec((SWA_HEADS, 1)),
                  _const_spec((CONV_WIDTH, CONV_CH)), _const_spec((1, CONV_CH)),
                  _const_spec((1, CONV_CH)), _const_spec((1, CONV_CH))],
        out_specs=[row(CONV_CH), blk3(SWA_HEADS, LANES), blk3(MEM_HEADS, MEM_HEAD_DIM)],
        out_shape=[jax.ShapeDtypeStruct((n, CONV_CH), BF16),
                   jax.ShapeDtypeStruct((n, SWA_HEADS, LANES), F32),
                   jax.ShapeDtypeStruct((n, MEM_HEADS, MEM_HEAD_DIM), BF16)],
        compiler_params=_params("parallel"),
        name="branches_sample",
    )(sinks, u, cache_conv, qx, kn, vn, ck, cv, qm, cmk, cmv, bias, bias0, dww, dwb, lng, lnb)


def _merge_kernel(x_ref, g_ref, conv_ref, swa_ref, mem_ref, wg_ref, wco_ref, wso_ref, wmo_ref, wo_ref,
                  o_ref):
    x = x_ref[...]
    h = _rms(x, g_ref[...]).astype(BF16)
    merged = None
    for br, (a_ref, w_ref) in enumerate(((conv_ref, wco_ref), (swa_ref, wso_ref), (mem_ref, wmo_ref))):
        gate = jax.nn.sigmoid(_dot(h, wg_ref[:, br * D_MODEL:(br + 1) * D_MODEL]))
        term = gate * _dot(a_ref[...], w_ref[...])
        merged = term if merged is None else merged + term
    o_ref[...] = x + _dot(merged.astype(BF16), wo_ref[...])


def _merge(x, g, conv, swa, mem, wg, wco, wso, wmo, wo, tm):
    t = x.shape[0]
    row = lambda n: pl.BlockSpec((tm, n), lambda i: (i, 0))
    return pl.pallas_call(
        _merge_kernel,
        grid=(t // tm,),
        in_specs=[row(D_MODEL), _const_spec((1, D_MODEL)), row(CONV_CH), row(W_Q), row(W_QM),
                  _const_spec((D_MODEL, 3 * D_MODEL)), _const_spec((CONV_CH, D_MODEL)),
                  _const_spec((W_Q, D_MODEL)), _const_spec((W_QM, D_MODEL)),
                  _const_spec((D_MODEL, D_MODEL))],
        out_specs=row(D_MODEL),
        out_shape=jax.ShapeDtypeStruct((t, D_MODEL), F32),
        compiler_params=_params("parallel"),
        name="merge",
    )(x, g, conv, swa, mem, wg, wco, wso, wmo, wo)


def _gelu(x):
    return 0.5 * x * (1.0 + lax.erf(x * (2.0 ** -0.5)))


def _top_values(arrs, count, with_rank=False):
    out = []
    ranks = [jnp.full(a.shape, float(count), F32) for a in arrs] if with_rank else None
    for it in range(count):
        m = jnp.max(functools.reduce(jnp.maximum, arrs), axis=0, keepdims=True)
        out.append(m)
        hit = [a == m for a in arrs]
        if with_rank:
            ranks = [jnp.where(hh, float(it), rk) for hh, rk in zip(hit, ranks)]
        arrs = [jnp.where(hh, -jnp.inf, a) for hh, a in zip(hit, arrs)]
    return (out, ranks) if with_rank else out


def _oddeven_merge_sort(lo, hi):
    def merge(lo, hi, r):
        step = 2 * r
        if step < hi - lo:
            yield from merge(lo, hi, step)
            yield from merge(lo + r, hi, step)
            yield from ((i, i + r) for i in range(lo + r, hi - r, step))
        else:
            yield (lo, lo + r)

    if hi > lo:
        mid = lo + (hi - lo) // 2
        yield from _oddeven_merge_sort(lo, mid)
        yield from _oddeven_merge_sort(mid + 1, hi)
        yield from merge(lo, hi, 1)


def _compare_exchange(a, i, j):
    if a[j] is None:
        return
    if a[i] is None:
        a[i], a[j] = a[j], None
    else:
        a[i], a[j] = jnp.maximum(a[i], a[j]), jnp.minimum(a[i], a[j])


def _top16_sorted(blocks):
    n = PEER_TOPK
    assert n // 2 < len(blocks) <= n and blocks[0].shape[0] == SUBLANES
    a = list(blocks) + [None] * (n - len(blocks))
    larger = lambda x, y: x if y is None else y if x is None else jnp.maximum(x, y)
    for i, j in _oddeven_merge_sort(0, n - 1):
        _compare_exchange(a, i, j)
    for shift in (4, 2, 1):
        b = [None if x is None else pltpu.roll(x, shift, 0) for x in a]
        a = [larger(a[i], b[n - 1 - i]) for i in range(n)]
        d = n // 2
        while d:
            for i in range(n):
                if not i & d:
                    _compare_exchange(a, i, i + d)
            d //= 2
    return a


def _rank_in_sorted(x, top):
    assert len(top) == 16
    rank, lo = None, [0] * 1
    conds = []
    for width in (8, 4, 2, 1):
        idx = [b + width - 1 for b in lo]
        thr = [top[i] for i in idx]
        for c in reversed(conds):
            thr = [jnp.where(c, thr[2 * k], thr[2 * k + 1]) for k in range(len(thr) // 2)]
        c = x >= thr[0]
        step = jnp.where(c, 0.0, float(width))
        rank = step if rank is None else rank + step
        conds.append(c)
        lo = [b + off for b in lo for off in (0, width)]
    return jnp.where(x >= top[15], rank, 16.0)


def _peer_route(h, qt, keys_ref, l_s, e0_s, r1_s, e1_s, sv0_s, sv1_s):
    nk, tb = PEER_N_KEYS, qt.shape[1]
    r = 2 * h * PEER_DK_HALF
    for lg in range(tb // LANES):
        lanes = slice(lg * LANES, (lg + 1) * LANES)
        blocks = lambda a: [a[SUBLANES * j:SUBLANES * (j + 1), :] for j in range(nk // SUBLANES)]
        s0 = blocks(_dot(keys_ref[h, 0], qt[r:r + PEER_DK_HALF, lanes]))
        s1 = blocks(_dot(keys_ref[h, 1], qt[r + PEER_DK_HALF:r + 2 * PEER_DK_HALF, lanes]))
        top0, top1 = _top16_sorted(s0), _top16_sorted(s1)
        for j in range(PEER_TOPK):
            sv0_s[j:j + 1, lanes] = top0[j][0:1]
            sv1_s[j:j + 1, lanes] = top1[j][0:1]
        a0, a1 = sv0_s[0:8, lanes], sv0_s[8:16, lanes]
        b0, b1 = sv1_s[0:8, lanes], sv1_s[8:16, lanes]
        cands = [a0[0:1] + b0, a0[0:1] + b1] + [a0[a:a + 1] + b0 for a in range(1, 8)] + [a1 + b0[0:1]]
        best = _top16_sorted(cands)
        sel = [cd >= best[PEER_TOPK - 1] for cd in cands]
        z = functools.reduce(jnp.add, [jnp.where(sl, jnp.exp(cd - best[0]), 0.0) for sl, cd in zip(sel, cands)])
        z = jnp.sum(z, axis=0, keepdims=True)
        cnt = [jnp.sum(jnp.where(sl, 1.0, 0.0), axis=0, keepdims=True) for sl in sel[:9]]
        counts = [cnt[0] + cnt[1]] + cnt[2:9]
        tail = jnp.where(sel[9], 1.0, 0.0)
        counts = [jnp.broadcast_to(counts[a] if a < 8 else tail[a - 8:a - 7], (SUBLANES, LANES))
                  for a in range(PEER_TOPK)]
        lrow, rank1 = [], []
        for x0, x1 in zip(s0, s1):
            lx = jnp.zeros_like(x0)
            for a in reversed(range(PEER_TOPK)):
                lx = jnp.where(x0 >= top0[a], counts[a], lx)
            lrow.append(lx)
            rank1.append(_rank_in_sorted(x1, top1))
        l_s[h, :, lanes] = jnp.concatenate(lrow, axis=0)
        e0_s[h, :, lanes] = jnp.exp(jnp.concatenate(s0, axis=0) - top0[0][0:1]) / z
        r1_s[h, :, lanes] = jnp.concatenate(rank1, axis=0).astype(BF16)
        e1_s[h, :, lanes] = jnp.exp(jnp.concatenate(s1, axis=0) - top1[0][0:1]).astype(BF16)


def _peer_gate_chunk(ci, at_ref, ct_ref, lrow_s, erow_s, r1_s, e1_s, cb):
    nk, tb = PEER_N_KEYS, ct_ref.shape[1]
    tile = (nk // BF16_ROWS, BF16_ROWS, tb)
    row = lambda ref, h, j: jnp.broadcast_to(ref[ci, h, j:j + 1, :], tile[1:]).astype(BF16)[None]
    for j in range(cb):
        g = None
        for h in range(PEER_HEADS):
            term = jnp.where(r1_s[h].reshape(tile) < row(lrow_s, h, j), e1_s[h].reshape(tile),
                             jnp.zeros((), BF16)) * row(erow_s, h, j)
            g = term if g is None else g + term
        rows = slice((ci * cb + j) * nk, (ci * cb + j + 1) * nk)
        ct_ref[rows, :] = (_gelu(at_ref[j * nk:(j + 1) * nk, :].reshape(tile)) * g).reshape(nk, tb)


def _peer_kernel(x_ref, g2_ref, gf_ref, wqt_ref, keys_ref, wda_ref, wdb_ref, wut_ref, o_ref,
                 h_s, l_s, e0_s, r1_s, e1_s, sv0_s, sv1_s, lrow_s, erow_s, at0_s, at1_s, ct_s, acc_s, *, cb):
    s = pl.program_id(1)

    @pl.when(s == 0)
    def _first():
        h_s[...] = _rms(x_ref[...], g2_ref[...]).T.astype(BF16)
        qt = _dot(wqt_ref[...], h_s[...]).astype(BF16)
        for h in range(PEER_HEADS):
            _peer_route(h, qt, keys_ref, l_s, e0_s, r1_s, e1_s, sv0_s, sv1_s)
        at0_s[...] = _dot(wda_ref[...], h_s[...]).astype(BF16)
        acc_s[...] = jnp.zeros_like(acc_s)

    @pl.when(s > 0)
    def _steady():
        for ci in range(2):
            base = pl.multiple_of((2 * s - 2 + ci) * cb, cb)
            for h in range(PEER_HEADS):
                lrow_s[ci, h] = l_s[h, pl.ds(base, cb), :]
                erow_s[ci, h] = e0_s[h, pl.ds(base, cb), :]
        gate = functools.partial(_peer_gate_chunk, ct_ref=ct_s, lrow_s=lrow_s, erow_s=erow_s,
                                 r1_s=r1_s, e1_s=e1_s, cb=cb)
        gate(0, at0_s)
        at1_s[...] = _dot(wda_ref[...], h_s[...]).astype(BF16)
        at0_s[...] = _dot(wdb_ref[...], h_s[...]).astype(BF16)
        gate(1, at1_s)
        acc_s[...] += _dot(wut_ref[...], ct_s[...])

    @pl.when(s == pl.num_programs(1) - 1)
    def _last():
        y = x_ref[...] + acc_s[...].T
        o_ref[...] = _rms(y, gf_ref[...])


def _peer(x, g2, gf, wqt, keys, wd, wut, tb, cb):
    t = x.shape[0]
    ec = cb * PEER_N_KEYS
    nch = wd.shape[0] // ec
    assert nch % 2 == 0 and tb % LANES == 0
    stat = lambda dt: pltpu.VMEM((PEER_HEADS, PEER_N_KEYS, tb), dt)
    return pl.pallas_call(
        functools.partial(_peer_kernel, cb=cb),
        grid=(t // tb, nch // 2 + 1),
        in_specs=[pl.BlockSpec((tb, D_MODEL), lambda i, s: (i, 0)),
                  _const_spec((1, D_MODEL)), _const_spec((1, D_MODEL)),
                  _const_spec((D_MODEL, D_MODEL)),
                  _const_spec((PEER_HEADS, 2, PEER_N_KEYS, PEER_DK_HALF)),
                  pl.BlockSpec((ec, D_MODEL), lambda i, s: (jnp.maximum(2 * s - 1, 0), 0)),
                  pl.BlockSpec((ec, D_MODEL), lambda i, s: (jnp.minimum(2 * s, nch - 1), 0)),
                  pl.BlockSpec((D_MODEL, 2 * ec), lambda i, s: (0, jnp.maximum(s - 1, 0)))],
        out_specs=pl.BlockSpec((tb, D_MODEL), lambda i, s: (i, 0)),
        out_shape=jax.ShapeDtypeStruct((t, D_MODEL), F32),
        scratch_shapes=[pltpu.VMEM((D_MODEL, tb), BF16), stat(F32), stat(F32), stat(BF16), stat(BF16),
                        pltpu.VMEM((PEER_TOPK, tb), F32), pltpu.VMEM((PEER_TOPK, tb), F32),
                        pltpu.VMEM((2, PEER_HEADS, cb, tb), F32), pltpu.VMEM((2, PEER_HEADS, cb, tb), F32),
                        pltpu.VMEM((ec, tb), BF16), pltpu.VMEM((ec, tb), BF16),
                        pltpu.VMEM((2 * ec, tb), BF16), pltpu.VMEM((D_MODEL, tb), F32)],
        compiler_params=_params("parallel", "arbitrary"),
        name="peer",
    )(x, g2, gf, wqt, keys, wd, wd, wut)


def _rel_bucket(dist):
    n = jnp.maximum(dist, 0)
    max_exact = REL_BUCKETS // 2
    nf = jnp.maximum(n, 1).astype(F32)
    large = max_exact + (jnp.log(nf / max_exact) / jnp.log(REL_MAX_DIST / max_exact)
                         * (REL_BUCKETS - max_exact)).astype(jnp.int32)
    return jnp.where(n < max_exact, n, jnp.minimum(large, REL_BUCKETS - 1))


def _tile(t, cap):
    tm = min(t, cap)
    assert t % tm == 0, (t, tm)
    return tm


def kernel(x_prompt, x_sample, cache_conv, cache_swa_k, cache_swa_v, cache_mem_k, cache_mem_v, mem_prompt, rel_bias_table, norm1_g, w_in, conv_dw_w, conv_dw_b, conv_ln_g, conv_ln_b, w_conv_out, swa_sinks, w_swa_out, mem_norm_g, w_mem_kv, w_mem_out, w_out, norm2_g, peer_w_q, peer_keys, peer_w_down, peer_w_up, final_norm_g):
    assert w_in.shape[0] == 1, "single layer"
    batch, seq, _ = x_prompt.shape
    nsamp = x_sample.shape[0]
    assert x_sample.shape[1] == 1 and seq % WINDOW == 0
    row = lambda a: a.reshape(1, -1)

    w_proj = w_in[0, :, :W_PROJ].astype(BF16)
    w_gate = w_in[0, :, W_PROJ:].astype(BF16)
    g1, g2, gf = row(norm1_g[0]), row(norm2_g[0]), row(final_norm_g)
    dww, dwb = conv_dw_w[0], row(conv_dw_b[0])
    lng, lnb = row(conv_ln_g[0]), row(conv_ln_b[0])
    wco, wso = w_conv_out[0].astype(BF16), w_swa_out[0].astype(BF16)
    wmo, wo = w_mem_out[0].astype(BF16), w_out[0].astype(BF16)
    wqt = peer_w_q[0].T.astype(BF16)
    keys = peer_keys[0].astype(BF16)
    wd = peer_w_down.reshape(-1, D_MODEL).astype(BF16)
    wut = peer_w_up.reshape(-1, D_MODEL).astype(BF16).T
    sinks = swa_sinks[0]

    qi = jnp.arange(WINDOW)[:, None]
    ki = jnp.arange(2 * WINDOW)[None, :]
    def table_rows(dist):
        onehot = (_rel_bucket(dist)[..., None] == jnp.arange(REL_BUCKETS)).astype(F32)
        return jnp.einsum("...b,bh->h...", onehot, rel_bias_table.astype(F32),
                          precision=lax.Precision.HIGHEST)

    bias_p = table_rows(WINDOW + qi - ki)
    bias_s = table_rows(WINDOW - jnp.arange(WINDOW))
    bias_0 = table_rows(jnp.zeros((1,), jnp.int32))

    xp = x_prompt.reshape(batch * seq, D_MODEL)
    mkv = _norm_matmul(mem_prompt.reshape(batch * MEM_LEN, D_MODEL), row(mem_norm_g[0]),
                       w_mem_kv[0].astype(BF16), MEM_LEN)
    u_p, q_p, k_p, v_p, qm_p = _in_proj(xp, g1, w_proj, _tile(batch * seq, 512))
    tq = _tile(seq, 512)
    conv_p, swa_p, mem_p = _branches_prompt(sinks, u_p, q_p, k_p, v_p, qm_p, mkv, bias_p,
                                            dww, dwb, lng, lnb, batch, seq, tq)
    x2_p = _merge(xp, g1, conv_p, swa_p, mem_p, w_gate, wco, wso, wmo, wo, _tile(batch * seq, 512))
    y_p = _peer(x2_p, g2, gf, wqt, keys, wd, wut, _tile(batch * seq, 512), 8)

    xs = x_sample.reshape(nsamp, D_MODEL)
    u_s, q_s, k_s, v_s, qm_s = _in_proj(xs, g1, w_proj, _tile(nsamp, 128))
    q4 = q_s.reshape(nsamp, SWA_KV_HEADS, SWA_HEADS // SWA_KV_HEADS, SWA_HEAD_DIM)
    zq = jnp.zeros_like(q4[:, 0])
    qx = jnp.concatenate([jnp.concatenate([q4[:, 0], zq], -1), jnp.concatenate([zq, q4[:, 1]], -1)], 1)
    ck = cache_swa_k.reshape(nsamp, WINDOW, W_KV)
    cv = cache_swa_v.reshape(nsamp, WINDOW, W_KV)
    cmk = cache_mem_k.reshape(nsamp, MEM_LEN, MEM_HEADS, MEM_HEAD_DIM)
    cmv = cache_mem_v.reshape(nsamp, MEM_LEN, MEM_HEADS, MEM_HEAD_DIM)
    cconv = cache_conv.reshape(nsamp, CONV_WIDTH - 1, CONV_CH)
    conv_s, swa_x, mem_x = _branches_sample(sinks.reshape(SWA_HEADS, 1), u_s, cconv, qx, k_s, v_s, ck, cv,
                                            qm_s.reshape(nsamp, MEM_HEADS, MEM_HEAD_DIM), cmk, cmv,
                                            bias_s, bias_0, dww, dwb, lng, lnb, _tile(nsamp, 8))
    mem_s = mem_x.reshape(nsamp, W_QM)
    sx = swa_x.reshape(nsamp, SWA_KV_HEADS, SWA_HEADS // SWA_KV_HEADS, SWA_KV_HEADS, SWA_HEAD_DIM)
    swa_s = jnp.stack([sx[:, g, :, g] for g in range(SWA_KV_HEADS)], 1).reshape(nsamp, W_Q).astype(BF16)
    x2_s = _merge(xs, g1, conv_s, swa_s, mem_s, w_gate, wco, wso, wmo, wo, _tile(nsamp, 128))
    y_s = _peer(x2_s, g2, gf, wqt, keys, wd, wut, _tile(nsamp, 128), 8)

    hist = CONV_WIDTH - 1
    kv_shape = (SWA_KV_HEADS, SWA_HEAD_DIM)
    mkv5 = mkv.reshape(batch, MEM_LEN, 2, MEM_HEADS, MEM_HEAD_DIM)
    conv_state_p = u_p.reshape(batch, seq, CONV_CH)[:, -hist:]
    swa_k_p = k_p.reshape(batch, seq, *kv_shape)[:, -WINDOW:]
    swa_v_p = v_p.reshape(batch, seq, *kv_shape)[:, -WINDOW:]
    conv_state_s = jnp.concatenate([cache_conv[0][:, 1:], u_s[:, None, :]], axis=1)
    swa_k_s = jnp.concatenate([cache_swa_k[0][:, 1:], k_s.reshape(nsamp, 1, *kv_shape)], axis=1)
    swa_v_s = jnp.concatenate([cache_swa_v[0][:, 1:], v_s.reshape(nsamp, 1, *kv_shape)], axis=1)
    return (y_p.reshape(batch, seq, D_MODEL), y_s.reshape(nsamp, 1, D_MODEL),
            conv_state_p[None], swa_k_p[None], swa_v_p[None],
            mkv5[:, :, 0][None], mkv5[:, :, 1][None],
            conv_state_s[None], swa_k_s[None], swa_v_s[None])
```

```python
import functools

import jax
import jax.numpy as jnp
from jax import lax
from jax.experimental import pallas as pl
from jax.experimental.pallas import tpu as pltpu

F32 = jnp.float32
BF16 = jnp.bfloat16

D_MODEL = 1024
PAST_LEN = 16384
MEM_LEN = 256
CONV_CH = 512
CONV_WIDTH = 31
SWA_HEADS = 8
SWA_KV_HEADS = 2
SWA_HEAD_DIM = 64
WINDOW = 128
SWA_SCALE = SWA_HEAD_DIM ** -0.5
MEM_HEADS = 4
MEM_HEAD_DIM = 128
MEM_SCALE = MEM_HEAD_DIM ** -0.5
REL_BUCKETS = 32
REL_MAX_DIST = 128
PEER_HEADS = 8
PEER_N_KEYS = 128
PEER_DK_HALF = 64
PEER_TOPK = 16
EPS = 1e-6
NEG_INF = -1e30

W_GLU = 2 * CONV_CH
W_Q = SWA_HEADS * SWA_HEAD_DIM
W_KV = SWA_KV_HEADS * SWA_HEAD_DIM
W_QM = MEM_HEADS * MEM_HEAD_DIM
W_PROJ = W_GLU + W_Q + 2 * W_KV + W_QM

VMEM_LIMIT_BYTES = 56 * 1024 * 1024
LANES = 128
SUBLANES = 8
BF16_ROWS = 16
CONV_HALO = 32


def _params(*sem, flags=None):
    return pltpu.CompilerParams(dimension_semantics=sem, vmem_limit_bytes=VMEM_LIMIT_BYTES, flags=flags)


def _rms(x, g):
    return x * lax.rsqrt(jnp.mean(x * x, axis=-1, keepdims=True) + EPS) * g


def _dot(a, b):
    return jnp.dot(a, b, preferred_element_type=F32)


def _dot_nt(a, b):
    return lax.dot_general(a, b, (((1,), (1,)), ((), ())), preferred_element_type=F32)


def _const_spec(shape):
    zeros = (0,) * len(shape)
    return pl.BlockSpec(shape, lambda *_: zeros)


def _norm_matmul_kernel(x_ref, g_ref, w_ref, o_ref):
    o_ref[...] = _dot(_rms(x_ref[...], g_ref[...]).astype(BF16), w_ref[...])


def _norm_matmul(x, g, w, tm):
    t, n = x.shape[0], w.shape[1]
    return pl.pallas_call(
        _norm_matmul_kernel,
        grid=(t // tm,),
        in_specs=[pl.BlockSpec((tm, D_MODEL), lambda i: (i, 0)), _const_spec((1, D_MODEL)),
                  _const_spec((D_MODEL, n))],
        out_specs=pl.BlockSpec((tm, n), lambda i: (i, 0)),
        out_shape=jax.ShapeDtypeStruct((t, n), F32),
        compiler_params=_params("parallel"),
        name="memkv",
    )(x, g, w)


def _in_proj_kernel(x_ref, g_ref, w_ref, u_ref, q_ref, k_ref, v_ref, qm_ref):
    z = _dot(_rms(x_ref[...], g_ref[...]).astype(BF16), w_ref[...])
    a, b = z[:, :CONV_CH], z[:, CONV_CH:W_GLU]
    u_ref[...] = a * jax.nn.sigmoid(b)
    c = W_GLU
    q_ref[...] = (z[:, c:c + W_Q] * SWA_SCALE).astype(BF16)
    c += W_Q
    k_ref[...] = z[:, c:c + W_KV]
    c += W_KV
    v_ref[...] = z[:, c:c + W_KV]
    c += W_KV
    qm_ref[...] = z[:, c:c + W_QM].astype(BF16)


def _in_proj(x, g, w, tm):
    t = x.shape[0]
    row = lambda n: pl.BlockSpec((tm, n), lambda i: (i, 0))
    return pl.pallas_call(
        _in_proj_kernel,
        grid=(t // tm,),
        in_specs=[row(D_MODEL), _const_spec((1, D_MODEL)), _const_spec((D_MODEL, W_PROJ))],
        out_specs=[row(CONV_CH), row(W_Q), row(W_KV), row(W_KV), row(W_QM)],
        out_shape=[jax.ShapeDtypeStruct((t, CONV_CH), F32), jax.ShapeDtypeStruct((t, W_Q), BF16),
                   jax.ShapeDtypeStruct((t, W_KV), F32), jax.ShapeDtypeStruct((t, W_KV), F32),
                   jax.ShapeDtypeStruct((t, W_QM), BF16)],
        compiler_params=_params("parallel"),
        name="in_proj",
    )(x, g, w)


def _ln_silu(y, g, b):
    mu = jnp.mean(y, axis=-1, keepdims=True)
    var = jnp.mean(jnp.square(y - mu), axis=-1, keepdims=True)
    y = (y - mu) * lax.rsqrt(var + EPS) * g + b
    return y * jax.nn.sigmoid(y)


def _softmax_rows(s):
    e = jnp.exp(s - jnp.max(s, axis=-1, keepdims=True))
    return e / jnp.sum(e, axis=-1, keepdims=True)


def _branches_prompt_kernel(sinks_ref, u_ref, uh_ref, q_ref, k_ref, kp_ref, v_ref, vp_ref, qm_ref,
                            mk_ref, mv_ref, bias_ref, dww_ref, dwb_ref, lng_ref, lnb_ref,
                            conv_ref, swa_ref, mem_ref, ubuf, ushift, kbuf, vbuf, *, tq):
    i = pl.program_id(1)
    first = i == 0

    ubuf[0:CONV_HALO, :] = jnp.where(first, 0.0, uh_ref[...])
    ubuf[CONV_HALO:CONV_HALO + tq, :] = u_ref[...]
    nshift = ushift.shape[1]
    for b in range(1, SUBLANES):
        ushift[b] = ubuf[b:b + nshift, :]
    rb = 64
    off = CONV_HALO - (CONV_WIDTH - 1)
    for r in range(tq // rb):
        acc = jnp.broadcast_to(dwb_ref[...], (rb, CONV_CH))
        for j in range(CONV_WIDTH):
            a, b = divmod(off + j, SUBLANES)
            start = r * rb + a * SUBLANES
            rows = ubuf[start:start + rb, :] if b == 0 else ushift[b, start:start + rb, :]
            acc = acc + rows * dww_ref[j:j + 1, :]
        conv_ref[r * rb:(r + 1) * rb, :] = _ln_silu(acc, lng_ref[...], lnb_ref[...]).astype(BF16)

    lane = lax.broadcasted_iota(jnp.int32, (WINDOW + tq, LANES), 1)
    lo = lane < SWA_HEAD_DIM
    for src_ref, prev_ref, buf in ((k_ref, kp_ref, kbuf), (v_ref, vp_ref, vbuf)):
        full = jnp.concatenate([jnp.where(first, 0.0, prev_ref[...]), src_ref[...]], axis=0)
        rolled = pltpu.roll(full, SWA_HEAD_DIM, 1)
        buf[0] = jnp.where(lo, full, 0.0).astype(BF16)
        buf[1] = jnp.where(lo, 0.0, rolled).astype(BF16)
        buf[2] = jnp.where(lo, rolled, 0.0).astype(BF16)
        buf[3] = jnp.where(lo, 0.0, full).astype(BF16)

    qi = lax.broadcasted_iota(jnp.int32, (WINDOW, 2 * WINDOW), 0)
    ki = lax.broadcasted_iota(jnp.int32, (WINDOW, 2 * WINDOW), 1)
    dist = WINDOW + qi - ki
    band = (dist >= 0) & (dist <= WINDOW)
    nqb = tq // WINDOW
    for jb in range(nqb):
        r0 = jb * WINDOW
        kmin = jnp.where(i * nqb + jb > 0, 0, WINDOW)
        mask = band & (ki >= kmin)
        for p in range(SWA_HEADS // 2):
            g = p // 2
            qp = q_ref[r0:r0 + WINDOW, LANES * p:LANES * (p + 1)]
            o = None
            for half in range(2):
                h = 2 * p + half
                s = _dot_nt(qp, kbuf[2 * g + half, r0:r0 + 2 * WINDOW, :]) + bias_ref[h]
                s = jnp.where(mask, s, NEG_INF)
                sink = sinks_ref[h]
                m = jnp.maximum(jnp.max(s, axis=-1, keepdims=True), sink)
                pr = jnp.exp(s - m)
                pr = pr / (jnp.sum(pr, axis=-1, keepdims=True) + jnp.exp(sink - m))
                t = _dot(pr.astype(BF16), vbuf[2 * g + half, r0:r0 + 2 * WINDOW, :])
                o = t if o is None else o + t
            swa_ref[r0:r0 + WINDOW, LANES * p:LANES * (p + 1)] = o.astype(BF16)

    for hm in range(MEM_HEADS):
        sl = slice(hm * MEM_HEAD_DIM, (hm + 1) * MEM_HEAD_DIM)
        w = _softmax_rows(_dot_nt(qm_ref[:, sl], mk_ref[:, sl].astype(BF16)) * MEM_SCALE)
        mem_ref[:, sl] = _dot(w.astype(BF16), mv_ref[:, sl].astype(BF16)).astype(BF16)


def _branches_prompt(sinks, u, q, k, v, qm, mkv, bias, dww, dwb, lng, lnb, batch, seq, tq):
    t = batch * seq
    nq = seq // tq
    row = lambda n: pl.BlockSpec((tq, n), lambda b, i: (b * nq + i, 0))
    halo = lambda rows, n: pl.BlockSpec(
        (rows, n), lambda b, i: (jnp.maximum(b * (seq // rows) + i * (tq // rows) - 1, 0), 0))
    return pl.pallas_call(
        functools.partial(_branches_prompt_kernel, tq=tq),
        grid=(batch, nq),
        in_specs=[pl.BlockSpec(memory_space=pltpu.SMEM),
                  row(CONV_CH), halo(CONV_HALO, CONV_CH), row(W_Q),
                  row(W_KV), halo(WINDOW, W_KV), row(W_KV), halo(WINDOW, W_KV), row(W_QM),
                  pl.BlockSpec((MEM_LEN, W_QM), lambda b, i: (b, 0)),
                  pl.BlockSpec((MEM_LEN, W_QM), lambda b, i: (b, 1)),
                  _const_spec((SWA_HEADS, WINDOW, 2 * WINDOW)),
                  _const_spec((CONV_WIDTH, CONV_CH)), _const_spec((1, CONV_CH)),
                  _const_spec((1, CONV_CH)), _const_spec((1, CONV_CH))],
        out_specs=[row(CONV_CH), row(W_Q), row(W_QM)],
        out_shape=[jax.ShapeDtypeStruct((t, CONV_CH), BF16), jax.ShapeDtypeStruct((t, W_Q), BF16),
                   jax.ShapeDtypeStruct((t, W_QM), BF16)],
        scratch_shapes=[pltpu.VMEM((CONV_HALO + tq, CONV_CH), F32),
                        pltpu.VMEM((SUBLANES, CONV_HALO + tq - SUBLANES, CONV_CH), F32),
                        pltpu.VMEM((4, WINDOW + tq, LANES), BF16),
                        pltpu.VMEM((4, WINDOW + tq, LANES), BF16)],
        compiler_params=_params("parallel", "arbitrary"),
        name="branches_prompt",
    )(sinks, u, u, q, k, k, v, v, qm, mkv, mkv, bias, dww, dwb, lng, lnb)


def _branches_sample_kernel(sinks_ref, u_ref, cc_ref, qx_ref, kn_ref, vn_ref, ck_ref, cv_ref, qm_ref,
                            cmk_ref, cmv_ref, bias_ref, bias0_ref, dww_ref, dwb_ref, lng_ref, lnb_ref,
                            conv_ref, swa_ref, mem_ref, *, nb):
    hist = CONV_WIDTH - 1
    y = jnp.sum(cc_ref[...] * dww_ref[0:hist, :][None], axis=1)
    y = y + u_ref[...] * dww_ref[hist:hist + 1, :] + dwb_ref[...]
    conv_ref[...] = _ln_silu(y, lng_ref[...], lnb_ref[...]).astype(BF16)

    sink = sinks_ref[...]
    for n in range(nb):
        qx = qx_ref[n]
        s = _dot_nt(qx, ck_ref[n].astype(BF16)) + bias_ref[...]
        kn = kn_ref[n:n + 1, :].astype(BF16).astype(F32)
        s_new = jnp.sum(qx.astype(F32) * kn, axis=-1, keepdims=True) + bias0_ref[...]
        m = jnp.maximum(jnp.maximum(jnp.max(s, axis=-1, keepdims=True), s_new), sink)
        pr, pr_new = jnp.exp(s - m), jnp.exp(s_new - m)
        den = jnp.sum(pr, axis=-1, keepdims=True) + pr_new + jnp.exp(sink - m)
        vn = vn_ref[n:n + 1, :].astype(BF16).astype(F32)
        o = _dot((pr / den).astype(BF16), cv_ref[n].astype(BF16))
        swa_ref[n] = o + (pr_new / den).astype(BF16).astype(F32) * vn

        bf = lambda a: a.astype(BF16).astype(F32)
        sm = jnp.sum(bf(cmk_ref[n]) * qm_ref[n].astype(F32)[None], axis=-1, keepdims=True) * MEM_SCALE
        e = jnp.exp(sm - jnp.max(sm, axis=0, keepdims=True))
        w = e / jnp.sum(e, axis=0, keepdims=True)
        mem_ref[n] = jnp.sum(bf(w) * bf(cmv_ref[n]), axis=0).astype(BF16)


def _branches_sample(sinks, u, cache_conv, qx, kn, vn, ck, cv, qm, cmk, cmv, bias, bias0,
                     dww, dwb, lng, lnb, nb):
    n = u.shape[0]
    row = lambda c: pl.BlockSpec((nb, c), lambda i: (i, 0))
    blk3 = lambda a, c: pl.BlockSpec((nb, a, c), lambda i: (i, 0, 0))
    return pl.pallas_call(
        functools.partial(_branches_sample_kernel, nb=nb),
        grid=(n // nb,),
        in_specs=[_const_spec((SWA_HEADS, 1)),
                  row(CONV_CH), blk3(CONV_WIDTH - 1, CONV_CH), blk3(SWA_HEADS, LANES),
                  row(W_KV), row(W_KV), blk3(WINDOW, W_KV), blk3(WINDOW, W_KV),
                  blk3(MEM_HEADS, MEM_HEAD_DIM),
                  pl.BlockSpec((nb, MEM_LEN, MEM_HEADS, MEM_HEAD_DIM), lambda i: (i, 0, 0, 0)),
                  pl.BlockSpec((nb, MEM_LEN, MEM_HEADS, MEM_HEAD_DIM), lambda i: (i, 0, 0, 0)),
                  _const_spec((SWA_HEADS, WINDOW)), _const_spec((SWA_HEADS, 1)),
                  _const_spec((CONV_WIDTH, CONV_CH)), _const_spec((1, CONV_CH)),
                  _const_spec((1, CONV_CH)), _const_spec((1, CONV_CH))],
        out_specs=[row(CONV_CH), blk3(SWA_HEADS, LANES), blk3(MEM_HEADS, MEM_HEAD_DIM)],
        out_shape=[jax.ShapeDtypeStruct((n, CONV_CH), BF16),
                   jax.ShapeDtypeStruct((n, SWA_HEADS, LANES), F32),
                   jax.ShapeDtypeStruct((n, MEM_HEADS, MEM_HEAD_DIM), BF16)],
        compiler_params=_params("parallel"),
        name="branches_sample",
    )(sinks, u, cache_conv, qx, kn, vn, ck, cv, qm, cmk, cmv, bias, bias0, dww, dwb, lng, lnb)


def _merge_kernel(x_ref, g_ref, conv_ref, swa_ref, mem_ref, wg_ref, wco_ref, wso_ref, wmo_ref, wo_ref,
                  o_ref):
    x = x_ref[...]
    h = _rms(x, g_ref[...]).astype(BF16)
    merged = None
    for br, (a_ref, w_ref) in enumerate(((conv_ref, wco_ref), (swa_ref, wso_ref), (mem_ref, wmo_ref))):
        gate = jax.nn.sigmoid(_dot(h, wg_ref[:, br * D_MODEL:(br + 1) * D_MODEL]))
        term = gate * _dot(a_ref[...], w_ref[...])
        merged = term if merged is None else merged + term
    o_ref[...] = x + _dot(merged.astype(BF16), wo_ref[...])


def _merge(x, g, conv, swa, mem, wg, wco, wso, wmo, wo, tm):
    t = x.shape[0]
    row = lambda n: pl.BlockSpec((tm, n), lambda i: (i, 0))
    return pl.pallas_call(
        _merge_kernel,
        grid=(t // tm,),
        in_specs=[row(D_MODEL), _const_spec((1, D_MODEL)), row(CONV_CH), row(W_Q), row(W_QM),
                  _const_spec((D_MODEL, 3 * D_MODEL)), _const_spec((CONV_CH, D_MODEL)),
                  _const_spec((W_Q, D_MODEL)), _const_spec((W_QM, D_MODEL)),
                  _const_spec((D_MODEL, D_MODEL))],
        out_specs=row(D_MODEL),
        out_shape=jax.ShapeDtypeStruct((t, D_MODEL), F32),
        compiler_params=_params("parallel"),
        name="merge",
    )(x, g, conv, swa, mem, wg, wco, wso, wmo, wo)


def _gelu(x):
    return 0.5 * x * (1.0 + lax.erf(x * (2.0 ** -0.5)))


def _top_values(arrs, count, with_rank=False):
    out = []
    ranks = [jnp.full(a.shape, float(count), F32) for a in arrs] if with_rank else None
    for it in range(count):
        m = jnp.max(functools.reduce(jnp.maximum, arrs), axis=0, keepdims=True)
        out.append(m)
        hit = [a == m for a in arrs]
        if with_rank:
            ranks = [jnp.where(hh, float(it), rk) for hh, rk in zip(hit, ranks)]
        arrs = [jnp.where(hh, -jnp.inf, a) for hh, a in zip(hit, arrs)]
    return (out, ranks) if with_rank else out


def _oddeven_merge_sort(lo, hi):
    def merge(lo, hi, r):
        step = 2 * r
        if step < hi - lo:
            yield from merge(lo, hi, step)
            yield from merge(lo + r, hi, step)
            yield from ((i, i + r) for i in range(lo + r, hi - r, step))
        else:
            yield (lo, lo + r)

    if hi > lo:
        mid = lo + (hi - lo) // 2
        yield from _oddeven_merge_sort(lo, mid)
        yield from _oddeven_merge_sort(mid + 1, hi)
        yield from merge(lo, hi, 1)


def _compare_exchange(a, i, j):
    if a[j] is None:
        return
    if a[i] is None:
        a[i], a[j] = a[j], None
    else:
        a[i], a[j] = jnp.maximum(a[i], a[j]), jnp.minimum(a[i], a[j])


def _top16_sorted(blocks):
    n = PEER_TOPK
    assert n // 2 < len(blocks) <= n and blocks[0].shape[0] == SUBLANES
    a = list(blocks) + [None] * (n - len(blocks))
    larger = lambda x, y: x if y is None else y if x is None else jnp.maximum(x, y)
    for i, j in _oddeven_merge_sort(0, n - 1):
        _compare_exchange(a, i, j)
    for shift in (4, 2, 1):
        b = [None if x is None else pltpu.roll(x, shift, 0) for x in a]
        a = [larger(a[i], b[n - 1 - i]) for i in range(n)]
        d = n // 2
        while d:
            for i in range(n):
                if not i & d:
                    _compare_exchange(a, i, i + d)
            d //= 2
    return a


def _rank_in_sorted(x, top):
    assert len(top) == 16
    rank, lo = None, [0] * 1
    conds = []
    for width in (8, 4, 2, 1):
        idx = [b + width - 1 for b in lo]
        thr = [top[i] for i in idx]
        for c in reversed(conds):
            thr = [jnp.where(c, thr[2 * k], thr[2 * k + 1]) for k in range(len(thr) // 2)]
        c = x >= thr[0]
        step = jnp.where(c, 0.0, float(width))
        rank = step if rank is None else rank + step
        conds.append(c)
        lo = [b + off for b in lo for off in (0, width)]
    return jnp.where(x >= top[15], rank, 16.0)


def _peer_route(h, qt, keys_ref, l_s, e0_s, r1_s, e1_s, sv0_s, sv1_s):
    nk, tb = PEER_N_KEYS, qt.shape[1]
    r = 2 * h * PEER_DK_HALF
    for lg in range(tb // LANES):
        lanes = slice(lg * LANES, (lg + 1) * LANES)
        blocks = lambda a: [a[SUBLANES * j:SUBLANES * (j + 1), :] for j in range(nk // SUBLANES)]
        s0 = blocks(_dot(keys_ref[h, 0], qt[r:r + PEER_DK_HALF, lanes]))
        s1 = blocks(_dot(keys_ref[h, 1], qt[r + PEER_DK_HALF:r + 2 * PEER_DK_HALF, lanes]))
        top0, top1 = _top16_sorted(s0), _top16_sorted(s1)
        for j in range(PEER_TOPK):
            sv0_s[j:j + 1, lanes] = top0[j][0:1]
            sv1_s[j:j + 1, lanes] = top1[j][0:1]
        a0, a1 = sv0_s[0:8, lanes], sv0_s[8:16, lanes]
        b0, b1 = sv1_s[0:8, lanes], sv1_s[8:16, lanes]
        cands = [a0[0:1] + b0, a0[0:1] + b1] + [a0[a:a + 1] + b0 for a in range(1, 8)] + [a1 + b0[0:1]]
        best = _top16_sorted(cands)
        sel = [cd >= best[PEER_TOPK - 1] for cd in cands]
        z = functools.reduce(jnp.add, [jnp.where(sl, jnp.exp(cd - best[0]), 0.0) for sl, cd in zip(sel, cands)])
        z = jnp.sum(z, axis=0, keepdims=True)
        cnt = [jnp.sum(jnp.where(sl, 1.0, 0.0), axis=0, keepdims=True) for sl in sel[:9]]
        counts = [cnt[0] + cnt[1]] + cnt[2:9]
        tail = jnp.where(sel[9], 1.0, 0.0)
        counts = [jnp.broadcast_to(counts[a] if a < 8 else tail[a - 8:a - 7], (SUBLANES, LANES))
                  for a in range(PEER_TOPK)]
        lrow, rank1 = [], []
        for x0, x1 in zip(s0, s1):
            lx = jnp.zeros_like(x0)
            for a in reversed(range(PEER_TOPK)):
                lx = jnp.where(x0 >= top0[a], counts[a], lx)
            lrow.append(lx)
            rank1.append(_rank_in_sorted(x1, top1))
        l_s[h, :, lanes] = jnp.concatenate(lrow, axis=0)
        e0_s[h, :, lanes] = jnp.exp(jnp.concatenate(s0, axis=0) - top0[0][0:1]) / z
        r1_s[h, :, lanes] = jnp.concatenate(rank1, axis=0).astype(BF16)
        e1_s[h, :, lanes] = jnp.exp(jnp.concatenate(s1, axis=0) - top1[0][0:1]).astype(BF16)


def _peer_gate_chunk(ci, at_ref, ct_ref, lrow_s, erow_s, r1_s, e1_s, cb):
    nk, tb = PEER_N_KEYS, ct_ref.shape[1]
    tile = (nk // BF16_ROWS, BF16_ROWS, tb)
    row = lambda ref, h, j: jnp.broadcast_to(ref[ci, h, j:j + 1, :], tile[1:]).astype(BF16)[None]
    for j in range(cb):
        g = None
        for h in range(PEER_HEADS):
            term = jnp.where(r1_s[h].reshape(tile) < row(lrow_s, h, j), e1_s[h].reshape(tile),
                             jnp.zeros((), BF16)) * row(erow_s, h, j)
            g = term if g is None else g + term
        rows = slice((ci * cb + j) * nk, (ci * cb + j + 1) * nk)
        ct_ref[rows, :] = (_gelu(at_ref[j * nk:(j + 1) * nk, :].reshape(tile)) * g).reshape(nk, tb)


def _peer_kernel(x_ref, g2_ref, gf_ref, wqt_ref, keys_ref, wda_ref, wdb_ref, wut_ref, o_ref,
                 h_s, l_s, e0_s, r1_s, e1_s, sv0_s, sv1_s, lrow_s, erow_s, at0_s, at1_s, ct_s, acc_s, *, cb):
    s = pl.program_id(1)

    @pl.when(s == 0)
    def _first():
        h_s[...] = _rms(x_ref[...], g2_ref[...]).T.astype(BF16)
        qt = _dot(wqt_ref[...], h_s[...]).astype(BF16)
        for h in range(PEER_HEADS):
            _peer_route(h, qt, keys_ref, l_s, e0_s, r1_s, e1_s, sv0_s, sv1_s)
        at0_s[...] = _dot(wda_ref[...], h_s[...]).astype(BF16)
        acc_s[...] = jnp.zeros_like(acc_s)

    @pl.when(s > 0)
    def _steady():
        for ci in range(2):
            base = pl.multiple_of((2 * s - 2 + ci) * cb, cb)
            for h in range(PEER_HEADS):
                lrow_s[ci, h] = l_s[h, pl.ds(base, cb), :]
                erow_s[ci, h] = e0_s[h, pl.ds(base, cb), :]
        gate = functools.partial(_peer_gate_chunk, ct_ref=ct_s, lrow_s=lrow_s, erow_s=erow_s,
                                 r1_s=r1_s, e1_s=e1_s, cb=cb)
        gate(0, at0_s)
        at1_s[...] = _dot(wda_ref[...], h_s[...]).astype(BF16)
        at0_s[...] = _dot(wdb_ref[...], h_s[...]).astype(BF16)
        gate(1, at1_s)
        acc_s[...] += _dot(wut_ref[...], ct_s[...])

    @pl.when(s == pl.num_programs(1) - 1)
    def _last():
        y = x_ref[...] + acc_s[...].T
        o_ref[...] = _rms(y, gf_ref[...])


def _peer(x, g2, gf, wqt, keys, wd, wut, tb, cb):
    t = x.shape[0]
    ec = cb * PEER_N_KEYS
    nch = wd.shape[0] // ec
    assert nch % 2 == 0 and tb % LANES == 0
    stat = lambda dt: pltpu.VMEM((PEER_HEADS, PEER_N_KEYS, tb), dt)
    return pl.pallas_call(
        functools.partial(_peer_kernel, cb=cb),
        grid=(t // tb, nch // 2 + 1),
        in_specs=[pl.BlockSpec((tb, D_MODEL), lambda i, s: (i, 0)),
                  _const_spec((1, D_MODEL)), _const_spec((1, D_MODEL)),
                  _const_spec((D_MODEL, D_MODEL)),
                  _const_spec((PEER_HEADS, 2, PEER_N_KEYS, PEER_DK_HALF)),
                  pl.BlockSpec((ec, D_MODEL), lambda i, s: (jnp.maximum(2 * s - 1, 0), 0)),
                  pl.BlockSpec((ec, D_MODEL), lambda i, s: (jnp.minimum(2 * s, nch - 1), 0)),
                  pl.BlockSpec((D_MODEL, 2 * ec), lambda i, s: (0, jnp.maximum(s - 1, 0)))],
        out_specs=pl.BlockSpec((tb, D_MODEL), lambda i, s: (i, 0)),
        out_shape=jax.ShapeDtypeStruct((t, D_MODEL), F32),
        scratch_shapes=[pltpu.VMEM((D_MODEL, tb), BF16), stat(F32), stat(F32), stat(BF16), stat(BF16),
                        pltpu.VMEM((PEER_TOPK, tb), F32), pltpu.VMEM((PEER_TOPK, tb), F32),
                        pltpu.VMEM((2, PEER_HEADS, cb, tb), F32), pltpu.VMEM((2, PEER_HEADS, cb, tb), F32),
                        pltpu.VMEM((ec, tb), BF16), pltpu.VMEM((ec, tb), BF16),
                        pltpu.VMEM((2 * ec, tb), BF16), pltpu.VMEM((D_MODEL, tb), F32)],
        compiler_params=_params("parallel", "arbitrary"),
        name="peer",
    )(x, g2, gf, wqt, keys, wd, wd, wut)


def _rel_bucket(dist):
    n = jnp.maximum(dist, 0)
    max_exact = REL_BUCKETS // 2
    nf = jnp.maximum(n, 1).astype(F32)
    large = max_exact + (jnp.log(nf / max_exact) / jnp.log(REL_MAX_DIST / max_exact)
                         * (REL_BUCKETS - max_exact)).astype(jnp.int32)
    return jnp.where(n < max_exact, n, jnp.minimum(large, REL_BUCKETS - 1))


def _tile(t, cap):
    tm = min(t, cap)
    assert t % tm == 0, (t, tm)
    return tm


def kernel(x_prompt, x_sample, cache_conv, cache_swa_k, cache_swa_v, cache_mem_k, cache_mem_v, mem_prompt, rel_bias_table, norm1_g, w_in, conv_dw_w, conv_dw_b, conv_ln_g, conv_ln_b, w_conv_out, swa_sinks, w_swa_out, mem_norm_g, w_mem_kv, w_mem_out, w_out, norm2_g, peer_w_q, peer_keys, peer_w_down, peer_w_up, final_norm_g):
    assert w_in.shape[0] == 1, "single layer"
    batch, seq, _ = x_prompt.shape
    nsamp = x_sample.shape[0]
    assert x_sample.shape[1] == 1 and seq % WINDOW == 0
    row = lambda a: a.reshape(1, -1)

    w_proj = w_in[0, :, :W_PROJ].astype(BF16)
    w_gate = w_in[0, :, W_PROJ:].astype(BF16)
    g1, g2, gf = row(norm1_g[0]), row(norm2_g[0]), row(final_norm_g)
    dww, dwb = conv_dw_w[0], row(conv_dw_b[0])
    lng, lnb = row(conv_ln_g[0]), row(conv_ln_b[0])
    wco, wso = w_conv_out[0].astype(BF16), w_swa_out[0].astype(BF16)
    wmo, wo = w_mem_out[0].astype(BF16), w_out[0].astype(BF16)
    wqt = peer_w_q[0].T.astype(BF16)
    keys = peer_keys[0].astype(BF16)
    wd = peer_w_down.reshape(-1, D_MODEL).astype(BF16)
    wut = peer_w_up.reshape(-1, D_MODEL).astype(BF16).T
    sinks = swa_sinks[0]

    qi = jnp.arange(WINDOW)[:, None]
    ki = jnp.arange(2 * WINDOW)[None, :]
    def table_rows(dist):
        onehot = (_rel_bucket(dist)[..., None] == jnp.arange(REL_BUCKETS)).astype(F32)
        return jnp.einsum("...b,bh->h...", onehot, rel_bias_table.astype(F32),
                          precision=lax.Precision.HIGHEST)

    bias_p = table_rows(WINDOW + qi - ki)
    bias_s = table_rows(WINDOW - jnp.arange(WINDOW))
    bias_0 = table_rows(jnp.zeros((1,), jnp.int32))

    xp = x_prompt.reshape(batch * seq, D_MODEL)
    mkv = _norm_matmul(mem_prompt.reshape(batch * MEM_LEN, D_MODEL), row(mem_norm_g[0]),
                       w_mem_kv[0].astype(BF16), MEM_LEN)
    u_p, q_p, k_p, v_p, qm_p = _in_proj(xp, g1, w_proj, _tile(batch * seq, 512))
    tq = _tile(seq, 512)
    conv_p, swa_p, mem_p = _branches_prompt(sinks, u_p, q_p, k_p, v_p, qm_p, mkv, bias_p,
                                            dww, dwb, lng, lnb, batch, seq, tq)
    x2_p = _merge(xp, g1, conv_p, swa_p, mem_p, w_gate, wco, wso, wmo, wo, _tile(batch * seq, 512))
    y_p = _peer(x2_p, g2, gf, wqt, keys, wd, wut, _tile(batch * seq, 256), 8)

    xs = x_sample.reshape(nsamp, D_MODEL)
    u_s, q_s, k_s, v_s, qm_s = _in_proj(xs, g1, w_proj, _tile(nsamp, 128))
    q4 = q_s.reshape(nsamp, SWA_KV_HEADS, SWA_HEADS // SWA_KV_HEADS, SWA_HEAD_DIM)
    zq = jnp.zeros_like(q4[:, 0])
    qx = jnp.concatenate([jnp.concatenate([q4[:, 0], zq], -1), jnp.concatenate([zq, q4[:, 1]], -1)], 1)
    ck = cache_swa_k.reshape(nsamp, WINDOW, W_KV)
    cv = cache_swa_v.reshape(nsamp, WINDOW, W_KV)
    cmk = cache_mem_k.reshape(nsamp, MEM_LEN, MEM_HEADS, MEM_HEAD_DIM)
    cmv = cache_mem_v.reshape(nsamp, MEM_LEN, MEM_HEADS, MEM_HEAD_DIM)
    cconv = cache_conv.reshape(nsamp, CONV_WIDTH - 1, CONV_CH)
    conv_s, swa_x, mem_x = _branches_sample(sinks.reshape(SWA_HEADS, 1), u_s, cconv, qx, k_s, v_s, ck, cv,
                                            qm_s.reshape(nsamp, MEM_HEADS, MEM_HEAD_DIM), cmk, cmv,
                                            bias_s, bias_0, dww, dwb, lng, lnb, _tile(nsamp, 8))
    mem_s = mem_x.reshape(nsamp, W_QM)
    sx = swa_x.reshape(nsamp, SWA_KV_HEADS, SWA_HEADS // SWA_KV_HEADS, SWA_KV_HEADS, SWA_HEAD_DIM)
    swa_s = jnp.stack([sx[:, g, :, g] for g in range(SWA_KV_HEADS)], 1).reshape(nsamp, W_Q).astype(BF16)
    x2_s = _merge(xs, g1, conv_s, swa_s, mem_s, w_gate, wco, wso, wmo, wo, _tile(nsamp, 128))
    y_s = _peer(x2_s, g2, gf, wqt, keys, wd, wut, _tile(nsamp, 128), 8)

    hist = CONV_WIDTH - 1
    kv_shape = (SWA_KV_HEADS, SWA_HEAD_DIM)
    mkv5 = mkv.reshape(batch, MEM_LEN, 2, MEM_HEADS, MEM_HEAD_DIM)
    conv_state_p = u_p.reshape(batch, seq, CONV_CH)[:, -hist:]
    swa_k_p = k_p.reshape(batch, seq, *kv_shape)[:, -WINDOW:]
    swa_v_p = v_p.reshape(batch, seq, *kv_shape)[:, -WINDOW:]
    conv_state_s = jnp.concatenate([cache_conv[0][:, 1:], u_s[:, None, :]], axis=1)
    swa_k_s = jnp.concatenate([cache_swa_k[0][:, 1:], k_s.reshape(nsamp, 1, *kv_shape)], axis=1)
    swa_v_s = jnp.concatenate([cache_swa_v[0][:, 1:], v_s.reshape(nsamp, 1, *kv_shape)], axis=1)
    return (y_p.reshape(batch, seq, D_MODEL), y_s.reshape(nsamp, 1, D_MODEL),
            conv_state_p[None], swa_k_p[None], swa_v_p[None],
            mkv5[:, :, 0][None], mkv5[:, :, 1][None],
            conv_state_s[None], swa_k_s[None], swa_v_s[None])
```

```python
import functools

import jax
import jax.numpy as jnp
from jax import lax
from jax.experimental import pallas as pl
from jax.experimental.pallas import tpu as pltpu

F32 = jnp.float32
BF16 = jnp.bfloat16

D_MODEL = 1024
PAST_LEN = 16384
MEM_LEN = 256
CONV_CH = 512
CONV_WIDTH = 31
SWA_HEADS = 8
SWA_KV_HEADS = 2
SWA_HEAD_DIM = 64
WINDOW = 128
SWA_SCALE = SWA_HEAD_DIM ** -0.5
MEM_HEADS = 4
MEM_HEAD_DIM = 128
MEM_SCALE = MEM_HEAD_DIM ** -0.5
REL_BUCKETS = 32
REL_MAX_DIST = 128
PEER_HEADS = 8
PEER_N_KEYS = 128
PEER_DK_HALF = 64
PEER_TOPK = 16
EPS = 1e-6
NEG_INF = -1e30

W_GLU = 2 * CONV_CH
W_Q = SWA_HEADS * SWA_HEAD_DIM
W_KV = SWA_KV_HEADS * SWA_HEAD_DIM
W_QM = MEM_HEADS * MEM_HEAD_DIM
W_PROJ = W_GLU + W_Q + 2 * W_KV + W_QM

VMEM_LIMIT_BYTES = 56 * 1024 * 1024
LANES = 128
SUBLANES = 8
BF16_ROWS = 16
CONV_HALO = 32


def _params(*sem, flags=None):
    return pltpu.CompilerParams(dimension_semantics=sem, vmem_limit_bytes=VMEM_LIMIT_BYTES, flags=flags)


def _rms(x, g):
    return x * lax.rsqrt(jnp.mean(x * x, axis=-1, keepdims=True) + EPS) * g


def _dot(a, b):
    return jnp.dot(a, b, preferred_element_type=F32)


def _dot_nt(a, b):
    return lax.dot_general(a, b, (((1,), (1,)), ((), ())), preferred_element_type=F32)


def _const_spec(shape):
    zeros = (0,) * len(shape)
    return pl.BlockSpec(shape, lambda *_: zeros)


def _norm_matmul_kernel(x_ref, g_ref, w_ref, o_ref):
    o_ref[...] = _dot(_rms(x_ref[...], g_ref[...]).astype(BF16), w_ref[...])


def _norm_matmul(x, g, w, tm):
    t, n = x.shape[0], w.shape[1]
    return pl.pallas_call(
        _norm_matmul_kernel,
        grid=(t // tm,),
        in_specs=[pl.BlockSpec((tm, D_MODEL), lambda i: (i, 0)), _const_spec((1, D_MODEL)),
                  _const_spec((D_MODEL, n))],
        out_specs=pl.BlockSpec((tm, n), lambda i: (i, 0)),
        out_shape=jax.ShapeDtypeStruct((t, n), F32),
        compiler_params=_params("parallel"),
        name="memkv",
    )(x, g, w)


def _in_proj_kernel(x_ref, g_ref, w_ref, u_ref, q_ref, k_ref, v_ref, qm_ref):
    z = _dot(_rms(x_ref[...], g_ref[...]).astype(BF16), w_ref[...])
    a, b = z[:, :CONV_CH], z[:, CONV_CH:W_GLU]
    u_ref[...] = a * jax.nn.sigmoid(b)
    c = W_GLU
    q_ref[...] = (z[:, c:c + W_Q] * SWA_SCALE).astype(BF16)
    c += W_Q
    k_ref[...] = z[:, c:c + W_KV]
    c += W_KV
    v_ref[...] = z[:, c:c + W_KV]
    c += W_KV
    qm_ref[...] = z[:, c:c + W_QM].astype(BF16)


def _in_proj(x, g, w, tm):
    t = x.shape[0]
    row = lambda n: pl.BlockSpec((tm, n), lambda i: (i, 0))
    return pl.pallas_call(
        _in_proj_kernel,
        grid=(t // tm,),
        in_specs=[row(D_MODEL), _const_spec((1, D_MODEL)), _const_spec((D_MODEL, W_PROJ))],
        out_specs=[row(CONV_CH), row(W_Q), row(W_KV), row(W_KV), row(W_QM)],
        out_shape=[jax.ShapeDtypeStruct((t, CONV_CH), F32), jax.ShapeDtypeStruct((t, W_Q), BF16),
                   jax.ShapeDtypeStruct((t, W_KV), F32), jax.ShapeDtypeStruct((t, W_KV), F32),
                   jax.ShapeDtypeStruct((t, W_QM), BF16)],
        compiler_params=_params("parallel"),
        name="in_proj",
    )(x, g, w)


def _ln_silu(y, g, b):
    mu = jnp.mean(y, axis=-1, keepdims=True)
    var = jnp.mean(jnp.square(y - mu), axis=-1, keepdims=True)
    y = (y - mu) * lax.rsqrt(var + EPS) * g + b
    return y * jax.nn.sigmoid(y)


def _softmax_rows(s):
    e = jnp.exp(s - jnp.max(s, axis=-1, keepdims=True))
    return e / jnp.sum(e, axis=-1, keepdims=True)


def _branches_prompt_kernel(sinks_ref, u_ref, uh_ref, q_ref, k_ref, kp_ref, v_ref, vp_ref, qm_ref,
                            mk_ref, mv_ref, bias_ref, dww_ref, dwb_ref, lng_ref, lnb_ref,
                            conv_ref, swa_ref, mem_ref, ubuf, ushift, kbuf, vbuf, *, tq):
    i = pl.program_id(1)
    first = i == 0

    ubuf[0:CONV_HALO, :] = jnp.where(first, 0.0, uh_ref[...])
    ubuf[CONV_HALO:CONV_HALO + tq, :] = u_ref[...]
    nshift = ushift.shape[1]
    for b in range(1, SUBLANES):
        ushift[b] = ubuf[b:b + nshift, :]
    rb = 64
    off = CONV_HALO - (CONV_WIDTH - 1)
    for r in range(tq // rb):
        acc = jnp.broadcast_to(dwb_ref[...], (rb, CONV_CH))
        for j in range(CONV_WIDTH):
            a, b = divmod(off + j, SUBLANES)
            start = r * rb + a * SUBLANES
            rows = ubuf[start:start + rb, :] if b == 0 else ushift[b, start:start + rb, :]
            acc = acc + rows * dww_ref[j:j + 1, :]
        conv_ref[r * rb:(r + 1) * rb, :] = _ln_silu(acc, lng_ref[...], lnb_ref[...]).astype(BF16)

    lane = lax.broadcasted_iota(jnp.int32, (WINDOW + tq, LANES), 1)
    lo = lane < SWA_HEAD_DIM
    for src_ref, prev_ref, buf in ((k_ref, kp_ref, kbuf), (v_ref, vp_ref, vbuf)):
        full = jnp.concatenate([jnp.where(first, 0.0, prev_ref[...]), src_ref[...]], axis=0)
        rolled = pltpu.roll(full, SWA_HEAD_DIM, 1)
        buf[0] = jnp.where(lo, full, 0.0).astype(BF16)
        buf[1] = jnp.where(lo, 0.0, rolled).astype(BF16)
        buf[2] = jnp.where(lo, rolled, 0.0).astype(BF16)
        buf[3] = jnp.where(lo, 0.0, full).astype(BF16)

    qi = lax.broadcasted_iota(jnp.int32, (WINDOW, 2 * WINDOW), 0)
    ki = lax.broadcasted_iota(jnp.int32, (WINDOW, 2 * WINDOW), 1)
    dist = WINDOW + qi - ki
    band = (dist >= 0) & (dist <= WINDOW)
    nqb = tq // WINDOW
    for jb in range(nqb):
        r0 = jb * WINDOW
        kmin = jnp.where(i * nqb + jb > 0, 0, WINDOW)
        mask = band & (ki >= kmin)
        for p in range(SWA_HEADS // 2):
            g = p // 2
            qp = q_ref[r0:r0 + WINDOW, LANES * p:LANES * (p + 1)]
            o = None
            for half in range(2):
                h = 2 * p + half
                s = _dot_nt(qp, kbuf[2 * g + half, r0:r0 + 2 * WINDOW, :]) + bias_ref[h]
                s = jnp.where(mask, s, NEG_INF)
                sink = sinks_ref[h]
                m = jnp.maximum(jnp.max(s, axis=-1, keepdims=True), sink)
                pr = jnp.exp(s - m)
                pr = pr / (jnp.sum(pr, axis=-1, keepdims=True) + jnp.exp(sink - m))
                t = _dot(pr.astype(BF16), vbuf[2 * g + half, r0:r0 + 2 * WINDOW, :])
                o = t if o is None else o + t
            swa_ref[r0:r0 + WINDOW, LANES * p:LANES * (p + 1)] = o.astype(BF16)

    for hm in range(MEM_HEADS):
        sl = slice(hm * MEM_HEAD_DIM, (hm + 1) * MEM_HEAD_DIM)
        w = _softmax_rows(_dot_nt(qm_ref[:, sl], mk_ref[:, sl].astype(BF16)) * MEM_SCALE)
        mem_ref[:, sl] = _dot(w.astype(BF16), mv_ref[:, sl].astype(BF16)).astype(BF16)


def _branches_prompt(sinks, u, q, k, v, qm, mkv, bias, dww, dwb, lng, lnb, batch, seq, tq):
    t = batch * seq
    nq = seq // tq
    row = lambda n: pl.BlockSpec((tq, n), lambda b, i: (b * nq + i, 0))
    halo = lambda rows, n: pl.BlockSpec(
        (rows, n), lambda b, i: (jnp.maximum(b * (seq // rows) + i * (tq // rows) - 1, 0), 0))
    return pl.pallas_call(
        functools.partial(_branches_prompt_kernel, tq=tq),
        grid=(batch, nq),
        in_specs=[pl.BlockSpec(memory_space=pltpu.SMEM),
                  row(CONV_CH), halo(CONV_HALO, CONV_CH), row(W_Q),
                  row(W_KV), halo(WINDOW, W_KV), row(W_KV), halo(WINDOW, W_KV), row(W_QM),
                  pl.BlockSpec((MEM_LEN, W_QM), lambda b, i: (b, 0)),
                  pl.BlockSpec((MEM_LEN, W_QM), lambda b, i: (b, 1)),
                  _const_spec((SWA_HEADS, WINDOW, 2 * WINDOW)),
                  _const_spec((CONV_WIDTH, CONV_CH)), _const_spec((1, CONV_CH)),
                  _const_spec((1, CONV_CH)), _const_spec((1, CONV_CH))],
        out_specs=[row(CONV_CH), row(W_Q), row(W_QM)],
        out_shape=[jax.ShapeDtypeStruct((t, CONV_CH), BF16), jax.ShapeDtypeStruct((t, W_Q), BF16),
                   jax.ShapeDtypeStruct((t, W_QM), BF16)],
        scratch_shapes=[pltpu.VMEM((CONV_HALO + tq, CONV_CH), F32),
                        pltpu.VMEM((SUBLANES, CONV_HALO + tq - SUBLANES, CONV_CH), F32),
                        pltpu.VMEM((4, WINDOW + tq, LANES), BF16),
                        pltpu.VMEM((4, WINDOW + tq, LANES), BF16)],
        compiler_params=_params("parallel", "arbitrary"),
        name="branches_prompt",
    )(sinks, u, u, q, k, k, v, v, qm, mkv, mkv, bias, dww, dwb, lng, lnb)


def _branches_sample_kernel(sinks_ref, u_ref, cc_ref, qx_ref, kn_ref, vn_ref, ck_ref, cv_ref, qm_ref,
                            cmk_ref, cmv_ref, bias_ref, bias0_ref, dww_ref, dwb_ref, lng_ref, lnb_ref,
                            conv_ref, swa_ref, mem_ref, *, nb):
    hist = CONV_WIDTH - 1
    y = jnp.sum(cc_ref[...] * dww_ref[0:hist, :][None], axis=1)
    y = y + u_ref[...] * dww_ref[hist:hist + 1, :] + dwb_ref[...]
    conv_ref[...] = _ln_silu(y, lng_ref[...], lnb_ref[...]).astype(BF16)

    sink = sinks_ref[...]
    for n in range(nb):
        qx = qx_ref[n]
        s = _dot_nt(qx, ck_ref[n].astype(BF16)) + bias_ref[...]
        kn = kn_ref[n:n + 1, :].astype(BF16).astype(F32)
        s_new = jnp.sum(qx.astype(F32) * kn, axis=-1, keepdims=True) + bias0_ref[...]
        m = jnp.maximum(jnp.maximum(jnp.max(s, axis=-1, keepdims=True), s_new), sink)
        pr, pr_new = jnp.exp(s - m), jnp.exp(s_new - m)
        den = jnp.sum(pr, axis=-1, keepdims=True) + pr_new + jnp.exp(sink - m)
        vn = vn_ref[n:n + 1, :].astype(BF16).astype(F32)
        o = _dot((pr / den).astype(BF16), cv_ref[n].astype(BF16))
        swa_ref[n] = o + (pr_new / den).astype(BF16).astype(F32) * vn

        bf = lambda a: a.astype(BF16).astype(F32)
        sm = jnp.sum(bf(cmk_ref[n]) * qm_ref[n].astype(F32)[None], axis=-1, keepdims=True) * MEM_SCALE
        e = jnp.exp(sm - jnp.max(sm, axis=0, keepdims=True))
        w = e / jnp.sum(e, axis=0, keepdims=True)
        mem_ref[n] = jnp.sum(bf(w) * bf(cmv_ref[n]), axis=0).astype(BF16)


def _branches_sample(sinks, u, cache_conv, qx, kn, vn, ck, cv, qm, cmk, cmv, bias, bias0,
                     dww, dwb, lng, lnb, nb):
    n = u.shape[0]
    row = lambda c: pl.BlockSpec((nb, c), lambda i: (i, 0))
    blk3 = lambda a, c: pl.BlockSpec((nb, a, c), lambda i: (i, 0, 0))
    return pl.pallas_call(
        functools.partial(_branches_sample_kernel, nb=nb),
        grid=(n // nb,),
        in_specs=[_const_spec((SWA_HEADS, 1)),
                  row(CONV_CH), blk3(CONV_WIDTH - 1, CONV_CH), blk3(SWA_HEADS, LANES),
                  row(W_KV), row(W_KV), blk3(WINDOW, W_KV), blk3(WINDOW, W_KV),
                  blk3(MEM_HEADS, MEM_HEAD_DIM),
                  pl.BlockSpec((nb, MEM_LEN, MEM_HEADS, MEM_HEAD_DIM), lambda i: (i, 0, 0, 0)),
                  pl.BlockSpec((nb, MEM_LEN, MEM_HEADS, MEM_HEAD_DIM), lambda i: (i, 0, 0, 0)),
                  _const_spec((SWA_HEADS, WINDOW)), _const_spec((SWA_HEADS, 1)),
                  _const_spec((CONV_WIDTH, CONV_CH)), _const_spec((1, CONV_CH)),
                  _const_spec((1, CONV_CH)), _const_spec((1, CONV_CH))],
        out_specs=[row(CONV_CH), blk3(SWA_HEADS, LANES), blk3(MEM_HEADS, MEM_HEAD_DIM)],
        out_shape=[jax.ShapeDtypeStruct((n, CONV_CH), BF16),
                   jax.ShapeDtypeStruct((n, SWA_HEADS, LANES), F32),
                   jax.ShapeDtypeStruct((n, MEM_HEADS, MEM_HEAD_DIM), BF16)],
        compiler_params=_params("parallel"),
        name="branches_sample",
    )(sinks, u, cache_conv, qx, kn, vn, ck, cv, qm, cmk, cmv, bias, bias0, dww, dwb, lng, lnb)


def _merge_kernel(x_ref, g_ref, conv_ref, swa_ref, mem_ref, wg_ref, wco_ref, wso_ref, wmo_ref, wo_ref,
                  o_ref):
    x = x_ref[...]
    h = _rms(x, g_ref[...]).astype(BF16)
    merged = None
    for br, (a_ref, w_ref) in enumerate(((conv_ref, wco_ref), (swa_ref, wso_ref), (mem_ref, wmo_ref))):
        gate = jax.nn.sigmoid(_dot(h, wg_ref[:, br * D_MODEL:(br + 1) * D_MODEL]))
        term = gate * _dot(a_ref[...], w_ref[...])
        merged = term if merged is None else merged + term
    o_ref[...] = x + _dot(merged.astype(BF16), wo_ref[...])


def _merge(x, g, conv, swa, mem, wg, wco, wso, wmo, wo, tm):
    t = x.shape[0]
    row = lambda n: pl.BlockSpec((tm, n), lambda i: (i, 0))
    return pl.pallas_call(
        _merge_kernel,
        grid=(t // tm,),
        in_specs=[row(D_MODEL), _const_spec((1, D_MODEL)), row(CONV_CH), row(W_Q), row(W_QM),
                  _const_spec((D_MODEL, 3 * D_MODEL)), _const_spec((CONV_CH, D_MODEL)),
                  _const_spec((W_Q, D_MODEL)), _const_spec((W_QM, D_MODEL)),
                  _const_spec((D_MODEL, D_MODEL))],
        out_specs=row(D_MODEL),
        out_shape=jax.ShapeDtypeStruct((t, D_MODEL), F32),
        compiler_params=_params("parallel"),
        name="merge",
    )(x, g, conv, swa, mem, wg, wco, wso, wmo, wo)


def _gelu_x2(x):
    return x * (1.0 + lax.erf(x * (2.0 ** -0.5)))


def _top_values(arrs, count, with_rank=False):
    out = []
    ranks = [jnp.full(a.shape, float(count), F32) for a in arrs] if with_rank else None
    for it in range(count):
        m = jnp.max(functools.reduce(jnp.maximum, arrs), axis=0, keepdims=True)
        out.append(m)
        hit = [a == m for a in arrs]
        if with_rank:
            ranks = [jnp.where(hh, float(it), rk) for hh, rk in zip(hit, ranks)]
        arrs = [jnp.where(hh, -jnp.inf, a) for hh, a in zip(hit, arrs)]
    return (out, ranks) if with_rank else out


def _oddeven_merge_sort(lo, hi):
    def merge(lo, hi, r):
        step = 2 * r
        if step < hi - lo:
            yield from merge(lo, hi, step)
            yield from merge(lo + r, hi, step)
            yield from ((i, i + r) for i in range(lo + r, hi - r, step))
        else:
            yield (lo, lo + r)

    if hi > lo:
        mid = lo + (hi - lo) // 2
        yield from _oddeven_merge_sort(lo, mid)
        yield from _oddeven_merge_sort(mid + 1, hi)
        yield from merge(lo, hi, 1)


def _compare_exchange(a, i, j):
    if a[j] is None:
        return
    if a[i] is None:
        a[i], a[j] = a[j], None
    else:
        a[i], a[j] = jnp.maximum(a[i], a[j]), jnp.minimum(a[i], a[j])


def _top16_sorted(blocks):
    n = PEER_TOPK
    assert n // 2 < len(blocks) <= n and blocks[0].shape[0] == SUBLANES
    a = list(blocks) + [None] * (n - len(blocks))
    larger = lambda x, y: x if y is None else y if x is None else jnp.maximum(x, y)
    for i, j in _oddeven_merge_sort(0, n - 1):
        _compare_exchange(a, i, j)
    for shift in (4, 2, 1):
        b = [None if x is None else pltpu.roll(x, shift, 0) for x in a]
        a = [larger(a[i], b[n - 1 - i]) for i in range(n)]
        d = n // 2
        while d:
            for i in range(n):
                if not i & d:
                    _compare_exchange(a, i, i + d)
            d //= 2
    return a


def _rank_in_sorted(x, top):
    assert len(top) == 16
    rank, lo = None, [0] * 1
    conds = []
    for width in (8, 4, 2, 1):
        idx = [b + width - 1 for b in lo]
        thr = [top[i] for i in idx]
        for c in reversed(conds):
            thr = [jnp.where(c, thr[2 * k], thr[2 * k + 1]) for k in range(len(thr) // 2)]
        c = x >= thr[0]
        step = jnp.where(c, 0.0, float(width))
        rank = step if rank is None else rank + step
        conds.append(c)
        lo = [b + off for b in lo for off in (0, width)]
    return jnp.where(x >= top[15], rank, 16.0)


def _peer_route(h, qt, keys_ref, l_s, e0_s, r1_s, e1_s, sv0_s, sv1_s):
    nk, tb = PEER_N_KEYS, qt.shape[1]
    r = 2 * h * PEER_DK_HALF
    for lg in range(tb // LANES):
        lanes = slice(lg * LANES, (lg + 1) * LANES)
        blocks = lambda a: [a[SUBLANES * j:SUBLANES * (j + 1), :] for j in range(nk // SUBLANES)]
        s0 = blocks(_dot(keys_ref[h, 0], qt[r:r + PEER_DK_HALF, lanes]))
        s1 = blocks(_dot(keys_ref[h, 1], qt[r + PEER_DK_HALF:r + 2 * PEER_DK_HALF, lanes]))
        top0, top1 = _top16_sorted(s0), _top16_sorted(s1)
        for j in range(PEER_TOPK):
            sv0_s[j:j + 1, lanes] = top0[j][0:1]
            sv1_s[j:j + 1, lanes] = top1[j][0:1]
        a0, a1 = sv0_s[0:8, lanes], sv0_s[8:16, lanes]
        b0, b1 = sv1_s[0:8, lanes], sv1_s[8:16, lanes]
        cands = [a0[0:1] + b0, a0[0:1] + b1] + [a0[a:a + 1] + b0 for a in range(1, 8)] + [a1 + b0[0:1]]
        best = _top16_sorted(cands)
        sel = [cd >= best[PEER_TOPK - 1] for cd in cands]
        z = functools.reduce(jnp.add, [jnp.where(sl, jnp.exp(cd - best[0]), 0.0) for sl, cd in zip(sel, cands)])
        z = jnp.sum(z, axis=0, keepdims=True)
        cnt = [jnp.sum(jnp.where(sl, 1.0, 0.0), axis=0, keepdims=True) for sl in sel[:9]]
        counts = [cnt[0] + cnt[1]] + cnt[2:9]
        tail = jnp.where(sel[9], 1.0, 0.0)
        counts = [jnp.broadcast_to(counts[a] if a < 8 else tail[a - 8:a - 7], (SUBLANES, LANES))
                  for a in range(PEER_TOPK)]
        lrow, rank1 = [], []
        for x0, x1 in zip(s0, s1):
            lx = jnp.zeros_like(x0)
            for a in reversed(range(PEER_TOPK)):
                lx = jnp.where(x0 >= top0[a], counts[a], lx)
            lrow.append(lx)
            rank1.append(_rank_in_sorted(x1, top1))
        l_s[h, :, lanes] = jnp.concatenate(lrow, axis=0)
        e0_s[h, :, lanes] = jnp.exp(jnp.concatenate(s0, axis=0) - top0[0][0:1]) / z * 0.5
        r1_s[h, :, lanes] = jnp.concatenate(rank1, axis=0).astype(BF16)
        e1_s[h, :, lanes] = jnp.exp(jnp.concatenate(s1, axis=0) - top1[0][0:1]).astype(BF16)


def _peer_gate_chunk(ci, at_ref, ct_ref, lrow_s, erow_s, r1_s, e1_s, cb):
    nk, tb = PEER_N_KEYS, ct_ref.shape[1]
    tile = (nk // BF16_ROWS, BF16_ROWS, tb)
    row = lambda ref, h, j: jnp.broadcast_to(ref[ci, h, j:j + 1, :], tile[1:]).astype(BF16)[None]
    for j in range(cb):
        g = None
        for h in range(PEER_HEADS):
            term = jnp.where(r1_s[h].reshape(tile) < row(lrow_s, h, j), e1_s[h].reshape(tile),
                             jnp.zeros((), BF16)) * row(erow_s, h, j)
            g = term if g is None else g + term
        rows = slice((ci * cb + j) * nk, (ci * cb + j + 1) * nk)
        ct_ref[rows, :] = (_gelu_x2(at_ref[j * nk:(j + 1) * nk, :].reshape(tile)) * g).reshape(nk, tb)


def _peer_kernel(x_ref, g2_ref, gf_ref, wqt_ref, keys_ref, wda_ref, wdb_ref, wut_ref, o_ref,
                 h_s, l_s, e0_s, r1_s, e1_s, sv0_s, sv1_s, lrow_s, erow_s, at0_s, at1_s, ct_s, acc_s, *, cb):
    s = pl.program_id(1)

    @pl.when(s == 0)
    def _first():
        h_s[...] = _rms(x_ref[...], g2_ref[...]).T.astype(BF16)
        qt = _dot(wqt_ref[...], h_s[...]).astype(BF16)
        for h in range(PEER_HEADS):
            _peer_route(h, qt, keys_ref, l_s, e0_s, r1_s, e1_s, sv0_s, sv1_s)
        at0_s[...] = _dot(wda_ref[...], h_s[...]).astype(BF16)
        acc_s[...] = jnp.zeros_like(acc_s)

    @pl.when(s > 0)
    def _steady():
        for ci in range(2):
            base = pl.multiple_of((2 * s - 2 + ci) * cb, cb)
            for h in range(PEER_HEADS):
                lrow_s[ci, h] = l_s[h, pl.ds(base, cb), :]
                erow_s[ci, h] = e0_s[h, pl.ds(base, cb), :]
        gate = functools.partial(_peer_gate_chunk, ct_ref=ct_s, lrow_s=lrow_s, erow_s=erow_s,
                                 r1_s=r1_s, e1_s=e1_s, cb=cb)
        gate(0, at0_s)
        at1_s[...] = _dot(wda_ref[...], h_s[...]).astype(BF16)
        at0_s[...] = _dot(wdb_ref[...], h_s[...]).astype(BF16)
        gate(1, at1_s)
        acc_s[...] += _dot(wut_ref[...], ct_s[...])

    @pl.when(s == pl.num_programs(1) - 1)
    def _last():
        y = x_ref[...] + acc_s[...].T
        o_ref[...] = _rms(y, gf_ref[...])


def _peer(x, g2, gf, wqt, keys, wd, wut, tb, cb):
    t = x.shape[0]
    ec = cb * PEER_N_KEYS
    nch = wd.shape[0] // ec
    assert nch % 2 == 0 and tb % LANES == 0
    stat = lambda dt: pltpu.VMEM((PEER_HEADS, PEER_N_KEYS, tb), dt)
    return pl.pallas_call(
        functools.partial(_peer_kernel, cb=cb),
        grid=(t // tb, nch // 2 + 1),
        in_specs=[pl.BlockSpec((tb, D_MODEL), lambda i, s: (i, 0)),
                  _const_spec((1, D_MODEL)), _const_spec((1, D_MODEL)),
                  _const_spec((D_MODEL, D_MODEL)),
                  _const_spec((PEER_HEADS, 2, PEER_N_KEYS, PEER_DK_HALF)),
                  pl.BlockSpec((ec, D_MODEL), lambda i, s: (jnp.maximum(2 * s - 1, 0), 0)),
                  pl.BlockSpec((ec, D_MODEL), lambda i, s: (jnp.minimum(2 * s, nch - 1), 0)),
                  pl.BlockSpec((D_MODEL, 2 * ec), lambda i, s: (0, jnp.maximum(s - 1, 0)))],
        out_specs=pl.BlockSpec((tb, D_MODEL), lambda i, s: (i, 0)),
        out_shape=jax.ShapeDtypeStruct((t, D_MODEL), F32),
        scratch_shapes=[pltpu.VMEM((D_MODEL, tb), BF16), stat(F32), stat(F32), stat(BF16), stat(BF16),
                        pltpu.VMEM((PEER_TOPK, tb), F32), pltpu.VMEM((PEER_TOPK, tb), F32),
                        pltpu.VMEM((2, PEER_HEADS, cb, tb), F32), pltpu.VMEM((2, PEER_HEADS, cb, tb), F32),
                        pltpu.VMEM((ec, tb), BF16), pltpu.VMEM((ec, tb), BF16),
                        pltpu.VMEM((2 * ec, tb), BF16), pltpu.VMEM((D_MODEL, tb), F32)],
        compiler_params=_params("parallel", "arbitrary"),
        name="peer",
    )(x, g2, gf, wqt, keys, wd, wd, wut)


def _rel_bucket(dist):
    n = jnp.maximum(dist, 0)
    max_exact = REL_BUCKETS // 2
    nf = jnp.maximum(n, 1).astype(F32)
    large = max_exact + (jnp.log(nf / max_exact) / jnp.log(REL_MAX_DIST / max_exact)
                         * (REL_BUCKETS - max_exact)).astype(jnp.int32)
    return jnp.where(n < max_exact, n, jnp.minimum(large, REL_BUCKETS - 1))


TOKEN_TILE = 512
SAMPLE_TILE = 8
PEER_CHUNK_ROW_BLOCKS = 8


def _tile(t, cap=TOKEN_TILE):
    tm = min(t, cap)
    assert t % tm == 0, (t, tm)
    return tm


def kernel(x_prompt, x_sample, cache_conv, cache_swa_k, cache_swa_v, cache_mem_k, cache_mem_v, mem_prompt, rel_bias_table, norm1_g, w_in, conv_dw_w, conv_dw_b, conv_ln_g, conv_ln_b, w_conv_out, swa_sinks, w_swa_out, mem_norm_g, w_mem_kv, w_mem_out, w_out, norm2_g, peer_w_q, peer_keys, peer_w_down, peer_w_up, final_norm_g):
    assert w_in.shape[0] == 1, "single layer"
    batch, seq, _ = x_prompt.shape
    nsamp = x_sample.shape[0]
    assert x_sample.shape[1] == 1 and seq % WINDOW == 0
    row = lambda a: a.reshape(1, -1)

    w_proj = w_in[0, :, :W_PROJ].astype(BF16)
    w_gate = w_in[0, :, W_PROJ:].astype(BF16)
    g1, g2, gf = row(norm1_g[0]), row(norm2_g[0]), row(final_norm_g)
    dww, dwb = conv_dw_w[0], row(conv_dw_b[0])
    lng, lnb = row(conv_ln_g[0]), row(conv_ln_b[0])
    wco, wso = w_conv_out[0].astype(BF16), w_swa_out[0].astype(BF16)
    wmo, wo = w_mem_out[0].astype(BF16), w_out[0].astype(BF16)
    wqt = peer_w_q[0].T.astype(BF16)
    keys = peer_keys[0].astype(BF16)
    wd = peer_w_down.reshape(-1, D_MODEL).astype(BF16)
    wut = peer_w_up.reshape(-1, D_MODEL).astype(BF16).T
    sinks = swa_sinks[0]

    qi = jnp.arange(WINDOW)[:, None]
    ki = jnp.arange(2 * WINDOW)[None, :]
    def table_rows(dist):
        onehot = (_rel_bucket(dist)[..., None] == jnp.arange(REL_BUCKETS)).astype(F32)
        return jnp.einsum("...b,bh->h...", onehot, rel_bias_table.astype(F32),
                          precision=lax.Precision.HIGHEST)

    bias_p = table_rows(WINDOW + qi - ki)
    bias_s = table_rows(WINDOW - jnp.arange(WINDOW))
    bias_0 = table_rows(jnp.zeros((1,), jnp.int32))

    xp = x_prompt.reshape(batch * seq, D_MODEL)
    mkv = _norm_matmul(mem_prompt.reshape(batch * MEM_LEN, D_MODEL), row(mem_norm_g[0]),
                       w_mem_kv[0].astype(BF16), MEM_LEN)
    u_p, q_p, k_p, v_p, qm_p = _in_proj(xp, g1, w_proj, _tile(batch * seq))
    tq = _tile(seq)
    conv_p, swa_p, mem_p = _branches_prompt(sinks, u_p, q_p, k_p, v_p, qm_p, mkv, bias_p,
                                            dww, dwb, lng, lnb, batch, seq, tq)
    x2_p = _merge(xp, g1, conv_p, swa_p, mem_p, w_gate, wco, wso, wmo, wo, _tile(batch * seq))
    y_p = _peer(x2_p, g2, gf, wqt, keys, wd, wut, _tile(batch * seq), PEER_CHUNK_ROW_BLOCKS)

    xs = x_sample.reshape(nsamp, D_MODEL)
    u_s, q_s, k_s, v_s, qm_s = _in_proj(xs, g1, w_proj, _tile(nsamp))
    q4 = q_s.reshape(nsamp, SWA_KV_HEADS, SWA_HEADS // SWA_KV_HEADS, SWA_HEAD_DIM)
    zq = jnp.zeros_like(q4[:, 0])
    qx = jnp.concatenate([jnp.concatenate([q4[:, 0], zq], -1), jnp.concatenate([zq, q4[:, 1]], -1)], 1)
    ck = cache_swa_k.reshape(nsamp, WINDOW, W_KV)
    cv = cache_swa_v.reshape(nsamp, WINDOW, W_KV)
    cmk = cache_mem_k.reshape(nsamp, MEM_LEN, MEM_HEADS, MEM_HEAD_DIM)
    cmv = cache_mem_v.reshape(nsamp, MEM_LEN, MEM_HEADS, MEM_HEAD_DIM)
    cconv = cache_conv.reshape(nsamp, CONV_WIDTH - 1, CONV_CH)
    conv_s, swa_x, mem_x = _branches_sample(sinks.reshape(SWA_HEADS, 1), u_s, cconv, qx, k_s, v_s, ck, cv,
                                            qm_s.reshape(nsamp, MEM_HEADS, MEM_HEAD_DIM), cmk, cmv,
                                            bias_s, bias_0, dww, dwb, lng, lnb, _tile(nsamp, SAMPLE_TILE))
    mem_s = mem_x.reshape(nsamp, W_QM)
    sx = swa_x.reshape(nsamp, SWA_KV_HEADS, SWA_HEADS // SWA_KV_HEADS, SWA_KV_HEADS, SWA_HEAD_DIM)
    swa_s = jnp.stack([sx[:, g, :, g] for g in range(SWA_KV_HEADS)], 1).reshape(nsamp, W_Q).astype(BF16)
    x2_s = _merge(xs, g1, conv_s, swa_s, mem_s, w_gate, wco, wso, wmo, wo, _tile(nsamp))
    y_s = _peer(x2_s, g2, gf, wqt, keys, wd, wut, _tile(nsamp), PEER_CHUNK_ROW_BLOCKS)

    hist = CONV_WIDTH - 1
    kv_shape = (SWA_KV_HEADS, SWA_HEAD_DIM)
    mkv5 = mkv.reshape(batch, MEM_LEN, 2, MEM_HEADS, MEM_HEAD_DIM)
    conv_state_p = u_p.reshape(batch, seq, CONV_CH)[:, -hist:]
    swa_k_p = k_p.reshape(batch, seq, *kv_shape)[:, -WINDOW:]
    swa_v_p = v_p.reshape(batch, seq, *kv_shape)[:, -WINDOW:]
    conv_state_s = jnp.concatenate([cache_conv[0][:, 1:], u_s[:, None, :]], axis=1)
    swa_k_s = jnp.concatenate([cache_swa_k[0][:, 1:], k_s.reshape(nsamp, 1, *kv_shape)], axis=1)
    swa_v_s = jnp.concatenate([cache_swa_v[0][:, 1:], v_s.reshape(nsamp, 1, *kv_shape)], axis=1)
    return (y_p.reshape(batch, seq, D_MODEL), y_s.reshape(nsamp, 1, D_MODEL),
            conv_state_p[None], swa_k_p[None], swa_v_p[None],
            mkv5[:, :, 0][None], mkv5[:, :, 1][None],
            conv_state_s[None], swa_k_s[None], swa_v_s[None])
```

```python
import functools

import jax
import jax.numpy as jnp
from jax import lax
from jax.experimental import pallas as pl
from jax.experimental.pallas import tpu as pltpu

F32 = jnp.float32
BF16 = jnp.bfloat16

D_MODEL = 1024
PAST_LEN = 16384
MEM_LEN = 256
CONV_CH = 512
CONV_WIDTH = 31
SWA_HEADS = 8
SWA_KV_HEADS = 2
SWA_HEAD_DIM = 64
WINDOW = 128
SWA_SCALE = SWA_HEAD_DIM ** -0.5
MEM_HEADS = 4
MEM_HEAD_DIM = 128
MEM_SCALE = MEM_HEAD_DIM ** -0.5
REL_BUCKETS = 32
REL_MAX_DIST = 128
PEER_HEADS = 8
PEER_N_KEYS = 128
PEER_DK_HALF = 64
PEER_TOPK = 16
EPS = 1e-6
NEG_INF = -1e30

W_GLU = 2 * CONV_CH
W_Q = SWA_HEADS * SWA_HEAD_DIM
W_KV = SWA_KV_HEADS * SWA_HEAD_DIM
W_QM = MEM_HEADS * MEM_HEAD_DIM
W_PROJ = W_GLU + W_Q + 2 * W_KV + W_QM

VMEM_LIMIT_BYTES = 56 * 1024 * 1024
LANES = 128
SUBLANES = 8
BF16_ROWS = 16
CONV_HALO = 32


def _params(*sem, flags=None):
    return pltpu.CompilerParams(dimension_semantics=sem, vmem_limit_bytes=VMEM_LIMIT_BYTES, flags=flags)


def _rms(x, g):
    return x * lax.rsqrt(jnp.mean(x * x, axis=-1, keepdims=True) + EPS) * g


def _dot(a, b):
    return jnp.dot(a, b, preferred_element_type=F32)


def _dot_nt(a, b):
    return lax.dot_general(a, b, (((1,), (1,)), ((), ())), preferred_element_type=F32)


def _const_spec(shape):
    zeros = (0,) * len(shape)
    return pl.BlockSpec(shape, lambda *_: zeros)


def _norm_matmul_kernel(x_ref, g_ref, w_ref, o_ref):
    o_ref[...] = _dot(_rms(x_ref[...], g_ref[...]).astype(BF16), w_ref[...])


def _norm_matmul(x, g, w, tm):
    t, n = x.shape[0], w.shape[1]
    return pl.pallas_call(
        _norm_matmul_kernel,
        grid=(t // tm,),
        in_specs=[pl.BlockSpec((tm, D_MODEL), lambda i: (i, 0)), _const_spec((1, D_MODEL)),
                  _const_spec((D_MODEL, n))],
        out_specs=pl.BlockSpec((tm, n), lambda i: (i, 0)),
        out_shape=jax.ShapeDtypeStruct((t, n), F32),
        compiler_params=_params("parallel"),
        name="memkv",
    )(x, g, w)


def _in_proj_kernel(x_ref, g_ref, w_ref, u_ref, q_ref, k_ref, v_ref, qm_ref):
    z = _dot(_rms(x_ref[...], g_ref[...]).astype(BF16), w_ref[...])
    a, b = z[:, :CONV_CH], z[:, CONV_CH:W_GLU]
    u_ref[...] = a * jax.nn.sigmoid(b)
    c = W_GLU
    q_ref[...] = (z[:, c:c + W_Q] * SWA_SCALE).astype(BF16)
    c += W_Q
    k_ref[...] = z[:, c:c + W_KV]
    c += W_KV
    v_ref[...] = z[:, c:c + W_KV]
    c += W_KV
    qm_ref[...] = z[:, c:c + W_QM].astype(BF16)


def _in_proj(x, g, w, tm):
    t = x.shape[0]
    row = lambda n: pl.BlockSpec((tm, n), lambda i: (i, 0))
    return pl.pallas_call(
        _in_proj_kernel,
        grid=(t // tm,),
        in_specs=[row(D_MODEL), _const_spec((1, D_MODEL)), _const_spec((D_MODEL, W_PROJ))],
        out_specs=[row(CONV_CH), row(W_Q), row(W_KV), row(W_KV), row(W_QM)],
        out_shape=[jax.ShapeDtypeStruct((t, CONV_CH), F32), jax.ShapeDtypeStruct((t, W_Q), BF16),
                   jax.ShapeDtypeStruct((t, W_KV), F32), jax.ShapeDtypeStruct((t, W_KV), F32),
                   jax.ShapeDtypeStruct((t, W_QM), BF16)],
        compiler_params=_params("parallel"),
        name="in_proj",
    )(x, g, w)


def _ln_silu(y, g, b):
    mu = jnp.mean(y, axis=-1, keepdims=True)
    var = jnp.mean(jnp.square(y - mu), axis=-1, keepdims=True)
    y = (y - mu) * lax.rsqrt(var + EPS) * g + b
    return y * jax.nn.sigmoid(y)


def _softmax_rows(s):
    e = jnp.exp(s - jnp.max(s, axis=-1, keepdims=True))
    return e / jnp.sum(e, axis=-1, keepdims=True)


def _branches_prompt_kernel(sinks_ref, u_ref, uh_ref, q_ref, k_ref, kp_ref, v_ref, vp_ref, qm_ref,
                            mk_ref, mv_ref, bias_ref, dww_ref, dwb_ref, lng_ref, lnb_ref,
                            conv_ref, swa_ref, mem_ref, ubuf, ushift, kbuf, vbuf, *, tq):
    i = pl.program_id(1)
    first = i == 0

    ubuf[0:CONV_HALO, :] = jnp.where(first, 0.0, uh_ref[...])
    ubuf[CONV_HALO:CONV_HALO + tq, :] = u_ref[...]
    nshift = ushift.shape[1]
    for b in range(1, SUBLANES):
        ushift[b] = ubuf[b:b + nshift, :]
    rb = 64
    off = CONV_HALO - (CONV_WIDTH - 1)
    for r in range(tq // rb):
        acc = jnp.broadcast_to(dwb_ref[...], (rb, CONV_CH))
        for j in range(CONV_WIDTH):
            a, b = divmod(off + j, SUBLANES)
            start = r * rb + a * SUBLANES
            rows = ubuf[start:start + rb, :] if b == 0 else ushift[b, start:start + rb, :]
            acc = acc + rows * dww_ref[j:j + 1, :]
        conv_ref[r * rb:(r + 1) * rb, :] = _ln_silu(acc, lng_ref[...], lnb_ref[...]).astype(BF16)

    lane = lax.broadcasted_iota(jnp.int32, (WINDOW + tq, LANES), 1)
    lo = lane < SWA_HEAD_DIM
    for src_ref, prev_ref, buf in ((k_ref, kp_ref, kbuf), (v_ref, vp_ref, vbuf)):
        full = jnp.concatenate([jnp.where(first, 0.0, prev_ref[...]), src_ref[...]], axis=0)
        rolled = pltpu.roll(full, SWA_HEAD_DIM, 1)
        buf[0] = jnp.where(lo, full, 0.0).astype(BF16)
        buf[1] = jnp.where(lo, 0.0, rolled).astype(BF16)
        buf[2] = jnp.where(lo, rolled, 0.0).astype(BF16)
        buf[3] = jnp.where(lo, 0.0, full).astype(BF16)

    qi = lax.broadcasted_iota(jnp.int32, (WINDOW, 2 * WINDOW), 0)
    ki = lax.broadcasted_iota(jnp.int32, (WINDOW, 2 * WINDOW), 1)
    dist = WINDOW + qi - ki
    band = (dist >= 0) & (dist <= WINDOW)
    nqb = tq // WINDOW
    for jb in range(nqb):
        r0 = jb * WINDOW
        kmin = jnp.where(i * nqb + jb > 0, 0, WINDOW)
        mask = band & (ki >= kmin)
        for p in range(SWA_HEADS // 2):
            g = p // 2
            qp = q_ref[r0:r0 + WINDOW, LANES * p:LANES * (p + 1)]
            o = None
            for half in range(2):
                h = 2 * p + half
                s = _dot_nt(qp, kbuf[2 * g + half, r0:r0 + 2 * WINDOW, :]) + bias_ref[h]
                s = jnp.where(mask, s, NEG_INF)
                sink = sinks_ref[h]
                m = jnp.maximum(jnp.max(s, axis=-1, keepdims=True), sink)
                pr = jnp.exp(s - m)
                pr = pr / (jnp.sum(pr, axis=-1, keepdims=True) + jnp.exp(sink - m))
                t = _dot(pr.astype(BF16), vbuf[2 * g + half, r0:r0 + 2 * WINDOW, :])
                o = t if o is None else o + t
            swa_ref[r0:r0 + WINDOW, LANES * p:LANES * (p + 1)] = o.astype(BF16)

    for hm in range(MEM_HEADS):
        sl = slice(hm * MEM_HEAD_DIM, (hm + 1) * MEM_HEAD_DIM)
        w = _softmax_rows(_dot_nt(qm_ref[:, sl], mk_ref[:, sl].astype(BF16)) * MEM_SCALE)
        mem_ref[:, sl] = _dot(w.astype(BF16), mv_ref[:, sl].astype(BF16)).astype(BF16)


def _branches_prompt(sinks, u, q, k, v, qm, mkv, bias, dww, dwb, lng, lnb, batch, seq, tq):
    t = batch * seq
    nq = seq // tq
    row = lambda n: pl.BlockSpec((tq, n), lambda b, i: (b * nq + i, 0))
    halo = lambda rows, n: pl.BlockSpec(
        (rows, n), lambda b, i: (jnp.maximum(b * (seq // rows) + i * (tq // rows) - 1, 0), 0))
    return pl.pallas_call(
        functools.partial(_branches_prompt_kernel, tq=tq),
        grid=(batch, nq),
        in_specs=[pl.BlockSpec(memory_space=pltpu.SMEM),
                  row(CONV_CH), halo(CONV_HALO, CONV_CH), row(W_Q),
                  row(W_KV), halo(WINDOW, W_KV), row(W_KV), halo(WINDOW, W_KV), row(W_QM),
                  pl.BlockSpec((MEM_LEN, W_QM), lambda b, i: (b, 0)),
                  pl.BlockSpec((MEM_LEN, W_QM), lambda b, i: (b, 1)),
                  _const_spec((SWA_HEADS, WINDOW, 2 * WINDOW)),
                  _const_spec((CONV_WIDTH, CONV_CH)), _const_spec((1, CONV_CH)),
                  _const_spec((1, CONV_CH)), _const_spec((1, CONV_CH))],
        out_specs=[row(CONV_CH), row(W_Q), row(W_QM)],
        out_shape=[jax.ShapeDtypeStruct((t, CONV_CH), BF16), jax.ShapeDtypeStruct((t, W_Q), BF16),
                   jax.ShapeDtypeStruct((t, W_QM), BF16)],
        scratch_shapes=[pltpu.VMEM((CONV_HALO + tq, CONV_CH), F32),
                        pltpu.VMEM((SUBLANES, CONV_HALO + tq - SUBLANES, CONV_CH), F32),
                        pltpu.VMEM((4, WINDOW + tq, LANES), BF16),
                        pltpu.VMEM((4, WINDOW + tq, LANES), BF16)],
        compiler_params=_params("parallel", "arbitrary"),
        name="branches_prompt",
    )(sinks, u, u, q, k, k, v, v, qm, mkv, mkv, bias, dww, dwb, lng, lnb)


def _branches_sample_kernel(sinks_ref, u_ref, cc_ref, qx_ref, kn_ref, vn_ref, ck_ref, cv_ref, qm_ref,
                            cmk_ref, cmv_ref, bias_ref, bias0_ref, dww_ref, dwb_ref, lng_ref, lnb_ref,
                            conv_ref, swa_ref, mem_ref, *, nb):
    hist = CONV_WIDTH - 1
    y = jnp.sum(cc_ref[...] * dww_ref[0:hist, :][None], axis=1)
    y = y + u_ref[...] * dww_ref[hist:hist + 1, :] + dwb_ref[...]
    conv_ref[...] = _ln_silu(y, lng_ref[...], lnb_ref[...]).astype(BF16)

    sink = sinks_ref[...]
    for n in range(nb):
        qx = qx_ref[n]
        s = _dot_nt(qx, ck_ref[n].astype(BF16)) + bias_ref[...]
        kn = kn_ref[n:n + 1, :].astype(BF16).astype(F32)
        s_new = jnp.sum(qx.astype(F32) * kn, axis=-1, keepdims=True) + bias0_ref[...]
        m = jnp.maximum(jnp.maximum(jnp.max(s, axis=-1, keepdims=True), s_new), sink)
        pr, pr_new = jnp.exp(s - m), jnp.exp(s_new - m)
        den = jnp.sum(pr, axis=-1, keepdims=True) + pr_new + jnp.exp(sink - m)
        vn = vn_ref[n:n + 1, :].astype(BF16).astype(F32)
        o = _dot((pr / den).astype(BF16), cv_ref[n].astype(BF16))
        swa_ref[n] = o + (pr_new / den).astype(BF16).astype(F32) * vn

        bf = lambda a: a.astype(BF16).astype(F32)
        sm = jnp.sum(bf(cmk_ref[n]) * qm_ref[n].astype(F32)[None], axis=-1, keepdims=True) * MEM_SCALE
        e = jnp.exp(sm - jnp.max(sm, axis=0, keepdims=True))
        w = e / jnp.sum(e, axis=0, keepdims=True)
        mem_ref[n] = jnp.sum(bf(w) * bf(cmv_ref[n]), axis=0).astype(BF16)


def _branches_sample(sinks, u, cache_conv, qx, kn, vn, ck, cv, qm, cmk, cmv, bias, bias0,
                     dww, dwb, lng, lnb, nb):
    n = u.shape[0]
    row = lambda c: pl.BlockSpec((nb, c), lambda i: (i, 0))
    blk3 = lambda a, c: pl.BlockSpec((nb, a, c), lambda i: (i, 0, 0))
    return pl.pallas_call(
        functools.partial(_branches_sample_kernel, nb=nb),
        grid=(n // nb,),
        in_specs=[_const_spec((SWA_HEADS, 1)),
                  row(CONV_CH), blk3(CONV_WIDTH - 1, CONV_CH), blk3(SWA_HEADS, LANES),
                  row(W_KV), row(W_KV), blk3(WINDOW, W_KV), blk3(WINDOW, W_KV),
                  blk3(MEM_HEADS, MEM_HEAD_DIM),
                  pl.BlockSpec((nb, MEM_LEN, MEM_HEADS, MEM_HEAD_DIM), lambda i: (i, 0, 0, 0)),
                  pl.BlockSpec((nb, MEM_LEN, MEM_HEADS, MEM_HEAD_DIM), lambda i: (i, 0, 0, 0)),
                  _const_spec((SWA_HEADS, WINDOW)), _const_spec((SWA_HEADS, 1)),
                  _const_spec((CONV_WIDTH, CONV_CH)), _const_spec((1, CONV_CH)),
                  _const_spec((1, CONV_CH)), _const_spec((1, CONV_CH))],
        out_specs=[row(CONV_CH), blk3(SWA_HEADS, LANES), blk3(MEM_HEADS, MEM_HEAD_DIM)],
        out_shape=[jax.ShapeDtypeStruct((n, CONV_CH), BF16),
                   jax.ShapeDtypeStruct((n, SWA_HEADS, LANES), F32),
                   jax.ShapeDtypeStruct((n, MEM_HEADS, MEM_HEAD_DIM), BF16)],
        compiler_params=_params("parallel"),
        name="branches_sample",
    )(sinks, u, cache_conv, qx, kn, vn, ck, cv, qm, cmk, cmv, bias, bias0, dww, dwb, lng, lnb)


def _merge_kernel(x_ref, g_ref, conv_ref, swa_ref, mem_ref, wg_ref, wco_ref, wso_ref, wmo_ref, wo_ref,
                  o_ref):
    x = x_ref[...]
    h = _rms(x, g_ref[...]).astype(BF16)
    merged = None
    for br, (a_ref, w_ref) in enumerate(((conv_ref, wco_ref), (swa_ref, wso_ref), (mem_ref, wmo_ref))):
        gate = jax.nn.sigmoid(_dot(h, wg_ref[:, br * D_MODEL:(br + 1) * D_MODEL]))
        term = gate * _dot(a_ref[...], w_ref[...])
        merged = term if merged is None else merged + term
    o_ref[...] = x + _dot(merged.astype(BF16), wo_ref[...])


def _merge(x, g, conv, swa, mem, wg, wco, wso, wmo, wo, tm):
    t = x.shape[0]
    row = lambda n: pl.BlockSpec((tm, n), lambda i: (i, 0))
    return pl.pallas_call(
        _merge_kernel,
        grid=(t // tm,),
        in_specs=[row(D_MODEL), _const_spec((1, D_MODEL)), row(CONV_CH), row(W_Q), row(W_QM),
                  _const_spec((D_MODEL, 3 * D_MODEL)), _const_spec((CONV_CH, D_MODEL)),
                  _const_spec((W_Q, D_MODEL)), _const_spec((W_QM, D_MODEL)),
                  _const_spec((D_MODEL, D_MODEL))],
        out_specs=row(D_MODEL),
        out_shape=jax.ShapeDtypeStruct((t, D_MODEL), F32),
        compiler_params=_params("parallel"),
        name="merge",
    )(x, g, conv, swa, mem, wg, wco, wso, wmo, wo)


def _gelu_x2(x):
    return x * (1.0 + lax.erf(x * (2.0 ** -0.5)))


def _top_values(arrs, count, with_rank=False):
    out = []
    ranks = [jnp.full(a.shape, float(count), F32) for a in arrs] if with_rank else None
    for it in range(count):
        m = jnp.max(functools.reduce(jnp.maximum, arrs), axis=0, keepdims=True)
        out.append(m)
        hit = [a == m for a in arrs]
        if with_rank:
            ranks = [jnp.where(hh, float(it), rk) for hh, rk in zip(hit, ranks)]
        arrs = [jnp.where(hh, -jnp.inf, a) for hh, a in zip(hit, arrs)]
    return (out, ranks) if with_rank else out


def _oddeven_merge_sort(lo, hi):
    def merge(lo, hi, r):
        step = 2 * r
        if step < hi - lo:
            yield from merge(lo, hi, step)
            yield from merge(lo + r, hi, step)
            yield from ((i, i + r) for i in range(lo + r, hi - r, step))
        else:
            yield (lo, lo + r)

    if hi > lo:
        mid = lo + (hi - lo) // 2
        yield from _oddeven_merge_sort(lo, mid)
        yield from _oddeven_merge_sort(mid + 1, hi)
        yield from merge(lo, hi, 1)


def _compare_exchange(a, i, j):
    if a[j] is None:
        return
    if a[i] is None:
        a[i], a[j] = a[j], None
    else:
        a[i], a[j] = jnp.maximum(a[i], a[j]), jnp.minimum(a[i], a[j])


def _top16_sorted(blocks):
    n = PEER_TOPK
    assert n // 2 < len(blocks) <= n and blocks[0].shape[0] == SUBLANES
    a = list(blocks) + [None] * (n - len(blocks))
    larger = lambda x, y: x if y is None else y if x is None else jnp.maximum(x, y)
    for i, j in _oddeven_merge_sort(0, n - 1):
        _compare_exchange(a, i, j)
    for shift in (4, 2, 1):
        b = [None if x is None else pltpu.roll(x, shift, 0) for x in a]
        a = [larger(a[i], b[n - 1 - i]) for i in range(n)]
        d = n // 2
        while d:
            for i in range(n):
                if not i & d:
                    _compare_exchange(a, i, i + d)
            d //= 2
    return a


def _rank_in_sorted(x, top):
    assert len(top) == 16
    rank, lo = None, [0] * 1
    conds = []
    for width in (8, 4, 2, 1):
        idx = [b + width - 1 for b in lo]
        thr = [top[i] for i in idx]
        for c in reversed(conds):
            thr = [jnp.where(c, thr[2 * k], thr[2 * k + 1]) for k in range(len(thr) // 2)]
        c = x >= thr[0]
        step = jnp.where(c, 0.0, float(width))
        rank = step if rank is None else rank + step
        conds.append(c)
        lo = [b + off for b in lo for off in (0, width)]
    return jnp.where(x >= top[15], rank, 16.0)


def _peer_route(h, qt, keys_ref, l_s, e0_s, r1_s, e1_s, sv0_s, sv1_s):
    nk, tb = PEER_N_KEYS, qt.shape[1]
    r = 2 * h * PEER_DK_HALF
    for lg in range(tb // LANES):
        lanes = slice(lg * LANES, (lg + 1) * LANES)
        blocks = lambda a: [a[SUBLANES * j:SUBLANES * (j + 1), :] for j in range(nk // SUBLANES)]
        s0 = blocks(_dot(keys_ref[h, 0], qt[r:r + PEER_DK_HALF, lanes]))
        s1 = blocks(_dot(keys_ref[h, 1], qt[r + PEER_DK_HALF:r + 2 * PEER_DK_HALF, lanes]))
        top0, top1 = _top16_sorted(s0), _top16_sorted(s1)
        for j in range(PEER_TOPK):
            sv0_s[j:j + 1, lanes] = top0[j][0:1]
            sv1_s[j:j + 1, lanes] = top1[j][0:1]
        a0, a1 = sv0_s[0:8, lanes], sv0_s[8:16, lanes]
        b0, b1 = sv1_s[0:8, lanes], sv1_s[8:16, lanes]
        cands = [a0[0:1] + b0, a0[0:1] + b1] + [a0[a:a + 1] + b0 for a in range(1, 8)] + [a1 + b0[0:1]]
        best = _top16_sorted(cands)
        sel = [cd >= best[PEER_TOPK - 1] for cd in cands]
        z = functools.reduce(jnp.add, [jnp.where(sl, jnp.exp(cd - best[0]), 0.0) for sl, cd in zip(sel, cands)])
        z = jnp.sum(z, axis=0, keepdims=True)
        cnt = [jnp.sum(jnp.where(sl, 1.0, 0.0), axis=0, keepdims=True) for sl in sel[:9]]
        counts = [cnt[0] + cnt[1]] + cnt[2:9]
        tail = jnp.where(sel[9], 1.0, 0.0)
        counts = [jnp.broadcast_to(counts[a] if a < 8 else tail[a - 8:a - 7], (SUBLANES, LANES))
                  for a in range(PEER_TOPK)]
        lrow, rank1 = [], []
        for x0, x1 in zip(s0, s1):
            lx = jnp.zeros_like(x0)
            for a in reversed(range(PEER_TOPK)):
                lx = jnp.where(x0 >= top0[a], counts[a], lx)
            lrow.append(lx)
            rank1.append(_rank_in_sorted(x1, top1))
        l_s[h, :, lanes] = jnp.concatenate(lrow, axis=0)
        e0_s[h, :, lanes] = jnp.exp(jnp.concatenate(s0, axis=0) - top0[0][0:1]) / z * 0.5
        r1_s[h, :, lanes] = jnp.concatenate(rank1, axis=0).astype(BF16)
        e1_s[h, :, lanes] = jnp.exp(jnp.concatenate(s1, axis=0) - top1[0][0:1]).astype(BF16)


def _peer_gate_chunk(ci, at_ref, ct_ref, lrow_s, erow_s, r1_s, e1_s, cb):
    nk, tb = PEER_N_KEYS, ct_ref.shape[1]
    tile = (nk // BF16_ROWS, BF16_ROWS, tb)
    row = lambda ref, h, j: jnp.broadcast_to(ref[ci, h, j:j + 1, :], tile[1:]).astype(BF16)[None]
    for j in range(cb):
        g = None
        for h in range(PEER_HEADS):
            term = jnp.where(r1_s[h].reshape(tile) < row(lrow_s, h, j), e1_s[h].reshape(tile),
                             jnp.zeros((), BF16)) * row(erow_s, h, j)
            g = term if g is None else g + term
        rows = slice((ci * cb + j) * nk, (ci * cb + j + 1) * nk)
        ct_ref[rows, :] = (_gelu_x2(at_ref[j * nk:(j + 1) * nk, :].reshape(tile)) * g).reshape(nk, tb)


def _peer_kernel(x_ref, g2_ref, gf_ref, wqt_ref, keys_ref, wda_ref, wdb_ref, wut_ref, o_ref,
                 h_s, l_s, e0_s, r1_s, e1_s, sv0_s, sv1_s, lrow_s, erow_s, at0_s, at1_s, ct_s, acc_s, *, cb):
    s = pl.program_id(1)

    @pl.when(s == 0)
    def _first():
        h_s[...] = _rms(x_ref[...], g2_ref[...]).T.astype(BF16)
        qt = _dot(wqt_ref[...], h_s[...]).astype(BF16)
        for h in range(PEER_HEADS):
            _peer_route(h, qt, keys_ref, l_s, e0_s, r1_s, e1_s, sv0_s, sv1_s)
        at0_s[...] = _dot(wda_ref[...], h_s[...]).astype(BF16)
        acc_s[...] = jnp.zeros_like(acc_s)

    @pl.when(s > 0)
    def _steady():
        for ci in range(2):
            base = pl.multiple_of((2 * s - 2 + ci) * cb, cb)
            for h in range(PEER_HEADS):
                lrow_s[ci, h] = l_s[h, pl.ds(base, cb), :]
                erow_s[ci, h] = e0_s[h, pl.ds(base, cb), :]
        gate = functools.partial(_peer_gate_chunk, ct_ref=ct_s, lrow_s=lrow_s, erow_s=erow_s,
                                 r1_s=r1_s, e1_s=e1_s, cb=cb)
        gate(0, at0_s)
        at1_s[...] = _dot(wda_ref[...], h_s[...]).astype(BF16)

        @pl.when(s < pl.num_programs(1) - 1)
        def _next_chunk():
            at0_s[...] = _dot(wdb_ref[...], h_s[...]).astype(BF16)

        gate(1, at1_s)
        acc_s[...] += _dot(wut_ref[...], ct_s[...])

    @pl.when(s == pl.num_programs(1) - 1)
    def _last():
        y = x_ref[...] + acc_s[...].T
        o_ref[...] = _rms(y, gf_ref[...])


def _peer(x, g2, gf, wqt, keys, wd, wut, tb, cb):
    t = x.shape[0]
    ec = cb * PEER_N_KEYS
    nch = wd.shape[0] // ec
    assert nch % 2 == 0 and tb % LANES == 0
    stat = lambda dt: pltpu.VMEM((PEER_HEADS, PEER_N_KEYS, tb), dt)
    return pl.pallas_call(
        functools.partial(_peer_kernel, cb=cb),
        grid=(t // tb, nch // 2 + 1),
        in_specs=[pl.BlockSpec((tb, D_MODEL), lambda i, s: (i, 0)),
                  _const_spec((1, D_MODEL)), _const_spec((1, D_MODEL)),
                  _const_spec((D_MODEL, D_MODEL)),
                  _const_spec((PEER_HEADS, 2, PEER_N_KEYS, PEER_DK_HALF)),
                  pl.BlockSpec((ec, D_MODEL), lambda i, s: (jnp.maximum(2 * s - 1, 0), 0)),
                  pl.BlockSpec((ec, D_MODEL), lambda i, s: (jnp.minimum(2 * s, nch - 1), 0)),
                  pl.BlockSpec((D_MODEL, 2 * ec), lambda i, s: (0, jnp.maximum(s - 1, 0)))],
        out_specs=pl.BlockSpec((tb, D_MODEL), lambda i, s: (i, 0)),
        out_shape=jax.ShapeDtypeStruct((t, D_MODEL), F32),
        scratch_shapes=[pltpu.VMEM((D_MODEL, tb), BF16), stat(F32), stat(F32), stat(BF16), stat(BF16),
                        pltpu.VMEM((PEER_TOPK, tb), F32), pltpu.VMEM((PEER_TOPK, tb), F32),
                        pltpu.VMEM((2, PEER_HEADS, cb, tb), F32), pltpu.VMEM((2, PEER_HEADS, cb, tb), F32),
                        pltpu.VMEM((ec, tb), BF16), pltpu.VMEM((ec, tb), BF16),
                        pltpu.VMEM((2 * ec, tb), BF16), pltpu.VMEM((D_MODEL, tb), F32)],
        compiler_params=_params("parallel", "arbitrary"),
        name="peer",
    )(x, g2, gf, wqt, keys, wd, wd, wut)


def _rel_bucket(dist):
    n = jnp.maximum(dist, 0)
    max_exact = REL_BUCKETS // 2
    nf = jnp.maximum(n, 1).astype(F32)
    large = max_exact + (jnp.log(nf / max_exact) / jnp.log(REL_MAX_DIST / max_exact)
                         * (REL_BUCKETS - max_exact)).astype(jnp.int32)
    return jnp.where(n < max_exact, n, jnp.minimum(large, REL_BUCKETS - 1))


TOKEN_TILE = 512
SAMPLE_TILE = 8
PEER_CHUNK_ROW_BLOCKS = 8


def _tile(t, cap=TOKEN_TILE):
    tm = min(t, cap)
    assert t % tm == 0, (t, tm)
    return tm


def kernel(x_prompt, x_sample, cache_conv, cache_swa_k, cache_swa_v, cache_mem_k, cache_mem_v, mem_prompt, rel_bias_table, norm1_g, w_in, conv_dw_w, conv_dw_b, conv_ln_g, conv_ln_b, w_conv_out, swa_sinks, w_swa_out, mem_norm_g, w_mem_kv, w_mem_out, w_out, norm2_g, peer_w_q, peer_keys, peer_w_down, peer_w_up, final_norm_g):
    assert w_in.shape[0] == 1, "single layer"
    batch, seq, _ = x_prompt.shape
    nsamp = x_sample.shape[0]
    assert x_sample.shape[1] == 1 and seq % WINDOW == 0
    row = lambda a: a.reshape(1, -1)

    w_proj = w_in[0, :, :W_PROJ].astype(BF16)
    w_gate = w_in[0, :, W_PROJ:].astype(BF16)
    g1, g2, gf = row(norm1_g[0]), row(norm2_g[0]), row(final_norm_g)
    dww, dwb = conv_dw_w[0], row(conv_dw_b[0])
    lng, lnb = row(conv_ln_g[0]), row(conv_ln_b[0])
    wco, wso = w_conv_out[0].astype(BF16), w_swa_out[0].astype(BF16)
    wmo, wo = w_mem_out[0].astype(BF16), w_out[0].astype(BF16)
    wqt = peer_w_q[0].T.astype(BF16)
    keys = peer_keys[0].astype(BF16)
    wd = peer_w_down.reshape(-1, D_MODEL).astype(BF16)
    wut = peer_w_up.reshape(-1, D_MODEL).astype(BF16).T
    sinks = swa_sinks[0]

    qi = jnp.arange(WINDOW)[:, None]
    ki = jnp.arange(2 * WINDOW)[None, :]
    def table_rows(dist):
        onehot = (_rel_bucket(dist)[..., None] == jnp.arange(REL_BUCKETS)).astype(F32)
        return jnp.einsum("...b,bh->h...", onehot, rel_bias_table.astype(F32),
                          precision=lax.Precision.HIGHEST)

    bias_p = table_rows(WINDOW + qi - ki)
    bias_s = table_rows(WINDOW - jnp.arange(WINDOW))
    bias_0 = table_rows(jnp.zeros((1,), jnp.int32))

    xp = x_prompt.reshape(batch * seq, D_MODEL)
    mkv = _norm_matmul(mem_prompt.reshape(batch * MEM_LEN, D_MODEL), row(mem_norm_g[0]),
                       w_mem_kv[0].astype(BF16), MEM_LEN)
    u_p, q_p, k_p, v_p, qm_p = _in_proj(xp, g1, w_proj, _tile(batch * seq))
    tq = _tile(seq)
    conv_p, swa_p, mem_p = _branches_prompt(sinks, u_p, q_p, k_p, v_p, qm_p, mkv, bias_p,
                                            dww, dwb, lng, lnb, batch, seq, tq)
    x2_p = _merge(xp, g1, conv_p, swa_p, mem_p, w_gate, wco, wso, wmo, wo, _tile(batch * seq))
    y_p = _peer(x2_p, g2, gf, wqt, keys, wd, wut, _tile(batch * seq), PEER_CHUNK_ROW_BLOCKS)

    xs = x_sample.reshape(nsamp, D_MODEL)
    u_s, q_s, k_s, v_s, qm_s = _in_proj(xs, g1, w_proj, _tile(nsamp))
    q4 = q_s.reshape(nsamp, SWA_KV_HEADS, SWA_HEADS // SWA_KV_HEADS, SWA_HEAD_DIM)
    zq = jnp.zeros_like(q4[:, 0])
    qx = jnp.concatenate([jnp.concatenate([q4[:, 0], zq], -1), jnp.concatenate([zq, q4[:, 1]], -1)], 1)
    ck = cache_swa_k.reshape(nsamp, WINDOW, W_KV)
    cv = cache_swa_v.reshape(nsamp, WINDOW, W_KV)
    cmk = cache_mem_k.reshape(nsamp, MEM_LEN, MEM_HEADS, MEM_HEAD_DIM)
    cmv = cache_mem_v.reshape(nsamp, MEM_LEN, MEM_HEADS, MEM_HEAD_DIM)
    cconv = cache_conv.reshape(nsamp, CONV_WIDTH - 1, CONV_CH)
    conv_s, swa_x, mem_x = _branches_sample(sinks.reshape(SWA_HEADS, 1), u_s, cconv, qx, k_s, v_s, ck, cv,
                                            qm_s.reshape(nsamp, MEM_HEADS, MEM_HEAD_DIM), cmk, cmv,
                                            bias_s, bias_0, dww, dwb, lng, lnb, _tile(nsamp, SAMPLE_TILE))
    mem_s = mem_x.reshape(nsamp, W_QM)
    sx = swa_x.reshape(nsamp, SWA_KV_HEADS, SWA_HEADS // SWA_KV_HEADS, SWA_KV_HEADS, SWA_HEAD_DIM)
    swa_s = jnp.stack([sx[:, g, :, g] for g in range(SWA_KV_HEADS)], 1).reshape(nsamp, W_Q).astype(BF16)
    x2_s = _merge(xs, g1, conv_s, swa_s, mem_s, w_gate, wco, wso, wmo, wo, _tile(nsamp))
    y_s = _peer(x2_s, g2, gf, wqt, keys, wd, wut, _tile(nsamp), PEER_CHUNK_ROW_BLOCKS)

    hist = CONV_WIDTH - 1
    kv_shape = (SWA_KV_HEADS, SWA_HEAD_DIM)
    mkv5 = mkv.reshape(batch, MEM_LEN, 2, MEM_HEADS, MEM_HEAD_DIM)
    conv_state_p = u_p.reshape(batch, seq, CONV_CH)[:, -hist:]
    swa_k_p = k_p.reshape(batch, seq, *kv_shape)[:, -WINDOW:]
    swa_v_p = v_p.reshape(batch, seq, *kv_shape)[:, -WINDOW:]
    conv_state_s = jnp.concatenate([cache_conv[0][:, 1:], u_s[:, None, :]], axis=1)
    swa_k_s = jnp.concatenate([cache_swa_k[0][:, 1:], k_s.reshape(nsamp, 1, *kv_shape)], axis=1)
    swa_v_s = jnp.concatenate([cache_swa_v[0][:, 1:], v_s.reshape(nsamp, 1, *kv_shape)], axis=1)
    return (y_p.reshape(batch, seq, D_MODEL), y_s.reshape(nsamp, 1, D_MODEL),
            conv_state_p[None], swa_k_p[None], swa_v_p[None],
            mkv5[:, :, 0][None], mkv5[:, :, 1][None],
            conv_state_s[None], swa_k_s[None], swa_v_s[None])
```

```python
import functools

import jax
import jax.numpy as jnp
from jax import lax
from jax.experimental import pallas as pl
from jax.experimental.pallas import tpu as pltpu

F32 = jnp.float32
BF16 = jnp.bfloat16

D_MODEL = 1024
PAST_LEN = 16384
MEM_LEN = 256
CONV_CH = 512
CONV_WIDTH = 31
SWA_HEADS = 8
SWA_KV_HEADS = 2
SWA_HEAD_DIM = 64
WINDOW = 128
SWA_SCALE = SWA_HEAD_DIM ** -0.5
MEM_HEADS = 4
MEM_HEAD_DIM = 128
MEM_SCALE = MEM_HEAD_DIM ** -0.5
REL_BUCKETS = 32
REL_MAX_DIST = 128
PEER_HEADS = 8
PEER_N_KEYS = 128
PEER_DK_HALF = 64
PEER_TOPK = 16
EPS = 1e-6
NEG_INF = -1e30

W_GLU = 2 * CONV_CH
W_Q = SWA_HEADS * SWA_HEAD_DIM
W_KV = SWA_KV_HEADS * SWA_HEAD_DIM
W_QM = MEM_HEADS * MEM_HEAD_DIM
W_PROJ = W_GLU + W_Q + 2 * W_KV + W_QM

VMEM_LIMIT_BYTES = 56 * 1024 * 1024
LANES = 128
SUBLANES = 8
BF16_ROWS = 16
CONV_HALO = 32


def _params(*sem, flags=None):
    return pltpu.CompilerParams(dimension_semantics=sem, vmem_limit_bytes=VMEM_LIMIT_BYTES, flags=flags)


def _rms(x, g):
    return x * lax.rsqrt(jnp.mean(x * x, axis=-1, keepdims=True) + EPS) * g


def _dot(a, b):
    return jnp.dot(a, b, preferred_element_type=F32)


def _dot_nt(a, b):
    return lax.dot_general(a, b, (((1,), (1,)), ((), ())), preferred_element_type=F32)


def _const_spec(shape):
    zeros = (0,) * len(shape)
    return pl.BlockSpec(shape, lambda *_: zeros)


def _norm_matmul_kernel(x_ref, g_ref, w_ref, o_ref):
    o_ref[...] = _dot(_rms(x_ref[...], g_ref[...]).astype(BF16), w_ref[...])


def _norm_matmul(x, g, w, tm):
    t, n = x.shape[0], w.shape[1]
    return pl.pallas_call(
        _norm_matmul_kernel,
        grid=(t // tm,),
        in_specs=[pl.BlockSpec((tm, D_MODEL), lambda i: (i, 0)), _const_spec((1, D_MODEL)),
                  _const_spec((D_MODEL, n))],
        out_specs=pl.BlockSpec((tm, n), lambda i: (i, 0)),
        out_shape=jax.ShapeDtypeStruct((t, n), F32),
        compiler_params=_params("parallel"),
        name="memkv",
    )(x, g, w)


def _in_proj_kernel(x_ref, g_ref, w_ref, u_ref, q_ref, k_ref, v_ref, qm_ref):
    z = _dot(_rms(x_ref[...], g_ref[...]).astype(BF16), w_ref[...])
    a, b = z[:, :CONV_CH], z[:, CONV_CH:W_GLU]
    u_ref[...] = a * jax.nn.sigmoid(b)
    c = W_GLU
    q_ref[...] = (z[:, c:c + W_Q] * SWA_SCALE).astype(BF16)
    c += W_Q
    k_ref[...] = z[:, c:c + W_KV]
    c += W_KV
    v_ref[...] = z[:, c:c + W_KV]
    c += W_KV
    qm_ref[...] = z[:, c:c + W_QM].astype(BF16)


def _in_proj(x, g, w, tm):
    t = x.shape[0]
    row = lambda n: pl.BlockSpec((tm, n), lambda i: (i, 0))
    return pl.pallas_call(
        _in_proj_kernel,
        grid=(t // tm,),
        in_specs=[row(D_MODEL), _const_spec((1, D_MODEL)), _const_spec((D_MODEL, W_PROJ))],
        out_specs=[row(CONV_CH), row(W_Q), row(W_KV), row(W_KV), row(W_QM)],
        out_shape=[jax.ShapeDtypeStruct((t, CONV_CH), F32), jax.ShapeDtypeStruct((t, W_Q), BF16),
                   jax.ShapeDtypeStruct((t, W_KV), F32), jax.ShapeDtypeStruct((t, W_KV), F32),
                   jax.ShapeDtypeStruct((t, W_QM), BF16)],
        compiler_params=_params("parallel"),
        name="in_proj",
    )(x, g, w)


def _ln_silu(y, g, b):
    mu = jnp.mean(y, axis=-1, keepdims=True)
    var = jnp.mean(jnp.square(y - mu), axis=-1, keepdims=True)
    y = (y - mu) * lax.rsqrt(var + EPS) * g + b
    return y * jax.nn.sigmoid(y)


def _softmax_rows(s):
    e = jnp.exp(s - jnp.max(s, axis=-1, keepdims=True))
    return e / jnp.sum(e, axis=-1, keepdims=True)


def _branches_prompt_kernel(sinks_ref, u_ref, uh_ref, q_ref, k_ref, kp_ref, v_ref, vp_ref, qm_ref,
                            mk_ref, mv_ref, bias_ref, dww_ref, dwb_ref, lng_ref, lnb_ref,
                            conv_ref, swa_ref, mem_ref, ubuf, ushift, kbuf, vbuf, *, tq):
    i = pl.program_id(1)
    first = i == 0

    ubuf[0:CONV_HALO, :] = jnp.where(first, 0.0, uh_ref[...])
    ubuf[CONV_HALO:CONV_HALO + tq, :] = u_ref[...]
    nshift = ushift.shape[1]
    for b in range(1, SUBLANES):
        ushift[b] = ubuf[b:b + nshift, :]
    rb = 64
    off = CONV_HALO - (CONV_WIDTH - 1)
    for r in range(tq // rb):
        acc = jnp.broadcast_to(dwb_ref[...], (rb, CONV_CH))
        for j in range(CONV_WIDTH):
            a, b = divmod(off + j, SUBLANES)
            start = r * rb + a * SUBLANES
            rows = ubuf[start:start + rb, :] if b == 0 else ushift[b, start:start + rb, :]
            acc = acc + rows * dww_ref[j:j + 1, :]
        conv_ref[r * rb:(r + 1) * rb, :] = _ln_silu(acc, lng_ref[...], lnb_ref[...]).astype(BF16)

    lane = lax.broadcasted_iota(jnp.int32, (WINDOW + tq, LANES), 1)
    lo = lane < SWA_HEAD_DIM
    for src_ref, prev_ref, buf in ((k_ref, kp_ref, kbuf), (v_ref, vp_ref, vbuf)):
        full = jnp.concatenate([jnp.where(first, 0.0, prev_ref[...]), src_ref[...]], axis=0)
        rolled = pltpu.roll(full, SWA_HEAD_DIM, 1)
        buf[0] = jnp.where(lo, full, 0.0).astype(BF16)
        buf[1] = jnp.where(lo, 0.0, rolled).astype(BF16)
        buf[2] = jnp.where(lo, rolled, 0.0).astype(BF16)
        buf[3] = jnp.where(lo, 0.0, full).astype(BF16)

    qi = lax.broadcasted_iota(jnp.int32, (WINDOW, 2 * WINDOW), 0)
    ki = lax.broadcasted_iota(jnp.int32, (WINDOW, 2 * WINDOW), 1)
    dist = WINDOW + qi - ki
    band = (dist >= 0) & (dist <= WINDOW)
    nqb = tq // WINDOW
    for jb in range(nqb):
        r0 = jb * WINDOW
        kmin = jnp.where(i * nqb + jb > 0, 0, WINDOW)
        mask = band & (ki >= kmin)
        for p in range(SWA_HEADS // 2):
            g = p // 2
            qp = q_ref[r0:r0 + WINDOW, LANES * p:LANES * (p + 1)]
            o = None
            for half in range(2):
                h = 2 * p + half
                s = _dot_nt(qp, kbuf[2 * g + half, r0:r0 + 2 * WINDOW, :]) + bias_ref[h]
                s = jnp.where(mask, s, NEG_INF)
                sink = sinks_ref[h]
                m = jnp.maximum(jnp.max(s, axis=-1, keepdims=True), sink)
                pr = jnp.exp(s - m)
                pr = pr / (jnp.sum(pr, axis=-1, keepdims=True) + jnp.exp(sink - m))
                t = _dot(pr.astype(BF16), vbuf[2 * g + half, r0:r0 + 2 * WINDOW, :])
                o = t if o is None else o + t
            swa_ref[r0:r0 + WINDOW, LANES * p:LANES * (p + 1)] = o.astype(BF16)

    for hm in range(MEM_HEADS):
        sl = slice(hm * MEM_HEAD_DIM, (hm + 1) * MEM_HEAD_DIM)
        w = _softmax_rows(_dot_nt(qm_ref[:, sl], mk_ref[:, sl].astype(BF16)) * MEM_SCALE)
        mem_ref[:, sl] = _dot(w.astype(BF16), mv_ref[:, sl].astype(BF16)).astype(BF16)


def _branches_prompt(sinks, u, q, k, v, qm, mkv, bias, dww, dwb, lng, lnb, batch, seq, tq):
    t = batch * seq
    nq = seq // tq
    row = lambda n: pl.BlockSpec((tq, n), lambda b, i: (b * nq + i, 0))
    halo = lambda rows, n: pl.BlockSpec(
        (rows, n), lambda b, i: (jnp.maximum(b * (seq // rows) + i * (tq // rows) - 1, 0), 0))
    return pl.pallas_call(
        functools.partial(_branches_prompt_kernel, tq=tq),
        grid=(batch, nq),
        in_specs=[pl.BlockSpec(memory_space=pltpu.SMEM),
                  row(CONV_CH), halo(CONV_HALO, CONV_CH), row(W_Q),
                  row(W_KV), halo(WINDOW, W_KV), row(W_KV), halo(WINDOW, W_KV), row(W_QM),
                  pl.BlockSpec((MEM_LEN, W_QM), lambda b, i: (b, 0)),
                  pl.BlockSpec((MEM_LEN, W_QM), lambda b, i: (b, 1)),
                  _const_spec((SWA_HEADS, WINDOW, 2 * WINDOW)),
                  _const_spec((CONV_WIDTH, CONV_CH)), _const_spec((1, CONV_CH)),
                  _const_spec((1, CONV_CH)), _const_spec((1, CONV_CH))],
        out_specs=[row(CONV_CH), row(W_Q), row(W_QM)],
        out_shape=[jax.ShapeDtypeStruct((t, CONV_CH), BF16), jax.ShapeDtypeStruct((t, W_Q), BF16),
                   jax.ShapeDtypeStruct((t, W_QM), BF16)],
        scratch_shapes=[pltpu.VMEM((CONV_HALO + tq, CONV_CH), F32),
                        pltpu.VMEM((SUBLANES, CONV_HALO + tq - SUBLANES, CONV_CH), F32),
                        pltpu.VMEM((4, WINDOW + tq, LANES), BF16),
                        pltpu.VMEM((4, WINDOW + tq, LANES), BF16)],
        compiler_params=_params("parallel", "arbitrary"),
        name="branches_prompt",
    )(sinks, u, u, q, k, k, v, v, qm, mkv, mkv, bias, dww, dwb, lng, lnb)


def _branches_sample_kernel(sinks_ref, u_ref, cc_ref, qx_ref, kn_ref, vn_ref, ck_ref, cv_ref, qm_ref,
                            cmk_ref, cmv_ref, bias_ref, bias0_ref, dww_ref, dwb_ref, lng_ref, lnb_ref,
                            conv_ref, swa_ref, mem_ref, *, nb):
    hist = CONV_WIDTH - 1
    y = jnp.sum(cc_ref[...] * dww_ref[0:hist, :][None], axis=1)
    y = y + u_ref[...] * dww_ref[hist:hist + 1, :] + dwb_ref[...]
    conv_ref[...] = _ln_silu(y, lng_ref[...], lnb_ref[...]).astype(BF16)

    sink = sinks_ref[...]
    for n in range(nb):
        qx = qx_ref[n]
        s = _dot_nt(qx, ck_ref[n].astype(BF16)) + bias_ref[...]
        kn = kn_ref[n:n + 1, :].astype(BF16).astype(F32)
        s_new = jnp.sum(qx.astype(F32) * kn, axis=-1, keepdims=True) + bias0_ref[...]
        m = jnp.maximum(jnp.maximum(jnp.max(s, axis=-1, keepdims=True), s_new), sink)
        pr, pr_new = jnp.exp(s - m), jnp.exp(s_new - m)
        den = jnp.sum(pr, axis=-1, keepdims=True) + pr_new + jnp.exp(sink - m)
        vn = vn_ref[n:n + 1, :].astype(BF16).astype(F32)
        o = _dot((pr / den).astype(BF16), cv_ref[n].astype(BF16))
        swa_ref[n] = o + (pr_new / den).astype(BF16).astype(F32) * vn

        bf = lambda a: a.astype(BF16).astype(F32)
        sm = jnp.sum(bf(cmk_ref[n]) * qm_ref[n].astype(F32)[None], axis=-1, keepdims=True) * MEM_SCALE
        e = jnp.exp(sm - jnp.max(sm, axis=0, keepdims=True))
        w = e / jnp.sum(e, axis=0, keepdims=True)
        mem_ref[n] = jnp.sum(bf(w) * bf(cmv_ref[n]), axis=0).astype(BF16)


def _branches_sample(sinks, u, cache_conv, qx, kn, vn, ck, cv, qm, cmk, cmv, bias, bias0,
                     dww, dwb, lng, lnb, nb):
    n = u.shape[0]
    row = lambda c: pl.BlockSpec((nb, c), lambda i: (i, 0))
    blk3 = lambda a, c: pl.BlockSpec((nb, a, c), lambda i: (i, 0, 0))
    return pl.pallas_call(
        functools.partial(_branches_sample_kernel, nb=nb),
        grid=(n // nb,),
        in_specs=[_const_spec((SWA_HEADS, 1)),
                  row(CONV_CH), blk3(CONV_WIDTH - 1, CONV_CH), blk3(SWA_HEADS, LANES),
                  row(W_KV), row(W_KV), blk3(WINDOW, W_KV), blk3(WINDOW, W_KV),
                  blk3(MEM_HEADS, MEM_HEAD_DIM),
                  pl.BlockSpec((nb, MEM_LEN, MEM_HEADS, MEM_HEAD_DIM), lambda i: (i, 0, 0, 0)),
                  pl.BlockSpec((nb, MEM_LEN, MEM_HEADS, MEM_HEAD_DIM), lambda i: (i, 0, 0, 0)),
                  _const_spec((SWA_HEADS, WINDOW)), _const_spec((SWA_HEADS, 1)),
                  _const_spec((CONV_WIDTH, CONV_CH)), _const_spec((1, CONV_CH)),
                  _const_spec((1, CONV_CH)), _const_spec((1, CONV_CH))],
        out_specs=[row(CONV_CH), blk3(SWA_HEADS, LANES), blk3(MEM_HEADS, MEM_HEAD_DIM)],
        out_shape=[jax.ShapeDtypeStruct((n, CONV_CH), BF16),
                   jax.ShapeDtypeStruct((n, SWA_HEADS, LANES), F32),
                   jax.ShapeDtypeStruct((n, MEM_HEADS, MEM_HEAD_DIM), BF16)],
        compiler_params=_params("parallel"),
        name="branches_sample",
    )(sinks, u, cache_conv, qx, kn, vn, ck, cv, qm, cmk, cmv, bias, bias0, dww, dwb, lng, lnb)


def _merge_kernel(x_ref, g_ref, conv_ref, swa_ref, mem_ref, wg_ref, wco_ref, wso_ref, wmo_ref, wo_ref,
                  o_ref):
    x = x_ref[...]
    h = _rms(x, g_ref[...]).astype(BF16)
    merged = None
    for br, (a_ref, w_ref) in enumerate(((conv_ref, wco_ref), (swa_ref, wso_ref), (mem_ref, wmo_ref))):
        gate = jax.nn.sigmoid(_dot(h, wg_ref[:, br * D_MODEL:(br + 1) * D_MODEL]))
        term = gate * _dot(a_ref[...], w_ref[...])
        merged = term if merged is None else merged + term
    o_ref[...] = x + _dot(merged.astype(BF16), wo_ref[...])


def _merge(x, g, conv, swa, mem, wg, wco, wso, wmo, wo, tm):
    t = x.shape[0]
    row = lambda n: pl.BlockSpec((tm, n), lambda i: (i, 0))
    return pl.pallas_call(
        _merge_kernel,
        grid=(t // tm,),
        in_specs=[row(D_MODEL), _const_spec((1, D_MODEL)), row(CONV_CH), row(W_Q), row(W_QM),
                  _const_spec((D_MODEL, 3 * D_MODEL)), _const_spec((CONV_CH, D_MODEL)),
                  _const_spec((W_Q, D_MODEL)), _const_spec((W_QM, D_MODEL)),
                  _const_spec((D_MODEL, D_MODEL))],
        out_specs=row(D_MODEL),
        out_shape=jax.ShapeDtypeStruct((t, D_MODEL), F32),
        compiler_params=_params("parallel"),
        name="merge",
    )(x, g, conv, swa, mem, wg, wco, wso, wmo, wo)


def _gelu_x2(x):
    return x * (1.0 + lax.erf(x * (2.0 ** -0.5)))


def _top_values(arrs, count, with_rank=False):
    out = []
    ranks = [jnp.full(a.shape, float(count), F32) for a in arrs] if with_rank else None
    for it in range(count):
        m = jnp.max(functools.reduce(jnp.maximum, arrs), axis=0, keepdims=True)
        out.append(m)
        hit = [a == m for a in arrs]
        if with_rank:
            ranks = [jnp.where(hh, float(it), rk) for hh, rk in zip(hit, ranks)]
        arrs = [jnp.where(hh, -jnp.inf, a) for hh, a in zip(hit, arrs)]
    return (out, ranks) if with_rank else out


def _oddeven_merge_sort(lo, hi):
    def merge(lo, hi, r):
        step = 2 * r
        if step < hi - lo:
            yield from merge(lo, hi, step)
            yield from merge(lo + r, hi, step)
            yield from ((i, i + r) for i in range(lo + r, hi - r, step))
        else:
            yield (lo, lo + r)

    if hi > lo:
        mid = lo + (hi - lo) // 2
        yield from _oddeven_merge_sort(lo, mid)
        yield from _oddeven_merge_sort(mid + 1, hi)
        yield from merge(lo, hi, 1)


def _compare_exchange(a, i, j):
    if a[j] is None:
        return
    if a[i] is None:
        a[i], a[j] = a[j], None
    else:
        a[i], a[j] = jnp.maximum(a[i], a[j]), jnp.minimum(a[i], a[j])


def _top16_sorted(blocks):
    n = PEER_TOPK
    assert n // 2 < len(blocks) <= n and blocks[0].shape[0] == SUBLANES
    a = list(blocks) + [None] * (n - len(blocks))
    larger = lambda x, y: x if y is None else y if x is None else jnp.maximum(x, y)
    for i, j in _oddeven_merge_sort(0, n - 1):
        _compare_exchange(a, i, j)
    for shift in (4, 2, 1):
        b = [None if x is None else pltpu.roll(x, shift, 0) for x in a]
        a = [larger(a[i], b[n - 1 - i]) for i in range(n)]
        d = n // 2
        while d:
            for i in range(n):
                if not i & d:
                    _compare_exchange(a, i, i + d)
            d //= 2
    return a


def _rank_in_sorted(x, top):
    assert len(top) == 16
    rank, lo = None, [0] * 1
    conds = []
    for width in (8, 4, 2, 1):
        idx = [b + width - 1 for b in lo]
        thr = [top[i] for i in idx]
        for c in reversed(conds):
            thr = [jnp.where(c, thr[2 * k], thr[2 * k + 1]) for k in range(len(thr) // 2)]
        c = x >= thr[0]
        step = jnp.where(c, 0.0, float(width))
        rank = step if rank is None else rank + step
        conds.append(c)
        lo = [b + off for b in lo for off in (0, width)]
    return jnp.where(x >= top[15], rank, 16.0)


def _peer_route(h, qt, keys_ref, l_s, e0_s, r1_s, e1_s, sv0_s, sv1_s):
    nk, tb = PEER_N_KEYS, qt.shape[1]
    r = 2 * h * PEER_DK_HALF
    for lg in range(tb // LANES):
        lanes = slice(lg * LANES, (lg + 1) * LANES)
        blocks = lambda a: [a[SUBLANES * j:SUBLANES * (j + 1), :] for j in range(nk // SUBLANES)]
        s0 = blocks(_dot(keys_ref[h, 0], qt[r:r + PEER_DK_HALF, lanes]))
        s1 = blocks(_dot(keys_ref[h, 1], qt[r + PEER_DK_HALF:r + 2 * PEER_DK_HALF, lanes]))
        top0, top1 = _top16_sorted(s0), _top16_sorted(s1)
        for j in range(PEER_TOPK):
            sv0_s[j:j + 1, lanes] = top0[j][0:1]
            sv1_s[j:j + 1, lanes] = top1[j][0:1]
        a0, a1 = sv0_s[0:8, lanes], sv0_s[8:16, lanes]
        b0, b1 = sv1_s[0:8, lanes], sv1_s[8:16, lanes]
        cands = [a0[0:1] + b0, a0[0:1] + b1] + [a0[a:a + 1] + b0 for a in range(1, 8)] + [a1 + b0[0:1]]
        best = _top16_sorted(cands)
        sel = [cd >= best[PEER_TOPK - 1] for cd in cands]
        z = functools.reduce(jnp.add, [jnp.where(sl, jnp.exp(cd - best[0]), 0.0) for sl, cd in zip(sel, cands)])
        z = jnp.sum(z, axis=0, keepdims=True)
        cnt = [jnp.sum(jnp.where(sl, 1.0, 0.0), axis=0, keepdims=True) for sl in sel[:9]]
        counts = [cnt[0] + cnt[1]] + cnt[2:9]
        tail = jnp.where(sel[9], 1.0, 0.0)
        counts = [jnp.broadcast_to(counts[a] if a < 8 else tail[a - 8:a - 7], (SUBLANES, LANES))
                  for a in range(PEER_TOPK)]
        lrow, rank1 = [], []
        for x0, x1 in zip(s0, s1):
            lx = jnp.zeros_like(x0)
            for a in reversed(range(PEER_TOPK)):
                lx = jnp.where(x0 >= top0[a], counts[a], lx)
            lrow.append(lx)
            rank1.append(_rank_in_sorted(x1, top1))
        l_s[h, :, lanes] = jnp.concatenate(lrow, axis=0)
        e0_s[h, :, lanes] = jnp.exp(jnp.concatenate(s0, axis=0) - top0[0][0:1]) / z * 0.5
        r1_s[h, :, lanes] = jnp.concatenate(rank1, axis=0).astype(BF16)
        e1_s[h, :, lanes] = jnp.exp(jnp.concatenate(s1, axis=0) - top1[0][0:1]).astype(BF16)


def _peer_gate_chunk(ci, at_ref, ct_ref, lrow_s, erow_s, r1_s, e1_s, cb):
    nk, tb = PEER_N_KEYS, ct_ref.shape[1]
    tile = (nk // BF16_ROWS, BF16_ROWS, tb)
    row = lambda ref, h, j: jnp.broadcast_to(ref[ci, h, j:j + 1, :], tile[1:]).astype(BF16)[None]
    for j in range(cb):
        g = None
        for h in range(PEER_HEADS):
            term = jnp.where(r1_s[h].reshape(tile) < row(lrow_s, h, j), e1_s[h].reshape(tile),
                             jnp.zeros((), BF16)) * row(erow_s, h, j)
            g = term if g is None else g + term
        rows = slice((ci * cb + j) * nk, (ci * cb + j + 1) * nk)
        ct_ref[rows, :] = (_gelu_x2(at_ref[j * nk:(j + 1) * nk, :].reshape(tile)) * g).reshape(nk, tb)


def _peer_kernel(x_ref, g2_ref, gf_ref, wqt_ref, keys_ref, wda_ref, wdb_ref, wut_ref, o_ref,
                 h_s, l_s, e0_s, r1_s, e1_s, sv0_s, sv1_s, lrow_s, erow_s, at0_s, at1_s, ct_s, acc_s, *, cb):
    s = pl.program_id(1)

    @pl.when(s == 0)
    def _first():
        h_s[...] = _rms(x_ref[...], g2_ref[...]).T.astype(BF16)
        qt = _dot(wqt_ref[...], h_s[...]).astype(BF16)
        for h in range(PEER_HEADS):
            _peer_route(h, qt, keys_ref, l_s, e0_s, r1_s, e1_s, sv0_s, sv1_s)
        at0_s[...] = _dot(wda_ref[...], h_s[...]).astype(BF16)
        acc_s[...] = jnp.zeros_like(acc_s)

    @pl.when(s > 0)
    def _steady():
        for ci in range(2):
            base = pl.multiple_of((2 * s - 2 + ci) * cb, cb)
            for h in range(PEER_HEADS):
                lrow_s[ci, h] = l_s[h, pl.ds(base, cb), :]
                erow_s[ci, h] = e0_s[h, pl.ds(base, cb), :]
        gate = functools.partial(_peer_gate_chunk, ct_ref=ct_s, lrow_s=lrow_s, erow_s=erow_s,
                                 r1_s=r1_s, e1_s=e1_s, cb=cb)
        gate(0, at0_s)
        at1_s[...] = _dot(wda_ref[...], h_s[...]).astype(BF16)
        at0_s[...] = _dot(wdb_ref[...], h_s[...]).astype(BF16)
        gate(1, at1_s)
        ec = cb * PEER_N_KEYS
        acc_s[...] += _dot(wut_ref[:, :ec], ct_s[:ec, :]) + _dot(wut_ref[:, ec:], ct_s[ec:, :])

    @pl.when(s == pl.num_programs(1) - 1)
    def _last():
        y = x_ref[...] + acc_s[...].T
        o_ref[...] = _rms(y, gf_ref[...])


def _peer(x, g2, gf, wqt, keys, wd, wut, tb, cb):
    t = x.shape[0]
    ec = cb * PEER_N_KEYS
    nch = wd.shape[0] // ec
    assert nch % 2 == 0 and tb % LANES == 0
    stat = lambda dt: pltpu.VMEM((PEER_HEADS, PEER_N_KEYS, tb), dt)
    return pl.pallas_call(
        functools.partial(_peer_kernel, cb=cb),
        grid=(t // tb, nch // 2 + 1),
        in_specs=[pl.BlockSpec((tb, D_MODEL), lambda i, s: (i, 0)),
                  _const_spec((1, D_MODEL)), _const_spec((1, D_MODEL)),
                  _const_spec((D_MODEL, D_MODEL)),
                  _const_spec((PEER_HEADS, 2, PEER_N_KEYS, PEER_DK_HALF)),
                  pl.BlockSpec((ec, D_MODEL), lambda i, s: (jnp.maximum(2 * s - 1, 0), 0)),
                  pl.BlockSpec((ec, D_MODEL), lambda i, s: (jnp.minimum(2 * s, nch - 1), 0)),
                  pl.BlockSpec((D_MODEL, 2 * ec), lambda i, s: (0, jnp.maximum(s - 1, 0)))],
        out_specs=pl.BlockSpec((tb, D_MODEL), lambda i, s: (i, 0)),
        out_shape=jax.ShapeDtypeStruct((t, D_MODEL), F32),
        scratch_shapes=[pltpu.VMEM((D_MODEL, tb), BF16), stat(F32), stat(F32), stat(BF16), stat(BF16),
                        pltpu.VMEM((PEER_TOPK, tb), F32), pltpu.VMEM((PEER_TOPK, tb), F32),
                        pltpu.VMEM((2, PEER_HEADS, cb, tb), F32), pltpu.VMEM((2, PEER_HEADS, cb, tb), F32),
                        pltpu.VMEM((ec, tb), BF16), pltpu.VMEM((ec, tb), BF16),
                        pltpu.VMEM((2 * ec, tb), BF16), pltpu.VMEM((D_MODEL, tb), F32)],
        compiler_params=_params("parallel", "arbitrary"),
        name="peer",
    )(x, g2, gf, wqt, keys, wd, wd, wut)


def _rel_bucket(dist):
    n = jnp.maximum(dist, 0)
    max_exact = REL_BUCKETS // 2
    nf = jnp.maximum(n, 1).astype(F32)
    large = max_exact + (jnp.log(nf / max_exact) / jnp.log(REL_MAX_DIST / max_exact)
                         * (REL_BUCKETS - max_exact)).astype(jnp.int32)
    return jnp.where(n < max_exact, n, jnp.minimum(large, REL_BUCKETS - 1))


TOKEN_TILE = 512
SAMPLE_TILE = 8
PEER_CHUNK_ROW_BLOCKS = 8


def _tile(t, cap=TOKEN_TILE):
    tm = min(t, cap)
    assert t % tm == 0, (t, tm)
    return tm


def kernel(x_prompt, x_sample, cache_conv, cache_swa_k, cache_swa_v, cache_mem_k, cache_mem_v, mem_prompt, rel_bias_table, norm1_g, w_in, conv_dw_w, conv_dw_b, conv_ln_g, conv_ln_b, w_conv_out, swa_sinks, w_swa_out, mem_norm_g, w_mem_kv, w_mem_out, w_out, norm2_g, peer_w_q, peer_keys, peer_w_down, peer_w_up, final_norm_g):
    assert w_in.shape[0] == 1, "single layer"
    batch, seq, _ = x_prompt.shape
    nsamp = x_sample.shape[0]
    assert x_sample.shape[1] == 1 and seq % WINDOW == 0
    row = lambda a: a.reshape(1, -1)

    w_proj = w_in[0, :, :W_PROJ].astype(BF16)
    w_gate = w_in[0, :, W_PROJ:].astype(BF16)
    g1, g2, gf = row(norm1_g[0]), row(norm2_g[0]), row(final_norm_g)
    dww, dwb = conv_dw_w[0], row(conv_dw_b[0])
    lng, lnb = row(conv_ln_g[0]), row(conv_ln_b[0])
    wco, wso = w_conv_out[0].astype(BF16), w_swa_out[0].astype(BF16)
    wmo, wo = w_mem_out[0].astype(BF16), w_out[0].astype(BF16)
    wqt = peer_w_q[0].T.astype(BF16)
    keys = peer_keys[0].astype(BF16)
    wd = peer_w_down.reshape(-1, D_MODEL).astype(BF16)
    wut = peer_w_up.reshape(-1, D_MODEL).astype(BF16).T
    sinks = swa_sinks[0]

    qi = jnp.arange(WINDOW)[:, None]
    ki = jnp.arange(2 * WINDOW)[None, :]
    def table_rows(dist):
        onehot = (_rel_bucket(dist)[..., None] == jnp.arange(REL_BUCKETS)).astype(F32)
        return jnp.einsum("...b,bh->h...", onehot, rel_bias_table.astype(F32),
                          precision=lax.Precision.HIGHEST)

    bias_p = table_rows(WINDOW + qi - ki)
    bias_s = table_rows(WINDOW - jnp.arange(WINDOW))
    bias_0 = table_rows(jnp.zeros((1,), jnp.int32))

    xp = x_prompt.reshape(batch * seq, D_MODEL)
    mkv = _norm_matmul(mem_prompt.reshape(batch * MEM_LEN, D_MODEL), row(mem_norm_g[0]),
                       w_mem_kv[0].astype(BF16), MEM_LEN)
    u_p, q_p, k_p, v_p, qm_p = _in_proj(xp, g1, w_proj, _tile(batch * seq))
    tq = _tile(seq)
    conv_p, swa_p, mem_p = _branches_prompt(sinks, u_p, q_p, k_p, v_p, qm_p, mkv, bias_p,
                                            dww, dwb, lng, lnb, batch, seq, tq)
    x2_p = _merge(xp, g1, conv_p, swa_p, mem_p, w_gate, wco, wso, wmo, wo, _tile(batch * seq))
    y_p = _peer(x2_p, g2, gf, wqt, keys, wd, wut, _tile(batch * seq), PEER_CHUNK_ROW_BLOCKS)

    xs = x_sample.reshape(nsamp, D_MODEL)
    u_s, q_s, k_s, v_s, qm_s = _in_proj(xs, g1, w_proj, _tile(nsamp))
    q4 = q_s.reshape(nsamp, SWA_KV_HEADS, SWA_HEADS // SWA_KV_HEADS, SWA_HEAD_DIM)
    zq = jnp.zeros_like(q4[:, 0])
    qx = jnp.concatenate([jnp.concatenate([q4[:, 0], zq], -1), jnp.concatenate([zq, q4[:, 1]], -1)], 1)
    ck = cache_swa_k.reshape(nsamp, WINDOW, W_KV)
    cv = cache_swa_v.reshape(nsamp, WINDOW, W_KV)
    cmk = cache_mem_k.reshape(nsamp, MEM_LEN, MEM_HEADS, MEM_HEAD_DIM)
    cmv = cache_mem_v.reshape(nsamp, MEM_LEN, MEM_HEADS, MEM_HEAD_DIM)
    cconv = cache_conv.reshape(nsamp, CONV_WIDTH - 1, CONV_CH)
    conv_s, swa_x, mem_x = _branches_sample(sinks.reshape(SWA_HEADS, 1), u_s, cconv, qx, k_s, v_s, ck, cv,
                                            qm_s.reshape(nsamp, MEM_HEADS, MEM_HEAD_DIM), cmk, cmv,
                                            bias_s, bias_0, dww, dwb, lng, lnb, _tile(nsamp, SAMPLE_TILE))
    mem_s = mem_x.reshape(nsamp, W_QM)
    sx = swa_x.reshape(nsamp, SWA_KV_HEADS, SWA_HEADS // SWA_KV_HEADS, SWA_KV_HEADS, SWA_HEAD_DIM)
    swa_s = jnp.stack([sx[:, g, :, g] for g in range(SWA_KV_HEADS)], 1).reshape(nsamp, W_Q).astype(BF16)
    x2_s = _merge(xs, g1, conv_s, swa_s, mem_s, w_gate, wco, wso, wmo, wo, _tile(nsamp))
    y_s = _peer(x2_s, g2, gf, wqt, keys, wd, wut, _tile(nsamp), PEER_CHUNK_ROW_BLOCKS)

    hist = CONV_WIDTH - 1
    kv_shape = (SWA_KV_HEADS, SWA_HEAD_DIM)
    mkv5 = mkv.reshape(batch, MEM_LEN, 2, MEM_HEADS, MEM_HEAD_DIM)
    conv_state_p = u_p.reshape(batch, seq, CONV_CH)[:, -hist:]
    swa_k_p = k_p.reshape(batch, seq, *kv_shape)[:, -WINDOW:]
    swa_v_p = v_p.reshape(batch, seq, *kv_shape)[:, -WINDOW:]
    conv_state_s = jnp.concatenate([cache_conv[0][:, 1:], u_s[:, None, :]], axis=1)
    swa_k_s = jnp.concatenate([cache_swa_k[0][:, 1:], k_s.reshape(nsamp, 1, *kv_shape)], axis=1)
    swa_v_s = jnp.concatenate([cache_swa_v[0][:, 1:], v_s.reshape(nsamp, 1, *kv_shape)], axis=1)
    return (y_p.reshape(batch, seq, D_MODEL), y_s.reshape(nsamp, 1, D_MODEL),
            conv_state_p[None], swa_k_p[None], swa_v_p[None],
            mkv5[:, :, 0][None], mkv5[:, :, 1][None],
            conv_state_s[None], swa_k_s[None], swa_v_s[None])
```
